```python
import jax, jax.numpy as jnp
from jax import lax
import numpy as np

D_MODEL = 1024
BATCH = 8
SEQ = 2048
DEPTH = 2
DEC_BATCH = 128
DEC_SEQ = 1
PAST_LEN = 16384
PAGE_SIZE = 128

D_MIX = D_MODEL
W_A = D_MIX // 2
W_B = D_MIX - W_A
POOL_WINDOWS = (2, 4, 8, 16)
N_POOL_GROUPS = len(POOL_WINDOWS)
POOL_GROUP_DIM = W_A // N_POOL_GROUPS
POOL_BUF = max(POOL_WINDOWS) - 1
CHUNK = 128
N_HEADS_B = 4
HEAD_DIM_B = W_B // N_HEADS_B
D_IN = 2 * W_A + 3 * W_B
EPS = 1e-6

kernel_name = "hymba_pool_chunkmlp_step"


def rmsnorm(x, g):
    xf = x.astype(jnp.float32)
    y = xf * lax.rsqrt(jnp.mean(xf * xf, axis=-1, keepdims=True) + EPS)
    return (y * g.astype(jnp.float32)).astype(x.dtype)


def pool_mix(ctx, n_past, start_pos, w_pool, scale):
    B, L, W = ctx.shape
    T = L - n_past
    cf = ctx.astype(jnp.float32)
    cs = jnp.concatenate([jnp.zeros((B, 1, W), jnp.float32), jnp.cumsum(cf, axis=1)], axis=1)
    idx = np.arange(T) + n_past
    pos = start_pos + np.arange(T)
    means = []
    for g, w in enumerate(POOL_WINDOWS):
        sl = slice(g * POOL_GROUP_DIM, (g + 1) * POOL_GROUP_DIM)
        hi = cs[:, idx + 1, sl]
        lo = cs[:, np.maximum(idx + 1 - w, 0), sl]
        cnt = jnp.asarray(np.minimum(pos + 1, w), jnp.float32)[None, :, None]
        means.append((hi - lo) / cnt)
    d = (jnp.concatenate(means, axis=-1) - cf[:, n_past:]).astype(ctx.dtype)
    d = d.reshape(B, T, N_POOL_GROUPS, POOL_GROUP_DIM)
    y = jnp.einsum('btgc,gcd->btgd', d, w_pool).reshape(B, T, W_A)
    return y * scale


def chunk_mix(u, v, w_s, b_s):
    B, T, H, Dh = v.shape
    L = min(T, CHUNK)
    nc = -(-T // L)
    Tp = nc * L
    mask = np.tril(np.ones((L, L), dtype=bool))
    ws = jnp.where(mask[None], w_s[:, :L, :L], 0)
    vp = jnp.pad(v, ((0, 0), (0, Tp - T), (0, 0), (0, 0))).reshape(B, nc, L, H, Dh)
    s = jnp.einsum('hij,bcjhd->bcihd', ws, vp) + jnp.transpose(b_s[:, :L])[None, None, :, :, None]
    s = s.reshape(B, Tp, H, Dh)[:, :T]
    return u * s


def layer(x, past_pool, start_pos, norm_g, w_in, w_pool, pool_scale, v_norm_g, w_s, b_s, w_out):
    B, T, _ = x.shape
    h = rmsnorm(x, norm_g)
    proj = jnp.einsum('btd,de->bte', h, w_in)
    a_in = proj[..., :W_A]
    a_gate = proj[..., W_A:2 * W_A]
    u = proj[..., 2 * W_A:2 * W_A + W_B]
    v = proj[..., 2 * W_A + W_B:2 * W_A + 2 * W_B]
    b_gate = proj[..., 2 * W_A + 2 * W_B:]
    ctx = jnp.concatenate([past_pool.astype(a_in.dtype), a_in], axis=1)
    a_out = pool_mix(ctx, past_pool.shape[1], start_pos, w_pool, pool_scale) * jax.nn.silu(a_gate)
    vn = rmsnorm(v.reshape(B, T, N_HEADS_B, HEAD_DIM_B), v_norm_g.reshape(N_HEADS_B, HEAD_DIM_B))
    b_out = chunk_mix(u.reshape(B, T, N_HEADS_B, HEAD_DIM_B), vn, w_s, b_s).reshape(B, T, W_B)
    b_out = b_out * jax.nn.silu(b_gate)
    y = x + jnp.einsum('bte,ed->btd', jnp.concatenate([a_out, b_out], axis=-1), w_out)
    new_pool = ctx[:, -POOL_BUF:]
    n_last = (T - 1) % CHUNK + 1
    new_v = vn.reshape(B, T, W_B)[:, T - n_last:]
    return y, new_pool, new_v


def setup_inputs(seed: int = 0) -> dict:
    key = jax.random.key(seed)
    ks = jax.random.split(key, 12)
    nrm = jax.random.normal
    return {
        "x_prompt": nrm(ks[0], (BATCH, SEQ, D_MODEL), jnp.float32),
        "x_sample": nrm(ks[1], (DEC_BATCH, DEC_SEQ, D_MODEL), jnp.float32),
        "state_pool": nrm(ks[2], (DEPTH, DEC_BATCH, POOL_BUF, W_A), jnp.float32),
        "norm_g": 1.0 + 0.1 * nrm(ks[3], (DEPTH, D_MODEL), jnp.float32),
        "w_in": nrm(ks[4], (DEPTH, D_MODEL, D_IN), jnp.float32) * D_MODEL ** -0.5,
        "w_pool": nrm(ks[5], (DEPTH, N_POOL_GROUPS, POOL_GROUP_DIM, POOL_GROUP_DIM), jnp.float32) * POOL_GROUP_DIM ** -0.5,
        "pool_scale": 1.0 + 0.1 * nrm(ks[6], (DEPTH, W_A), jnp.float32),
        "v_norm_g": 1.0 + 0.1 * nrm(ks[7], (DEPTH, W_B), jnp.float32),
        "w_s": nrm(ks[8], (DEPTH, N_HEADS_B, CHUNK, CHUNK), jnp.float32) * CHUNK ** -0.5,
        "b_s": 1.0 + 0.1 * nrm(ks[9], (DEPTH, N_HEADS_B, CHUNK), jnp.float32),
        "w_out": nrm(ks[10], (DEPTH, D_MIX, D_MODEL), jnp.float32) * D_MIX ** -0.5,
        "final_norm_g": 1.0 + 0.1 * nrm(ks[11], (D_MODEL,), jnp.float32),
    }


def reference(x_prompt, x_sample, state_pool, norm_g, w_in, w_pool, pool_scale, v_norm_g, w_s, b_s, w_out, final_norm_g):
    yp, ys = x_prompt, x_sample
    empty_past = jnp.zeros((x_prompt.shape[0], 0, W_A), x_prompt.dtype)
    pool_p, pool_s, cv_p, cv_s = [], [], [], []
    for l in range(DEPTH):
        params = (norm_g[l], w_in[l], w_pool[l], pool_scale[l], v_norm_g[l], w_s[l], b_s[l], w_out[l])
        yp, npp, nvp = layer(yp, empty_past, 0, *params)
        ys, nps, nvs = layer(ys, state_pool[l], PAST_LEN, *params)
        pool_p.append(npp); pool_s.append(nps); cv_p.append(nvp); cv_s.append(nvs)
    y_prompt = rmsnorm(yp, final_norm_g)
    y_sample = rmsnorm(ys, final_norm_g)
    pool_prompt = jnp.stack(pool_p, axis=0)
    pool_sample = jnp.stack(pool_s, axis=0)
    chunk_v_prompt = jnp.stack(cv_p, axis=0)
    chunk_v_sample = jnp.stack(cv_s, axis=0)
    return (y_prompt, y_sample, pool_prompt, pool_sample, chunk_v_prompt, chunk_v_sample)
```

```python
import functools

import jax
import jax.numpy as jnp
from jax import lax
from jax.experimental import pallas as pl
from jax.experimental.pallas import tpu as pltpu

D_MODEL = 1024
DEPTH = 2
W_A = 512
W_B = 512
POOL_WINDOWS = (2, 4, 8, 16)
N_POOL_GROUPS = len(POOL_WINDOWS)
POOL_GROUP_DIM = W_A // N_POOL_GROUPS
POOL_BUF = max(POOL_WINDOWS) - 1
CHUNK = 128
N_HEADS_B = 4
HEAD_DIM_B = W_B // N_HEADS_B
EPS = 1e-6
SAMPLE_PAST_LEN = 16384

COL_A_IN = 0
COL_A_GATE = W_A
COL_U = 2 * W_A
COL_V = 2 * W_A + W_B
COL_B_GATE = 2 * W_A + 2 * W_B

HIST = 16
ROW_TILE = 512
VMEM_LIMIT_BYTES = 52 * 1024 * 1024

BF16 = jnp.bfloat16
F32 = jnp.float32


def _dot(a, b):
    return jnp.dot(a.astype(BF16), b.astype(BF16), preferred_element_type=F32)


def _rmsnorm(x, g):
    ms = jnp.mean(x * x, axis=-1, keepdims=True)
    return (x * lax.rsqrt(ms + EPS)) * g


def _silu(x):
    return x * (1.0 / (1.0 + jnp.exp(-x)))


def _head_rmsnorm(v, g):
    parts = []
    for h in range(N_HEADS_B):
        sl = slice(h * HEAD_DIM_B, (h + 1) * HEAD_DIM_B)
        parts.append(_rmsnorm(v[:, sl], g[:, sl]))
    return jnp.concatenate(parts, axis=-1)


def _masked_ws(ws_ref, l):
    row = lax.broadcasted_iota(jnp.int32, (CHUNK, CHUNK), 0)
    col = lax.broadcasted_iota(jnp.int32, (CHUNK, CHUNK), 1)
    keep = col <= row
    return [jnp.where(keep, ws_ref[l, h], 0.0).astype(BF16) for h in range(N_HEADS_B)]


def _prompt_kernel(x_ref, norm_g_ref, w_in_ref, w_pool_ref, pool_scale_ref, v_norm_g_ref, ws_ref,
                   bias_ref, w_out_ref, fin_g_ref,
                   y_ref, pool_ref, cv_ref,
                   a_hist, s2_hist, s4_hist, s8_hist):
    t = pl.program_id(1)
    last_t = pl.num_programs(1) - 1
    tm = x_ref.shape[1]
    n_chunks = tm // CHUNK

    @pl.when(t == 0)
    def _():
        a_hist[:, 0:HIST, :] = jnp.zeros((DEPTH, HIST, a_hist.shape[2]), F32)
        s2_hist[:, 0:HIST, :] = jnp.zeros((DEPTH, HIST, s2_hist.shape[2]), F32)
        s4_hist[:, 0:HIST, :] = jnp.zeros((DEPTH, HIST, s4_hist.shape[2]), F32)
        s8_hist[:, 0:HIST, :] = jnp.zeros((DEPTH, HIST, s8_hist.shape[2]), F32)

    pos1 = lax.broadcasted_iota(jnp.int32, (HIST, POOL_GROUP_DIM), 0) + t * tm + 1

    x = x_ref[0]
    for l in range(DEPTH):
        h = _rmsnorm(x, norm_g_ref[l]).astype(BF16)

        a_in = _dot(h, w_in_ref[l, :, COL_A_IN:COL_A_IN + W_A])
        a_hist[l, HIST:HIST + tm, :] = a_in
        s2 = a_in + a_hist[l, HIST - 1:HIST - 1 + tm, :]
        s2_hist[l, HIST:HIST + tm, :] = s2[:, POOL_GROUP_DIM:]
        s4 = s2[:, POOL_GROUP_DIM:] + s2_hist[l, HIST - 2:HIST - 2 + tm, :]
        s4_hist[l, HIST:HIST + tm, :] = s4[:, POOL_GROUP_DIM:]
        s8 = s4[:, POOL_GROUP_DIM:] + s4_hist[l, HIST - 4:HIST - 4 + tm, :]
        s8_hist[l, HIST:HIST + tm, :] = s8[:, POOL_GROUP_DIM:]
        s16 = s8[:, POOL_GROUP_DIM:] + s8_hist[l, HIST - 8:HIST - 8 + tm, :]
        win_sums = (s2[:, :POOL_GROUP_DIM], s4[:, :POOL_GROUP_DIM], s8[:, :POOL_GROUP_DIM], s16)

        a_hist[l, 0:HIST, :] = a_hist[l, tm:tm + HIST, :]
        s2_hist[l, 0:HIST, :] = s2_hist[l, tm:tm + HIST, :]
        s4_hist[l, 0:HIST, :] = s4_hist[l, tm:tm + HIST, :]
        s8_hist[l, 0:HIST, :] = s8_hist[l, tm:tm + HIST, :]

        @pl.when(t == last_t)
        def _():
            pool_ref[l, 0] = a_hist[l, 0:HIST, :]

        a_gate = _dot(h, w_in_ref[l, :, COL_A_GATE:COL_A_GATE + W_A])
        a_parts = []
        for g, w in enumerate(POOL_WINDOWS):
            sl = slice(g * POOL_GROUP_DIM, (g + 1) * POOL_GROUP_DIM)
            ssum = win_sums[g]
            inv_head = 1.0 / jnp.minimum(pos1, w).astype(F32)
            mean = jnp.concatenate([ssum[:HIST] * inv_head, ssum[HIST:] * (1.0 / w)], axis=0)
            d = mean - a_in[:, sl]
            yg = _dot(d, w_pool_ref[l, g]) * pool_scale_ref[l][:, sl]
            a_parts.append(yg * _silu(a_gate[:, sl]))
        a_out = jnp.concatenate(a_parts, axis=-1).astype(BF16)

        v = _dot(h, w_in_ref[l, :, COL_V:COL_V + W_B])
        vn = _head_rmsnorm(v, v_norm_g_ref[l])

        @pl.when(t == last_t)
        def _():
            cv_ref[l, 0] = vn[tm - CHUNK:, :]

        vn_b = vn.astype(BF16)
        ws_b = _masked_ws(ws_ref, l)
        s_heads = []
        for hh in range(N_HEADS_B):
            sl = slice(hh * HEAD_DIM_B, (hh + 1) * HEAD_DIM_B)
            rhs = jnp.concatenate([vn_b[c * CHUNK:(c + 1) * CHUNK, sl] for c in range(n_chunks)], axis=1)
            res = jnp.dot(ws_b[hh], rhs, preferred_element_type=F32)
            s_heads.append(jnp.concatenate(
                [res[:, c * HEAD_DIM_B:(c + 1) * HEAD_DIM_B] for c in range(n_chunks)], axis=0))
        bias = bias_ref[l]
        s = jnp.concatenate(s_heads, axis=-1) + jnp.concatenate([bias] * n_chunks, axis=0)
        u = _dot(h, w_in_ref[l, :, COL_U:COL_U + W_B])
        b_gate = _dot(h, w_in_ref[l, :, COL_B_GATE:COL_B_GATE + W_B])
        b_out = ((u * s) * _silu(b_gate)).astype(BF16)

        x = x + (_dot(a_out, w_out_ref[l, 0:W_A, :]) + _dot(b_out, w_out_ref[l, W_A:, :]))

    y_ref[0] = _rmsnorm(x, fin_g_ref[...])


def _sample_kernel(x_ref, sp_ref, norm_g_ref, w_in_ref, w_pool_ref, pool_scale_ref, v_norm_g_ref,
                   ws00_ref, b0_ref, w_out_ref, fin_g_ref,
                   y_ref, a_in_ref, vn_ref, *, past_len):
    x = x_ref[...]
    for l in range(DEPTH):
        h = _rmsnorm(x, norm_g_ref[l]).astype(BF16)
        a_in = _dot(h, w_in_ref[l, :, COL_A_IN:COL_A_IN + W_A])
        a_in_ref[l] = a_in
        a_gate = _dot(h, w_in_ref[l, :, COL_A_GATE:COL_A_GATE + W_A])
        a_parts = []
        for g, w in enumerate(POOL_WINDOWS):
            sl = slice(g * POOL_GROUP_DIM, (g + 1) * POOL_GROUP_DIM)
            ssum = a_in[:, sl]
            for k in range(1, w):
                ssum = ssum + sp_ref[l, POOL_BUF - k, :, sl]
            cnt = float(min(past_len + 1, w))
            d = ssum / cnt - a_in[:, sl]
            yg = _dot(d, w_pool_ref[l, g]) * pool_scale_ref[l][:, sl]
            a_parts.append(yg * _silu(a_gate[:, sl]))
        a_out = jnp.concatenate(a_parts, axis=-1).astype(BF16)

        v = _dot(h, w_in_ref[l, :, COL_V:COL_V + W_B])
        vn = _head_rmsnorm(v, v_norm_g_ref[l])
        vn_ref[l] = vn
        s = ws00_ref[l] * vn + b0_ref[l]
        u = _dot(h, w_in_ref[l, :, COL_U:COL_U + W_B])
        b_gate = _dot(h, w_in_ref[l, :, COL_B_GATE:COL_B_GATE + W_B])
        b_out = ((u * s) * _silu(b_gate)).astype(BF16)

        x = x + (_dot(a_out, w_out_ref[l, 0:W_A, :]) + _dot(b_out, w_out_ref[l, W_A:, :]))
    y_ref[...] = _rmsnorm(x, fin_g_ref[...])


def _const_spec(shape):
    zeros = (0,) * len(shape)
    return pl.BlockSpec(shape, lambda b, t: zeros, pipeline_mode=pl.Buffered(1))


def _prompt_call(x, norm_g, w_in_b, w_pool_b, pool_scale, v_norm_g, w_s, bias_full, w_out_b, fin_g):
    batch, seq, d = x.shape
    tm = ROW_TILE
    assert seq % tm == 0 and tm % CHUNK == 0 and tm >= HIST
    consts = (norm_g, w_in_b, w_pool_b, pool_scale, v_norm_g, w_s, bias_full, w_out_b, fin_g)
    return pl.pallas_call(
        _prompt_kernel,
        grid=(batch, seq // tm),
        in_specs=[pl.BlockSpec((1, tm, d), lambda b, t: (b, t, 0))] + [_const_spec(c.shape) for c in consts],
        out_specs=[
            pl.BlockSpec((1, tm, d), lambda b, t: (b, t, 0)),
            pl.BlockSpec((DEPTH, 1, HIST, W_A), lambda b, t: (0, b, 0, 0)),
            pl.BlockSpec((DEPTH, 1, CHUNK, W_B), lambda b, t: (0, b, 0, 0)),
        ],
        out_shape=[
            jax.ShapeDtypeStruct((batch, seq, d), F32),
            jax.ShapeDtypeStruct((DEPTH, batch, HIST, W_A), F32),
            jax.ShapeDtypeStruct((DEPTH, batch, CHUNK, W_B), F32),
        ],
        scratch_shapes=[
            pltpu.VMEM((DEPTH, HIST + tm, W_A), F32),
            pltpu.VMEM((DEPTH, HIST + tm, W_A - POOL_GROUP_DIM), F32),
            pltpu.VMEM((DEPTH, HIST + tm, W_A - 2 * POOL_GROUP_DIM), F32),
            pltpu.VMEM((DEPTH, HIST + tm, W_A - 3 * POOL_GROUP_DIM), F32),
        ],
        compiler_params=pltpu.CompilerParams(
            dimension_semantics=("arbitrary", "arbitrary"),
            vmem_limit_bytes=VMEM_LIMIT_BYTES,
        ),
        name="prompt_trunk",
    )(x, *consts)


def _sample_call(x, sp_t, norm_g, w_in_b, w_pool_b, pool_scale, v_norm_g, ws00, b0, w_out_b, fin_g, past_len):
    n, d = x.shape
    return pl.pallas_call(
        functools.partial(_sample_kernel, past_len=past_len),
        out_shape=[
            jax.ShapeDtypeStruct((n, d), F32),
            jax.ShapeDtypeStruct((DEPTH, n, W_A), F32),
            jax.ShapeDtypeStruct((DEPTH, n, W_B), F32),
        ],
        compiler_params=pltpu.CompilerParams(vmem_limit_bytes=VMEM_LIMIT_BYTES),
        name="sample_trunk",
    )(x, sp_t, norm_g, w_in_b, w_pool_b, pool_scale, v_norm_g, ws00, b0, w_out_b, fin_g)


def kernel(x_prompt, x_sample, state_pool, norm_g, w_in, w_pool, pool_scale, v_norm_g, w_s, b_s, w_out, final_norm_g):
    dec_batch, dec_seq, _ = x_sample.shape
    assert dec_seq == 1 and state_pool.shape[2] == POOL_BUF
    past_len = SAMPLE_PAST_LEN

    w_in_b = w_in.astype(BF16)
    w_out_b = w_out.astype(BF16)
    w_pool_b = w_pool.astype(BF16)
    norm_g3 = norm_g.reshape(DEPTH, 1, D_MODEL)
    pool_scale3 = pool_scale.reshape(DEPTH, 1, W_A)
    v_norm_g3 = v_norm_g.reshape(DEPTH, 1, W_B)
    fin_g2 = final_norm_g.reshape(1, D_MODEL)
    bias_full = jnp.repeat(jnp.transpose(b_s, (0, 2, 1)), HEAD_DIM_B, axis=2)

    y_prompt, pool_p16, chunk_v_prompt = _prompt_call(
        x_prompt, norm_g3, w_in_b, w_pool_b, pool_scale3, v_norm_g3, w_s, bias_full, w_out_b, fin_g2)
    pool_prompt = pool_p16[:, :, HIST - POOL_BUF:, :]

    sp_t = jnp.transpose(state_pool, (0, 2, 1, 3))
    ws00 = jnp.repeat(w_s[:, :, 0, 0], HEAD_DIM_B, axis=1).reshape(DEPTH, 1, W_B)
    b0 = jnp.repeat(b_s[:, :, 0], HEAD_DIM_B, axis=1).reshape(DEPTH, 1, W_B)
    y_s, a_in_s, vn_s = _sample_call(
        x_sample.reshape(dec_batch, D_MODEL), sp_t, norm_g3, w_in_b, w_pool_b, pool_scale3, v_norm_g3,
        ws00, b0, w_out_b, fin_g2, past_len)
    y_sample = y_s.reshape(dec_batch, 1, D_MODEL)
    pool_sample = jnp.concatenate([state_pool[:, :, 1:, :], a_in_s[:, :, None, :]], axis=2)
    chunk_v_sample = vn_s[:, :, None, :]
    return (y_prompt, y_sample, pool_prompt, pool_sample, chunk_v_prompt, chunk_v_sample)
```

```python
import functools

import jax
import jax.numpy as jnp
from jax import lax
from jax.experimental import pallas as pl
from jax.experimental.pallas import tpu as pltpu

D_MODEL = 1024
DEPTH = 2
W_A = 512
W_B = 512
POOL_WINDOWS = (2, 4, 8, 16)
N_POOL_GROUPS = len(POOL_WINDOWS)
POOL_GROUP_DIM = W_A // N_POOL_GROUPS
POOL_BUF = max(POOL_WINDOWS) - 1
CHUNK = 128
N_HEADS_B = 4
HEAD_DIM_B = W_B // N_HEADS_B
EPS = 1e-6
SAMPLE_PAST_LEN = 16384

COL_A_IN = 0
COL_A_GATE = W_A
COL_U = 2 * W_A
COL_V = 2 * W_A + W_B
COL_B_GATE = 2 * W_A + 2 * W_B

HIST = 16
ROW_TILE = 512
VMEM_LIMIT_BYTES = 52 * 1024 * 1024

BF16 = jnp.bfloat16
F32 = jnp.float32


def _dot(a, b):
    return jnp.dot(a.astype(BF16), b.astype(BF16), preferred_element_type=F32)


def _rmsnorm(x, g):
    ms = jnp.mean(x * x, axis=-1, keepdims=True)
    return (x * lax.rsqrt(ms + EPS)) * g


def _silu(x):
    return x * (1.0 / (1.0 + jnp.exp(-x)))


def _head_rmsnorm(v, g):
    parts = []
    for h in range(N_HEADS_B):
        sl = slice(h * HEAD_DIM_B, (h + 1) * HEAD_DIM_B)
        parts.append(_rmsnorm(v[:, sl], g[:, sl]))
    return jnp.concatenate(parts, axis=-1)


def _masked_ws(ws_ref, l):
    row = lax.broadcasted_iota(jnp.int32, (CHUNK, CHUNK), 0)
    col = lax.broadcasted_iota(jnp.int32, (CHUNK, CHUNK), 1)
    keep = col <= row
    return [jnp.where(keep, ws_ref[l, h], 0.0).astype(BF16) for h in range(N_HEADS_B)]


def _prompt_kernel(x_ref, norm_g_ref, w_in_ref, w_pool_ref, pool_scale_ref, v_norm_g_ref, ws_ref,
                   bias_ref, w_out_ref, fin_g_ref,
                   y_ref, pool_ref, cv_ref,
                   a_hist, s2_hist, s4_hist, s8_hist):
    t = pl.program_id(1)
    last_t = pl.num_programs(1) - 1
    tm = x_ref.shape[1]
    n_chunks = tm // CHUNK

    @pl.when(t == 0)
    def _():
        a_hist[:, 0:HIST, :] = jnp.zeros((DEPTH, HIST, a_hist.shape[2]), F32)
        s2_hist[:, 0:HIST, :] = jnp.zeros((DEPTH, HIST, s2_hist.shape[2]), F32)
        s4_hist[:, 0:HIST, :] = jnp.zeros((DEPTH, HIST, s4_hist.shape[2]), F32)
        s8_hist[:, 0:HIST, :] = jnp.zeros((DEPTH, HIST, s8_hist.shape[2]), F32)

    pos1 = lax.broadcasted_iota(jnp.int32, (HIST, POOL_GROUP_DIM), 0) + t * tm + 1

    x = x_ref[0]
    for l in range(DEPTH):
        h = _rmsnorm(x, norm_g_ref[l]).astype(BF16)

        v = _dot(h, w_in_ref[l, :, COL_V:COL_V + W_B])
        a_in = _dot(h, w_in_ref[l, :, COL_A_IN:COL_A_IN + W_A])
        a_gate = _dot(h, w_in_ref[l, :, COL_A_GATE:COL_A_GATE + W_A])
        u = _dot(h, w_in_ref[l, :, COL_U:COL_U + W_B])

        vn = _head_rmsnorm(v, v_norm_g_ref[l])

        cv_ref[l, 0] = vn[tm - CHUNK:, :]

        vn_b = vn.astype(BF16)

        a_hist[l, HIST:HIST + tm, :] = a_in
        s2 = a_in + a_hist[l, HIST - 1:HIST - 1 + tm, :]
        s2_hist[l, HIST:HIST + tm, :] = s2[:, POOL_GROUP_DIM:]
        s4 = s2[:, POOL_GROUP_DIM:] + s2_hist[l, HIST - 2:HIST - 2 + tm, :]
        s4_hist[l, HIST:HIST + tm, :] = s4[:, POOL_GROUP_DIM:]
        s8 = s4[:, POOL_GROUP_DIM:] + s4_hist[l, HIST - 4:HIST - 4 + tm, :]
        s8_hist[l, HIST:HIST + tm, :] = s8[:, POOL_GROUP_DIM:]
        s16 = s8[:, POOL_GROUP_DIM:] + s8_hist[l, HIST - 8:HIST - 8 + tm, :]
        win_sums = (s2[:, :POOL_GROUP_DIM], s4[:, :POOL_GROUP_DIM], s8[:, :POOL_GROUP_DIM], s16)

        a_hist[l, 0:HIST, :] = a_hist[l, tm:tm + HIST, :]
        s2_hist[l, 0:HIST, :] = s2_hist[l, tm:tm + HIST, :]
        s4_hist[l, 0:HIST, :] = s4_hist[l, tm:tm + HIST, :]
        s8_hist[l, 0:HIST, :] = s8_hist[l, tm:tm + HIST, :]

        pool_ref[l, 0] = a_in[tm - HIST:, :]

        d_parts = []
        for g, w in enumerate(POOL_WINDOWS):
            sl = slice(g * POOL_GROUP_DIM, (g + 1) * POOL_GROUP_DIM)
            ssum = win_sums[g]
            inv_head = 1.0 / jnp.minimum(pos1, w).astype(F32)
            mean = jnp.concatenate([ssum[:HIST] * inv_head, ssum[HIST:] * (1.0 / w)], axis=0)
            d_parts.append((mean - a_in[:, sl]).astype(BF16))

        ws_b = _masked_ws(ws_ref, l)
        s_heads = []
        for hh in range(N_HEADS_B):
            sl = slice(hh * HEAD_DIM_B, (hh + 1) * HEAD_DIM_B)
            rhs = jnp.concatenate([vn_b[c * CHUNK:(c + 1) * CHUNK, sl] for c in range(n_chunks)], axis=1)
            res = jnp.dot(ws_b[hh], rhs, preferred_element_type=F32)
            s_heads.append(jnp.concatenate(
                [res[:, c * HEAD_DIM_B:(c + 1) * HEAD_DIM_B] for c in range(n_chunks)], axis=0))

        y_parts = [_dot(d_parts[g], w_pool_ref[l, g]) for g in range(N_POOL_GROUPS)]

        b_gate = _dot(h, w_in_ref[l, :, COL_B_GATE:COL_B_GATE + W_B])

        a_out = jnp.concatenate(
            [(y_parts[g] * pool_scale_ref[l][:, g * POOL_GROUP_DIM:(g + 1) * POOL_GROUP_DIM])
             * _silu(a_gate[:, g * POOL_GROUP_DIM:(g + 1) * POOL_GROUP_DIM]) for g in range(N_POOL_GROUPS)],
            axis=-1).astype(BF16)
        out_a = _dot(a_out, w_out_ref[l, 0:W_A, :])

        bias = bias_ref[l]
        s = jnp.concatenate(s_heads, axis=-1) + jnp.concatenate([bias] * n_chunks, axis=0)
        b_out = ((u * s) * _silu(b_gate)).astype(BF16)
        x = x + (out_a + _dot(b_out, w_out_ref[l, W_A:, :]))

    y_ref[0] = _rmsnorm(x, fin_g_ref[...])


def _sample_kernel(x_ref, sp_ref, norm_g_ref, w_in_ref, w_pool_ref, pool_scale_ref, v_norm_g_ref,
                   ws00_ref, b0_ref, w_out_ref, fin_g_ref,
                   y_ref, a_in_ref, vn_ref, *, past_len):
    x = x_ref[...]
    for l in range(DEPTH):
        h = _rmsnorm(x, norm_g_ref[l]).astype(BF16)
        a_in = _dot(h, w_in_ref[l, :, COL_A_IN:COL_A_IN + W_A])
        a_in_ref[l] = a_in
        a_gate = _dot(h, w_in_ref[l, :, COL_A_GATE:COL_A_GATE + W_A])
        a_parts = []
        for g, w in enumerate(POOL_WINDOWS):
            sl = slice(g * POOL_GROUP_DIM, (g + 1) * POOL_GROUP_DIM)
            ssum = a_in[:, sl]
            for k in range(1, w):
                ssum = ssum + sp_ref[l, POOL_BUF - k, :, sl]
            cnt = float(min(past_len + 1, w))
            d = ssum / cnt - a_in[:, sl]
            yg = _dot(d, w_pool_ref[l, g]) * pool_scale_ref[l][:, sl]
            a_parts.append(yg * _silu(a_gate[:, sl]))
        a_out = jnp.concatenate(a_parts, axis=-1).astype(BF16)

        v = _dot(h, w_in_ref[l, :, COL_V:COL_V + W_B])
        vn = _head_rmsnorm(v, v_norm_g_ref[l])
        vn_ref[l] = vn
        s = ws00_ref[l] * vn + b0_ref[l]
        u = _dot(h, w_in_ref[l, :, COL_U:COL_U + W_B])
        b_gate = _dot(h, w_in_ref[l, :, COL_B_GATE:COL_B_GATE + W_B])
        b_out = ((u * s) * _silu(b_gate)).astype(BF16)

        x = x + (_dot(a_out, w_out_ref[l, 0:W_A, :]) + _dot(b_out, w_out_ref[l, W_A:, :]))
    y_ref[...] = _rmsnorm(x, fin_g_ref[...])


def _const_spec(shape):
    zeros = (0,) * len(shape)
    return pl.BlockSpec(shape, lambda b, t: zeros, pipeline_mode=pl.Buffered(1))


def _prompt_call(x, norm_g, w_in_b, w_pool_b, pool_scale, v_norm_g, w_s, bias_full, w_out_b, fin_g):
    batch, seq, d = x.shape
    tm = ROW_TILE
    assert seq % tm == 0 and tm % CHUNK == 0 and tm >= HIST
    consts = (norm_g, w_in_b, w_pool_b, pool_scale, v_norm_g, w_s, bias_full, w_out_b, fin_g)
    return pl.pallas_call(
        _prompt_kernel,
        grid=(batch, seq // tm),
        in_specs=[pl.BlockSpec((1, tm, d), lambda b, t: (b, t, 0))] + [_const_spec(c.shape) for c in consts],
        out_specs=[
            pl.BlockSpec((1, tm, d), lambda b, t: (b, t, 0)),
            pl.BlockSpec((DEPTH, 1, HIST, W_A), lambda b, t: (0, b, 0, 0)),
            pl.BlockSpec((DEPTH, 1, CHUNK, W_B), lambda b, t: (0, b, 0, 0)),
        ],
        out_shape=[
            jax.ShapeDtypeStruct((batch, seq, d), F32),
            jax.ShapeDtypeStruct((DEPTH, batch, HIST, W_A), F32),
            jax.ShapeDtypeStruct((DEPTH, batch, CHUNK, W_B), F32),
        ],
        scratch_shapes=[
            pltpu.VMEM((DEPTH, HIST + tm, W_A), F32),
            pltpu.VMEM((DEPTH, HIST + tm, W_A - POOL_GROUP_DIM), F32),
            pltpu.VMEM((DEPTH, HIST + tm, W_A - 2 * POOL_GROUP_DIM), F32),
            pltpu.VMEM((DEPTH, HIST + tm, W_A - 3 * POOL_GROUP_DIM), F32),
        ],
        compiler_params=pltpu.CompilerParams(
            dimension_semantics=("arbitrary", "arbitrary"),
            vmem_limit_bytes=VMEM_LIMIT_BYTES,
        ),
        name="prompt_trunk",
    )(x, *consts)


def _sample_call(x, sp_t, norm_g, w_in_b, w_pool_b, pool_scale, v_norm_g, ws00, b0, w_out_b, fin_g, past_len):
    n, d = x.shape
    return pl.pallas_call(
        functools.partial(_sample_kernel, past_len=past_len),
        out_shape=[
            jax.ShapeDtypeStruct((n, d), F32),
            jax.ShapeDtypeStruct((DEPTH, n, W_A), F32),
            jax.ShapeDtypeStruct((DEPTH, n, W_B), F32),
        ],
        compiler_params=pltpu.CompilerParams(vmem_limit_bytes=VMEM_LIMIT_BYTES),
        name="sample_trunk",
    )(x, sp_t, norm_g, w_in_b, w_pool_b, pool_scale, v_norm_g, ws00, b0, w_out_b, fin_g)


def kernel(x_prompt, x_sample, state_pool, norm_g, w_in, w_pool, pool_scale, v_norm_g, w_s, b_s, w_out, final_norm_g):
    dec_batch, dec_seq, _ = x_sample.shape
    assert dec_seq == 1 and state_pool.shape[2] == POOL_BUF
    past_len = SAMPLE_PAST_LEN

    w_in_b = w_in.astype(BF16)
    w_out_b = w_out.astype(BF16)
    w_pool_b = w_pool.astype(BF16)
    norm_g3 = norm_g.reshape(DEPTH, 1, D_MODEL)
    pool_scale3 = pool_scale.reshape(DEPTH, 1, W_A)
    v_norm_g3 = v_norm_g.reshape(DEPTH, 1, W_B)
    fin_g2 = final_norm_g.reshape(1, D_MODEL)
    bias_full = jnp.repeat(jnp.transpose(b_s, (0, 2, 1)), HEAD_DIM_B, axis=2)

    y_prompt, pool_p16, chunk_v_prompt = _prompt_call(
        x_prompt, norm_g3, w_in_b, w_pool_b, pool_scale3, v_norm_g3, w_s, bias_full, w_out_b, fin_g2)
    pool_prompt = pool_p16[:, :, HIST - POOL_BUF:, :]

    sp_t = jnp.transpose(state_pool, (0, 2, 1, 3))
    ws00 = jnp.repeat(w_s[:, :, 0, 0], HEAD_DIM_B, axis=1).reshape(DEPTH, 1, W_B)
    b0 = jnp.repeat(b_s[:, :, 0], HEAD_DIM_B, axis=1).reshape(DEPTH, 1, W_B)
    y_s, a_in_s, vn_s = _sample_call(
        x_sample.reshape(dec_batch, D_MODEL), sp_t, norm_g3, w_in_b, w_pool_b, pool_scale3, v_norm_g3,
        ws00, b0, w_out_b, fin_g2, past_len)
    y_sample = y_s.reshape(dec_batch, 1, D_MODEL)
    pool_sample = jnp.concatenate([state_pool[:, :, 1:, :], a_in_s[:, :, None, :]], axis=2)
    chunk_v_sample = vn_s[:, :, None, :]
    return (y_prompt, y_sample, pool_prompt, pool_sample, chunk_v_prompt, chunk_v_sample)
```

```python
import functools

import jax
import jax.numpy as jnp
from jax import lax
from jax.experimental import pallas as pl
from jax.experimental.pallas import tpu as pltpu

D_MODEL = 1024
DEPTH = 2
W_A = 512
W_B = 512
POOL_WINDOWS = (2, 4, 8, 16)
N_POOL_GROUPS = len(POOL_WINDOWS)
POOL_GROUP_DIM = W_A // N_POOL_GROUPS
POOL_BUF = max(POOL_WINDOWS) - 1
CHUNK = 128
N_HEADS_B = 4
HEAD_DIM_B = W_B // N_HEADS_B
EPS = 1e-6
SAMPLE_PAST_LEN = 16384

COL_A_IN = 0
COL_A_GATE = W_A
COL_U = 2 * W_A
COL_V = 2 * W_A + W_B
COL_B_GATE = 2 * W_A + 2 * W_B

HIST = 16
ROW_TILE = 512
VMEM_LIMIT_BYTES = 58 * 1024 * 1024

F32 = jnp.float32


def _dot(a, b):
    return jnp.dot(a, b, preferred_element_type=F32)


def _rmsnorm(x, g):
    ms = jnp.mean(x * x, axis=-1, keepdims=True)
    return (x * lax.rsqrt(ms + EPS)) * g


def _silu(x):
    return x * (1.0 / (1.0 + jnp.exp(-x)))


def _head_rmsnorm(v, g):
    parts = []
    for h in range(N_HEADS_B):
        sl = slice(h * HEAD_DIM_B, (h + 1) * HEAD_DIM_B)
        parts.append(_rmsnorm(v[:, sl], g[:, sl]))
    return jnp.concatenate(parts, axis=-1)


def _masked_ws(ws_ref, l):
    row = lax.broadcasted_iota(jnp.int32, (CHUNK, CHUNK), 0)
    col = lax.broadcasted_iota(jnp.int32, (CHUNK, CHUNK), 1)
    keep = col <= row
    return [jnp.where(keep, ws_ref[l, h], 0.0) for h in range(N_HEADS_B)]


def _prompt_kernel(x_ref, norm_g_ref, w_in_ref, w_pool_ref, pool_scale_ref, v_norm_g_ref, ws_ref,
                   bias_ref, w_out_ref, fin_g_ref,
                   y_ref, pool_ref, cv_ref,
                   a_hist, s2_hist, s4_hist, s8_hist):
    t = pl.program_id(1)
    tm = x_ref.shape[1]
    n_chunks = tm // CHUNK

    @pl.when(t == 0)
    def _():
        a_hist[:, 0:HIST, :] = jnp.zeros((DEPTH, HIST, a_hist.shape[2]), F32)
        s2_hist[:, 0:HIST, :] = jnp.zeros((DEPTH, HIST, s2_hist.shape[2]), F32)
        s4_hist[:, 0:HIST, :] = jnp.zeros((DEPTH, HIST, s4_hist.shape[2]), F32)
        s8_hist[:, 0:HIST, :] = jnp.zeros((DEPTH, HIST, s8_hist.shape[2]), F32)

    pos1 = lax.broadcasted_iota(jnp.int32, (HIST, POOL_GROUP_DIM), 0) + t * tm + 1

    x = x_ref[0]
    for l in range(DEPTH):
        h = _rmsnorm(x, norm_g_ref[l])

        v = _dot(h, w_in_ref[l, :, COL_V:COL_V + W_B])
        a_in = _dot(h, w_in_ref[l, :, COL_A_IN:COL_A_IN + W_A])
        a_gate = _dot(h, w_in_ref[l, :, COL_A_GATE:COL_A_GATE + W_A])
        u = _dot(h, w_in_ref[l, :, COL_U:COL_U + W_B])

        vn = _head_rmsnorm(v, v_norm_g_ref[l])

        cv_ref[l, 0] = vn[tm - CHUNK:, :]

        a_hist[l, HIST:HIST + tm, :] = a_in
        s2 = a_in + a_hist[l, HIST - 1:HIST - 1 + tm, :]
        s2_hist[l, HIST:HIST + tm, :] = s2[:, POOL_GROUP_DIM:]
        s4 = s2[:, POOL_GROUP_DIM:] + s2_hist[l, HIST - 2:HIST - 2 + tm, :]
        s4_hist[l, HIST:HIST + tm, :] = s4[:, POOL_GROUP_DIM:]
        s8 = s4[:, POOL_GROUP_DIM:] + s4_hist[l, HIST - 4:HIST - 4 + tm, :]
        s8_hist[l, HIST:HIST + tm, :] = s8[:, POOL_GROUP_DIM:]
        s16 = s8[:, POOL_GROUP_DIM:] + s8_hist[l, HIST - 8:HIST - 8 + tm, :]
        win_sums = (s2[:, :POOL_GROUP_DIM], s4[:, :POOL_GROUP_DIM], s8[:, :POOL_GROUP_DIM], s16)

        a_hist[l, 0:HIST, :] = a_hist[l, tm:tm + HIST, :]
        s2_hist[l, 0:HIST, :] = s2_hist[l, tm:tm + HIST, :]
        s4_hist[l, 0:HIST, :] = s4_hist[l, tm:tm + HIST, :]
        s8_hist[l, 0:HIST, :] = s8_hist[l, tm:tm + HIST, :]

        pool_ref[l, 0] = a_in[tm - HIST:, :]

        d_parts = []
        for g, w in enumerate(POOL_WINDOWS):
            sl = slice(g * POOL_GROUP_DIM, (g + 1) * POOL_GROUP_DIM)
            ssum = win_sums[g]
            inv_head = 1.0 / jnp.minimum(pos1, w).astype(F32)
            mean = jnp.concatenate([ssum[:HIST] * inv_head, ssum[HIST:] * (1.0 / w)], axis=0)
            d_parts.append(mean - a_in[:, sl])

        ws_m = _masked_ws(ws_ref, l)
        s_heads = []
        for hh in range(N_HEADS_B):
            sl = slice(hh * HEAD_DIM_B, (hh + 1) * HEAD_DIM_B)
            rhs = jnp.concatenate([vn[c * CHUNK:(c + 1) * CHUNK, sl] for c in range(n_chunks)], axis=1)
            res = _dot(ws_m[hh], rhs)
            s_heads.append(jnp.concatenate(
                [res[:, c * HEAD_DIM_B:(c + 1) * HEAD_DIM_B] for c in range(n_chunks)], axis=0))

        y_parts = [_dot(d_parts[g], w_pool_ref[l, g]) for g in range(N_POOL_GROUPS)]

        b_gate = _dot(h, w_in_ref[l, :, COL_B_GATE:COL_B_GATE + W_B])

        a_out = jnp.concatenate(
            [(y_parts[g] * pool_scale_ref[l][:, g * POOL_GROUP_DIM:(g + 1) * POOL_GROUP_DIM])
             * _silu(a_gate[:, g * POOL_GROUP_DIM:(g + 1) * POOL_GROUP_DIM]) for g in range(N_POOL_GROUPS)],
            axis=-1)
        out_a = _dot(a_out, w_out_ref[l, 0:W_A, :])

        bias = bias_ref[l]
        s = jnp.concatenate(s_heads, axis=-1) + jnp.concatenate([bias] * n_chunks, axis=0)
        b_out = (u * s) * _silu(b_gate)
        x = x + (out_a + _dot(b_out, w_out_ref[l, W_A:, :]))

    y_ref[0] = _rmsnorm(x, fin_g_ref[...])


def _sample_kernel(x_ref, sp_ref, norm_g_ref, w_in_ref, w_pool_ref, pool_scale_ref, v_norm_g_ref,
                   ws00_ref, b0_ref, w_out_ref, fin_g_ref,
                   y_ref, a_in_ref, vn_ref, *, past_len):
    x = x_ref[...]
    for l in range(DEPTH):
        h = _rmsnorm(x, norm_g_ref[l])
        a_in = _dot(h, w_in_ref[l, :, COL_A_IN:COL_A_IN + W_A])
        a_in_ref[l] = a_in
        a_gate = _dot(h, w_in_ref[l, :, COL_A_GATE:COL_A_GATE + W_A])
        a_parts = []
        for g, w in enumerate(POOL_WINDOWS):
            sl = slice(g * POOL_GROUP_DIM, (g + 1) * POOL_GROUP_DIM)
            ssum = a_in[:, sl]
            for k in range(1, w):
                ssum = ssum + sp_ref[l, POOL_BUF - k, :, sl]
            cnt = float(min(past_len + 1, w))
            d = ssum / cnt - a_in[:, sl]
            yg = _dot(d, w_pool_ref[l, g]) * pool_scale_ref[l][:, sl]
            a_parts.append(yg * _silu(a_gate[:, sl]))
        a_out = jnp.concatenate(a_parts, axis=-1)

        v = _dot(h, w_in_ref[l, :, COL_V:COL_V + W_B])
        vn = _head_rmsnorm(v, v_norm_g_ref[l])
        vn_ref[l] = vn
        s = ws00_ref[l] * vn + b0_ref[l]
        u = _dot(h, w_in_ref[l, :, COL_U:COL_U + W_B])
        b_gate = _dot(h, w_in_ref[l, :, COL_B_GATE:COL_B_GATE + W_B])
        b_out = (u * s) * _silu(b_gate)

        x = x + (_dot(a_out, w_out_ref[l, 0:W_A, :]) + _dot(b_out, w_out_ref[l, W_A:, :]))
    y_ref[...] = _rmsnorm(x, fin_g_ref[...])


def _const_spec(shape):
    zeros = (0,) * len(shape)
    return pl.BlockSpec(shape, lambda b, t: zeros, pipeline_mode=pl.Buffered(1))


def _prompt_call(x, norm_g, w_in, w_pool, pool_scale, v_norm_g, w_s, bias_full, w_out, fin_g):
    batch, seq, d = x.shape
    tm = ROW_TILE
    assert seq % tm == 0 and tm % CHUNK == 0 and tm >= HIST
    consts = (norm_g, w_in, w_pool, pool_scale, v_norm_g, w_s, bias_full, w_out, fin_g)
    return pl.pallas_call(
        _prompt_kernel,
        grid=(batch, seq // tm),
        in_specs=[pl.BlockSpec((1, tm, d), lambda b, t: (b, t, 0))] + [_const_spec(c.shape) for c in consts],
        out_specs=[
            pl.BlockSpec((1, tm, d), lambda b, t: (b, t, 0)),
            pl.BlockSpec((DEPTH, 1, HIST, W_A), lambda b, t: (0, b, 0, 0)),
            pl.BlockSpec((DEPTH, 1, CHUNK, W_B), lambda b, t: (0, b, 0, 0)),
        ],
        out_shape=[
            jax.ShapeDtypeStruct((batch, seq, d), F32),
            jax.ShapeDtypeStruct((DEPTH, batch, HIST, W_A), F32),
            jax.ShapeDtypeStruct((DEPTH, batch, CHUNK, W_B), F32),
        ],
        scratch_shapes=[
            pltpu.VMEM((DEPTH, HIST + tm, W_A), F32),
            pltpu.VMEM((DEPTH, HIST + tm, W_A - POOL_GROUP_DIM), F32),
            pltpu.VMEM((DEPTH, HIST + tm, W_A - 2 * POOL_GROUP_DIM), F32),
            pltpu.VMEM((DEPTH, HIST + tm, W_A - 3 * POOL_GROUP_DIM), F32),
        ],
        compiler_params=pltpu.CompilerParams(
            dimension_semantics=("arbitrary", "arbitrary"),
            vmem_limit_bytes=VMEM_LIMIT_BYTES,
        ),
        name="prompt_trunk",
    )(x, *consts)


def _sample_call(x, sp_t, norm_g, w_in, w_pool, pool_scale, v_norm_g, ws00, b0, w_out, fin_g, past_len):
    n, d = x.shape
    return pl.pallas_call(
        functools.partial(_sample_kernel, past_len=past_len),
        out_shape=[
            jax.ShapeDtypeStruct((n, d), F32),
            jax.ShapeDtypeStruct((DEPTH, n, W_A), F32),
            jax.ShapeDtypeStruct((DEPTH, n, W_B), F32),
        ],
        compiler_params=pltpu.CompilerParams(vmem_limit_bytes=VMEM_LIMIT_BYTES),
        name="sample_trunk",
    )(x, sp_t, norm_g, w_in, w_pool, pool_scale, v_norm_g, ws00, b0, w_out, fin_g)


def kernel(x_prompt, x_sample, state_pool, norm_g, w_in, w_pool, pool_scale, v_norm_g, w_s, b_s, w_out, final_norm_g):
    dec_batch, dec_seq, _ = x_sample.shape
    assert dec_seq == 1 and state_pool.shape[2] == POOL_BUF
    past_len = SAMPLE_PAST_LEN

    norm_g3 = norm_g.reshape(DEPTH, 1, D_MODEL)
    pool_scale3 = pool_scale.reshape(DEPTH, 1, W_A)
    v_norm_g3 = v_norm_g.reshape(DEPTH, 1, W_B)
    fin_g2 = final_norm_g.reshape(1, D_MODEL)
    bias_full = jnp.repeat(jnp.transpose(b_s, (0, 2, 1)), HEAD_DIM_B, axis=2)

    y_prompt, pool_p16, chunk_v_prompt = _prompt_call(
        x_prompt, norm_g3, w_in, w_pool, pool_scale3, v_norm_g3, w_s, bias_full, w_out, fin_g2)
    pool_prompt = pool_p16[:, :, HIST - POOL_BUF:, :]

    sp_t = jnp.transpose(state_pool, (0, 2, 1, 3))
    ws00 = jnp.repeat(w_s[:, :, 0, 0], HEAD_DIM_B, axis=1).reshape(DEPTH, 1, W_B)
    b0 = jnp.repeat(b_s[:, :, 0], HEAD_DIM_B, axis=1).reshape(DEPTH, 1, W_B)
    y_s, a_in_s, vn_s = _sample_call(
        x_sample.reshape(dec_batch, D_MODEL), sp_t, norm_g3, w_in, w_pool, pool_scale3, v_norm_g3,
        ws00, b0, w_out, fin_g2, past_len)
    y_sample = y_s.reshape(dec_batch, 1, D_MODEL)
    pool_sample = jnp.concatenate([state_pool[:, :, 1:, :], a_in_s[:, :, None, :]], axis=2)
    chunk_v_sample = vn_s[:, :, None, :]
    return (y_prompt, y_sample, pool_prompt, pool_sample, chunk_v_prompt, chunk_v_sample)
```

```python
import functools

import jax
import jax.numpy as jnp
from jax import lax
from jax.experimental import pallas as pl
from jax.experimental.pallas import tpu as pltpu

D_MODEL = 1024
DEPTH = 2
W_A = 512
W_B = 512
POOL_WINDOWS = (2, 4, 8, 16)
N_POOL_GROUPS = len(POOL_WINDOWS)
POOL_GROUP_DIM = W_A // N_POOL_GROUPS
POOL_BUF = max(POOL_WINDOWS) - 1
CHUNK = 128
N_HEADS_B = 4
HEAD_DIM_B = W_B // N_HEADS_B
EPS = 1e-6
SAMPLE_PAST_LEN = 16384

COL_A_IN = 0
COL_A_GATE = W_A
COL_U = 2 * W_A
COL_V = 2 * W_A + W_B
COL_B_GATE = 2 * W_A + 2 * W_B

HIST = 16
ROW_TILE = 512
VMEM_LIMIT_BYTES = 52 * 1024 * 1024

BF16 = jnp.bfloat16
F32 = jnp.float32


def _dot(a, b):
    return jnp.dot(a.astype(BF16), b.astype(BF16), preferred_element_type=F32)


def _rmsnorm(x, g):
    ms = jnp.mean(x * x, axis=-1, keepdims=True)
    return (x * lax.rsqrt(ms + EPS)) * g


def _silu(x):
    return x * (1.0 / (1.0 + jnp.exp(-x)))


def _head_rmsnorm(v, g):
    parts = []
    for h in range(N_HEADS_B):
        sl = slice(h * HEAD_DIM_B, (h + 1) * HEAD_DIM_B)
        parts.append(_rmsnorm(v[:, sl], g[:, sl]))
    return jnp.concatenate(parts, axis=-1)


def _masked_ws(ws_ref, l):
    row = lax.broadcasted_iota(jnp.int32, (CHUNK, CHUNK), 0)
    col = lax.broadcasted_iota(jnp.int32, (CHUNK, CHUNK), 1)
    keep = col <= row
    return [jnp.where(keep, ws_ref[l, h], 0.0).astype(BF16) for h in range(N_HEADS_B)]


def _prompt_kernel(x_ref, norm_g_ref, w_in_ref, w_pool_ref, pool_scale_ref, v_norm_g_ref, ws_ref,
                   bias_ref, w_out_ref, fin_g_ref,
                   y_ref, pool_ref, cv_ref,
                   a_hist, s2_hist, s4_hist, s8_hist, x_buf):
    t = pl.program_id(1)
    tm = x_ref.shape[1]
    n_chunks = tm // CHUNK

    @pl.when(t == 0)
    def _():
        a_hist[:, 0:HIST, :] = jnp.zeros((DEPTH, HIST, a_hist.shape[2]), F32)
        s2_hist[:, 0:HIST, :] = jnp.zeros((DEPTH, HIST, s2_hist.shape[2]), F32)
        s4_hist[:, 0:HIST, :] = jnp.zeros((DEPTH, HIST, s4_hist.shape[2]), F32)
        s8_hist[:, 0:HIST, :] = jnp.zeros((DEPTH, HIST, s8_hist.shape[2]), F32)

    pos1 = lax.broadcasted_iota(jnp.int32, (HIST, POOL_GROUP_DIM), 0) + t * tm + 1

    x_buf[...] = x_ref[0]

    def layer_body(l, carry):
        h = _rmsnorm(x_buf[...], norm_g_ref[l]).astype(BF16)

        v = _dot(h, w_in_ref[l, :, COL_V:COL_V + W_B])
        a_in = _dot(h, w_in_ref[l, :, COL_A_IN:COL_A_IN + W_A])
        a_gate = _dot(h, w_in_ref[l, :, COL_A_GATE:COL_A_GATE + W_A])
        u = _dot(h, w_in_ref[l, :, COL_U:COL_U + W_B])

        vn = _head_rmsnorm(v, v_norm_g_ref[l])

        cv_ref[l, 0] = vn[tm - CHUNK:, :]

        vn_b = vn.astype(BF16)

        a_hist[l, HIST:HIST + tm, :] = a_in
        s2 = a_in + a_hist[l, HIST - 1:HIST - 1 + tm, :]
        s2_hist[l, HIST:HIST + tm, :] = s2[:, POOL_GROUP_DIM:]
        s4 = s2[:, POOL_GROUP_DIM:] + s2_hist[l, HIST - 2:HIST - 2 + tm, :]
        s4_hist[l, HIST:HIST + tm, :] = s4[:, POOL_GROUP_DIM:]
        s8 = s4[:, POOL_GROUP_DIM:] + s4_hist[l, HIST - 4:HIST - 4 + tm, :]
        s8_hist[l, HIST:HIST + tm, :] = s8[:, POOL_GROUP_DIM:]
        s16 = s8[:, POOL_GROUP_DIM:] + s8_hist[l, HIST - 8:HIST - 8 + tm, :]
        win_sums = (s2[:, :POOL_GROUP_DIM], s4[:, :POOL_GROUP_DIM], s8[:, :POOL_GROUP_DIM], s16)

        a_hist[l, 0:HIST, :] = a_hist[l, tm:tm + HIST, :]
        s2_hist[l, 0:HIST, :] = s2_hist[l, tm:tm + HIST, :]
        s4_hist[l, 0:HIST, :] = s4_hist[l, tm:tm + HIST, :]
        s8_hist[l, 0:HIST, :] = s8_hist[l, tm:tm + HIST, :]

        pool_ref[l, 0] = a_in[tm - HIST:, :]

        d_parts = []
        for g, w in enumerate(POOL_WINDOWS):
            sl = slice(g * POOL_GROUP_DIM, (g + 1) * POOL_GROUP_DIM)
            ssum = win_sums[g]
            inv_head = 1.0 / jnp.minimum(pos1, w).astype(F32)
            mean = jnp.concatenate([ssum[:HIST] * inv_head, ssum[HIST:] * (1.0 / w)], axis=0)
            d_parts.append((mean - a_in[:, sl]).astype(BF16))

        ws_b = _masked_ws(ws_ref, l)
        s_heads = []
        for hh in range(N_HEADS_B):
            sl = slice(hh * HEAD_DIM_B, (hh + 1) * HEAD_DIM_B)
            rhs = jnp.concatenate([vn_b[c * CHUNK:(c + 1) * CHUNK, sl] for c in range(n_chunks)], axis=1)
            res = jnp.dot(ws_b[hh], rhs, preferred_element_type=F32)
            s_heads.append(jnp.concatenate(
                [res[:, c * HEAD_DIM_B:(c + 1) * HEAD_DIM_B] for c in range(n_chunks)], axis=0))

        y_parts = [_dot(d_parts[g], w_pool_ref[l, g]) for g in range(N_POOL_GROUPS)]

        b_gate = _dot(h, w_in_ref[l, :, COL_B_GATE:COL_B_GATE + W_B])

        a_out = jnp.concatenate(
            [(y_parts[g] * pool_scale_ref[l][:, g * POOL_GROUP_DIM:(g + 1) * POOL_GROUP_DIM])
             * _silu(a_gate[:, g * POOL_GROUP_DIM:(g + 1) * POOL_GROUP_DIM]) for g in range(N_POOL_GROUPS)],
            axis=-1).astype(BF16)
        out_a = _dot(a_out, w_out_ref[l, 0:W_A, :])

        bias = bias_ref[l]
        s = jnp.concatenate(s_heads, axis=-1) + jnp.concatenate([bias] * n_chunks, axis=0)
        b_out = ((u * s) * _silu(b_gate)).astype(BF16)
        x_buf[...] = x_buf[...] + (out_a + _dot(b_out, w_out_ref[l, W_A:, :]))
        return carry

    lax.fori_loop(0, DEPTH, layer_body, 0)
    y_ref[0] = _rmsnorm(x_buf[...], fin_g_ref[...])


def _sample_kernel(x_ref, sp_ref, norm_g_ref, w_in_ref, w_pool_ref, pool_scale_ref, v_norm_g_ref,
                   ws00_ref, b0_ref, w_out_ref, fin_g_ref,
                   y_ref, a_in_ref, vn_ref, *, past_len):
    x = x_ref[...]
    for l in range(DEPTH):
        h = _rmsnorm(x, norm_g_ref[l]).astype(BF16)
        a_in = _dot(h, w_in_ref[l, :, COL_A_IN:COL_A_IN + W_A])
        a_in_ref[l] = a_in
        a_gate = _dot(h, w_in_ref[l, :, COL_A_GATE:COL_A_GATE + W_A])
        a_parts = []
        for g, w in enumerate(POOL_WINDOWS):
            sl = slice(g * POOL_GROUP_DIM, (g + 1) * POOL_GROUP_DIM)
            ssum = a_in[:, sl]
            for k in range(1, w):
                ssum = ssum + sp_ref[l, POOL_BUF - k, :, sl]
            cnt = float(min(past_len + 1, w))
            d = ssum / cnt - a_in[:, sl]
            yg = _dot(d, w_pool_ref[l, g]) * pool_scale_ref[l][:, sl]
            a_parts.append(yg * _silu(a_gate[:, sl]))
        a_out = jnp.concatenate(a_parts, axis=-1).astype(BF16)

        v = _dot(h, w_in_ref[l, :, COL_V:COL_V + W_B])
        vn = _head_rmsnorm(v, v_norm_g_ref[l])
        vn_ref[l] = vn
        s = ws00_ref[l] * vn + b0_ref[l]
        u = _dot(h, w_in_ref[l, :, COL_U:COL_U + W_B])
        b_gate = _dot(h, w_in_ref[l, :, COL_B_GATE:COL_B_GATE + W_B])
        b_out = ((u * s) * _silu(b_gate)).astype(BF16)

        x = x + (_dot(a_out, w_out_ref[l, 0:W_A, :]) + _dot(b_out, w_out_ref[l, W_A:, :]))
    y_ref[...] = _rmsnorm(x, fin_g_ref[...])


def _const_spec(shape):
    zeros = (0,) * len(shape)
    return pl.BlockSpec(shape, lambda b, t: zeros, pipeline_mode=pl.Buffered(1))


def _prompt_call(x, norm_g, w_in_b, w_pool_b, pool_scale, v_norm_g, w_s, bias_full, w_out_b, fin_g):
    batch, seq, d = x.shape
    tm = ROW_TILE
    assert seq % tm == 0 and tm % CHUNK == 0 and tm >= HIST
    consts = (norm_g, w_in_b, w_pool_b, pool_scale, v_norm_g, w_s, bias_full, w_out_b, fin_g)
    return pl.pallas_call(
        _prompt_kernel,
        grid=(batch, seq // tm),
        in_specs=[pl.BlockSpec((1, tm, d), lambda b, t: (b, t, 0))] + [_const_spec(c.shape) for c in consts],
        out_specs=[
            pl.BlockSpec((1, tm, d), lambda b, t: (b, t, 0)),
            pl.BlockSpec((DEPTH, 1, HIST, W_A), lambda b, t: (0, b, 0, 0)),
            pl.BlockSpec((DEPTH, 1, CHUNK, W_B), lambda b, t: (0, b, 0, 0)),
        ],
        out_shape=[
            jax.ShapeDtypeStruct((batch, seq, d), F32),
            jax.ShapeDtypeStruct((DEPTH, batch, HIST, W_A), F32),
            jax.ShapeDtypeStruct((DEPTH, batch, CHUNK, W_B), F32),
        ],
        scratch_shapes=[
            pltpu.VMEM((DEPTH, HIST + tm, W_A), F32),
            pltpu.VMEM((DEPTH, HIST + tm, W_A - POOL_GROUP_DIM), F32),
            pltpu.VMEM((DEPTH, HIST + tm, W_A - 2 * POOL_GROUP_DIM), F32),
            pltpu.VMEM((DEPTH, HIST + tm, W_A - 3 * POOL_GROUP_DIM), F32),
            pltpu.VMEM((tm, d), F32),
        ],
        compiler_params=pltpu.CompilerParams(
            dimension_semantics=("arbitrary", "arbitrary"),
            vmem_limit_bytes=VMEM_LIMIT_BYTES,
        ),
        name="prompt_trunk",
    )(x, *consts)


def _sample_call(x, sp_t, norm_g, w_in_b, w_pool_b, pool_scale, v_norm_g, ws00, b0, w_out_b, fin_g, past_len):
    n, d = x.shape
    return pl.pallas_call(
        functools.partial(_sample_kernel, past_len=past_len),
        out_shape=[
            jax.ShapeDtypeStruct((n, d), F32),
            jax.ShapeDtypeStruct((DEPTH, n, W_A), F32),
            jax.ShapeDtypeStruct((DEPTH, n, W_B), F32),
        ],
        compiler_params=pltpu.CompilerParams(vmem_limit_bytes=VMEM_LIMIT_BYTES),
        name="sample_trunk",
    )(x, sp_t, norm_g, w_in_b, w_pool_b, pool_scale, v_norm_g, ws00, b0, w_out_b, fin_g)


def kernel(x_prompt, x_sample, state_pool, norm_g, w_in, w_pool, pool_scale, v_norm_g, w_s, b_s, w_out, final_norm_g):
    dec_batch, dec_seq, _ = x_sample.shape
    assert dec_seq == 1 and state_pool.shape[2] == POOL_BUF
    past_len = SAMPLE_PAST_LEN

    w_in_b = w_in.astype(BF16)
    w_out_b = w_out.astype(BF16)
    w_pool_b = w_pool.astype(BF16)
    norm_g3 = norm_g.reshape(DEPTH, 1, D_MODEL)
    pool_scale3 = pool_scale.reshape(DEPTH, 1, W_A)
    v_norm_g3 = v_norm_g.reshape(DEPTH, 1, W_B)
    fin_g2 = final_norm_g.reshape(1, D_MODEL)
    bias_full = jnp.repeat(jnp.transpose(b_s, (0, 2, 1)), HEAD_DIM_B, axis=2)

    y_prompt, pool_p16, chunk_v_prompt = _prompt_call(
        x_prompt, norm_g3, w_in_b, w_pool_b, pool_scale3, v_norm_g3, w_s, bias_full, w_out_b, fin_g2)
    pool_prompt = pool_p16[:, :, HIST - POOL_BUF:, :]

    sp_t = jnp.transpose(state_pool, (0, 2, 1, 3))
    ws00 = jnp.repeat(w_s[:, :, 0, 0], HEAD_DIM_B, axis=1).reshape(DEPTH, 1, W_B)
    b0 = jnp.repeat(b_s[:, :, 0], HEAD_DIM_B, axis=1).reshape(DEPTH, 1, W_B)
    y_s, a_in_s, vn_s = _sample_call(
        x_sample.reshape(dec_batch, D_MODEL), sp_t, norm_g3, w_in_b, w_pool_b, pool_scale3, v_norm_g3,
        ws00, b0, w_out_b, fin_g2, past_len)
    y_sample = y_s.reshape(dec_batch, 1, D_MODEL)
    pool_sample = jnp.concatenate([state_pool[:, :, 1:, :], a_in_s[:, :, None, :]], axis=2)
    chunk_v_sample = vn_s[:, :, None, :]
    return (y_prompt, y_sample, pool_prompt, pool_sample, chunk_v_prompt, chunk_v_sample)
```

```python
import functools

import jax
import jax.numpy as jnp
from jax import lax
from jax.experimental import pallas as pl
from jax.experimental.pallas import tpu as pltpu

D_MODEL = 1024
DEPTH = 2
W_A = 512
W_B = 512
POOL_WINDOWS = (2, 4, 8, 16)
N_POOL_GROUPS = len(POOL_WINDOWS)
POOL_GROUP_DIM = W_A // N_POOL_GROUPS
POOL_BUF = max(POOL_WINDOWS) - 1
CHUNK = 128
N_HEADS_B = 4
HEAD_DIM_B = W_B // N_HEADS_B
EPS = 1e-6
SAMPLE_PAST_LEN = 16384

COL_A_IN = 0
COL_A_GATE = W_A
COL_U = 2 * W_A
COL_V = 2 * W_A + W_B
COL_B_GATE = 2 * W_A + 2 * W_B

HIST = 16
ROW_TILE = 512
ROW_BLOCKS = 2
VMEM_LIMIT_BYTES = 52 * 1024 * 1024

BF16 = jnp.bfloat16
F32 = jnp.float32


def _dot(a, b):
    return jnp.dot(a.astype(BF16), b.astype(BF16), preferred_element_type=F32)


def _rmsnorm(x, g):
    ms = jnp.mean(x * x, axis=-1, keepdims=True)
    return (x * lax.rsqrt(ms + EPS)) * g


def _silu(x):
    hx = 0.5 * x
    return hx * jnp.tanh(hx) + hx


def _head_rmsnorm(v, g):
    parts = []
    for h in range(N_HEADS_B):
        sl = slice(h * HEAD_DIM_B, (h + 1) * HEAD_DIM_B)
        parts.append(_rmsnorm(v[:, sl], g[:, sl]))
    return jnp.concatenate(parts, axis=-1)


def _masked_ws(ws_ref, l):
    row = lax.broadcasted_iota(jnp.int32, (CHUNK, CHUNK), 0)
    col = lax.broadcasted_iota(jnp.int32, (CHUNK, CHUNK), 1)
    keep = col <= row
    return [jnp.where(keep, ws_ref[l, h], 0.0).astype(BF16) for h in range(N_HEADS_B)]


class _Refs:
    def __init__(self, **kw):
        self.__dict__.update(kw)


class _Chain:
    def __init__(self, l, r0, rows, t, tm, R):
        self.l, self.r0, self.rows, self.t, self.tm, self.R = l, r0, rows, t, tm, R

    def _in(self, col, width):
        R = self.R
        return jnp.dot(R.h_buf[self.r0:self.r0 + self.rows, :], R.w_in[self.l, :, col:col + width],
                       preferred_element_type=F32)

    def p0(self):
        self.v = self._in(COL_V, W_B)
        self.a_in = self._in(COL_A_IN, W_A)

    def p1(self):
        self.a_gate = self._in(COL_A_GATE, W_A)
        self.u = self._in(COL_U, W_B)

    def p2(self):
        l, r0, rows, R = self.l, self.r0, self.rows, self.R
        n_chunks = rows // CHUNK
        a_in = self.a_in
        base = HIST + r0

        vn = _head_rmsnorm(self.v, R.v_norm_g[l])
        self.cv_rows = vn[rows - CHUNK:, :]
        vn_b = vn.astype(BF16)

        R.a_hist[l, base:base + rows, :] = a_in
        s2 = a_in + R.a_hist[l, base - 1:base - 1 + rows, :]
        R.s2_hist[l, base:base + rows, :] = s2[:, POOL_GROUP_DIM:]
        s4 = s2[:, POOL_GROUP_DIM:] + R.s2_hist[l, base - 2:base - 2 + rows, :]
        R.s4_hist[l, base:base + rows, :] = s4[:, POOL_GROUP_DIM:]
        s8 = s4[:, POOL_GROUP_DIM:] + R.s4_hist[l, base - 4:base - 4 + rows, :]
        R.s8_hist[l, base:base + rows, :] = s8[:, POOL_GROUP_DIM:]
        s16 = s8[:, POOL_GROUP_DIM:] + R.s8_hist[l, base - 8:base - 8 + rows, :]
        win_sums = (s2[:, :POOL_GROUP_DIM], s4[:, :POOL_GROUP_DIM], s8[:, :POOL_GROUP_DIM], s16)
        self.pool_rows = a_in[rows - HIST:, :]

        d_parts = []
        for g, w in enumerate(POOL_WINDOWS):
            sl = slice(g * POOL_GROUP_DIM, (g + 1) * POOL_GROUP_DIM)
            ssum = win_sums[g]
            if r0 == 0:
                pos1 = lax.broadcasted_iota(jnp.int32, (HIST, POOL_GROUP_DIM), 0) + self.t * self.tm + 1
                inv_head = 1.0 / jnp.minimum(pos1, w).astype(F32)
                mean = jnp.concatenate([ssum[:HIST] * inv_head, ssum[HIST:] * (1.0 / w)], axis=0)
            else:
                mean = ssum * (1.0 / w)
            d_parts.append((mean - a_in[:, sl]).astype(BF16))

        ws_b = _masked_ws(R.ws, l)
        s_heads = []
        for hh in range(N_HEADS_B):
            sl = slice(hh * HEAD_DIM_B, (hh + 1) * HEAD_DIM_B)
            rhs = jnp.concatenate([vn_b[c * CHUNK:(c + 1) * CHUNK, sl] for c in range(n_chunks)], axis=1)
            res = jnp.dot(ws_b[hh], rhs, preferred_element_type=F32)
            s_heads.append(jnp.concatenate(
                [res[:, c * HEAD_DIM_B:(c + 1) * HEAD_DIM_B] for c in range(n_chunks)], axis=0))
        self.s_heads = s_heads

        self.y_parts = [_dot(d_parts[g], R.w_pool[l, g]) for g in range(N_POOL_GROUPS)]
        self.b_gate = self._in(COL_B_GATE, W_B)

    def p3(self):
        l, rows, R = self.l, self.rows, self.R
        n_chunks = rows // CHUNK
        a_out = ((jnp.concatenate(self.y_parts, axis=-1) * R.pool_scale[l]) * _silu(self.a_gate)).astype(BF16)
        out_a = _dot(a_out, R.w_out[l, 0:W_A, :])
        s = jnp.concatenate(self.s_heads, axis=-1) + jnp.concatenate([R.bias[l]] * n_chunks, axis=0)
        b_out = ((self.u * s) * _silu(self.b_gate)).astype(BF16)
        self.out = out_a + _dot(b_out, R.w_out[l, W_A:, :])


def _prompt_kernel(x_ref, norm_g_ref, w_in_ref, w_pool_ref, pool_scale_ref, v_norm_g_ref, ws_ref,
                   bias_ref, w_out_ref, fin_g_ref,
                   y_ref, pool_ref, cv_ref,
                   a_hist, s2_hist, s4_hist, s8_hist, h_buf, x_buf):
    t = pl.program_id(1)
    tm = x_ref.shape[1]
    rows = tm // ROW_BLOCKS
    R = _Refs(w_in=w_in_ref, w_pool=w_pool_ref, pool_scale=pool_scale_ref, v_norm_g=v_norm_g_ref, ws=ws_ref,
              bias=bias_ref, w_out=w_out_ref, a_hist=a_hist, s2_hist=s2_hist, s4_hist=s4_hist, s8_hist=s8_hist,
              h_buf=h_buf)
    hists = (a_hist, s2_hist, s4_hist, s8_hist)

    @pl.when(t == 0)
    def _():
        for buf in hists:
            buf[:, 0:HIST, :] = jnp.zeros((DEPTH, HIST, buf.shape[2]), F32)

    for l in range(DEPTH):
        def x_rows(rs, l=l):
            return x_ref[0, rs, :] if l == 0 else x_buf[rs, :]

        blocks = [slice(k * rows, (k + 1) * rows) for k in range(ROW_BLOCKS)]
        for rs in blocks:
            h_buf[rs, :] = _rmsnorm(x_rows(rs), norm_g_ref[l]).astype(BF16)
        chains = [_Chain(l, k * rows, rows, t, tm, R) for k in range(ROW_BLOCKS)]
        for c in chains:
            c.p0()
        for c in chains:
            c.p1()
        for c in chains:
            c.p2()
        for buf in hists:
            buf[l, 0:HIST, :] = buf[l, tm:tm + HIST, :]
        pool_ref[l, 0] = chains[-1].pool_rows
        cv_ref[l, 0] = chains[-1].cv_rows
        for c, rs in zip(chains, blocks):
            c.p3()
            x_new = x_rows(rs) + c.out
            if l + 1 < DEPTH:
                x_buf[rs, :] = x_new
            else:
                y_ref[0, rs, :] = _rmsnorm(x_new, fin_g_ref[...])


def _sample_kernel(x_ref, sp_ref, norm_g_ref, w_in_ref, w_pool_ref, pool_scale_ref, v_norm_g_ref,
                   ws00_ref, b0_ref, w_out_ref, fin_g_ref,
                   y_ref, a_in_ref, vn_ref, *, past_len):
    x = x_ref[...]
    for l in range(DEPTH):
        h = _rmsnorm(x, norm_g_ref[l]).astype(BF16)
        a_in = _dot(h, w_in_ref[l, :, COL_A_IN:COL_A_IN + W_A])
        a_in_ref[l] = a_in
        a_gate = _dot(h, w_in_ref[l, :, COL_A_GATE:COL_A_GATE + W_A])
        a_parts = []
        for g, w in enumerate(POOL_WINDOWS):
            sl = slice(g * POOL_GROUP_DIM, (g + 1) * POOL_GROUP_DIM)
            ssum = a_in[:, sl]
            for k in range(1, w):
                ssum = ssum + sp_ref[l, POOL_BUF - k, :, sl]
            cnt = float(min(past_len + 1, w))
            d = ssum / cnt - a_in[:, sl]
            yg = _dot(d, w_pool_ref[l, g]) * pool_scale_ref[l][:, sl]
            a_parts.append(yg * _silu(a_gate[:, sl]))
        a_out = jnp.concatenate(a_parts, axis=-1).astype(BF16)

        v = _dot(h, w_in_ref[l, :, COL_V:COL_V + W_B])
        vn = _head_rmsnorm(v, v_norm_g_ref[l])
        vn_ref[l] = vn
        s = ws00_ref[l] * vn + b0_ref[l]
        u = _dot(h, w_in_ref[l, :, COL_U:COL_U + W_B])
        b_gate = _dot(h, w_in_ref[l, :, COL_B_GATE:COL_B_GATE + W_B])
        b_out = ((u * s) * _silu(b_gate)).astype(BF16)

        x = x + (_dot(a_out, w_out_ref[l, 0:W_A, :]) + _dot(b_out, w_out_ref[l, W_A:, :]))
    y_ref[...] = _rmsnorm(x, fin_g_ref[...])


def _const_spec(shape):
    zeros = (0,) * len(shape)
    return pl.BlockSpec(shape, lambda b, t: zeros, pipeline_mode=pl.Buffered(1))


def _prompt_call(x, norm_g, w_in_b, w_pool_b, pool_scale, v_norm_g, w_s, bias_full, w_out_b, fin_g):
    batch, seq, d = x.shape
    tm = ROW_TILE
    assert seq % tm == 0 and tm % (ROW_BLOCKS * CHUNK) == 0
    consts = (norm_g, w_in_b, w_pool_b, pool_scale, v_norm_g, w_s, bias_full, w_out_b, fin_g)
    return pl.pallas_call(
        _prompt_kernel,
        grid=(batch, seq // tm),
        in_specs=[pl.BlockSpec((1, tm, d), lambda b, t: (b, t, 0))] + [_const_spec(c.shape) for c in consts],
        out_specs=[
            pl.BlockSpec((1, tm, d), lambda b, t: (b, t, 0)),
            pl.BlockSpec((DEPTH, 1, HIST, W_A), lambda b, t: (0, b, 0, 0)),
            pl.BlockSpec((DEPTH, 1, CHUNK, W_B), lambda b, t: (0, b, 0, 0)),
        ],
        out_shape=[
            jax.ShapeDtypeStruct((batch, seq, d), F32),
            jax.ShapeDtypeStruct((DEPTH, batch, HIST, W_A), F32),
            jax.ShapeDtypeStruct((DEPTH, batch, CHUNK, W_B), F32),
        ],
        scratch_shapes=[
            pltpu.VMEM((DEPTH, HIST + tm, W_A), F32),
            pltpu.VMEM((DEPTH, HIST + tm, W_A - POOL_GROUP_DIM), F32),
            pltpu.VMEM((DEPTH, HIST + tm, W_A - 2 * POOL_GROUP_DIM), F32),
            pltpu.VMEM((DEPTH, HIST + tm, W_A - 3 * POOL_GROUP_DIM), F32),
            pltpu.VMEM((tm, d), BF16),
            pltpu.VMEM((tm, d), F32),
        ],
        compiler_params=pltpu.CompilerParams(
            dimension_semantics=("arbitrary", "arbitrary"),
            vmem_limit_bytes=VMEM_LIMIT_BYTES,
        ),
        name="prompt_trunk",
    )(x, *consts)


def _sample_call(x, sp_t, norm_g, w_in_b, w_pool_b, pool_scale, v_norm_g, ws00, b0, w_out_b, fin_g, past_len):
    n, d = x.shape
    return pl.pallas_call(
        functools.partial(_sample_kernel, past_len=past_len),
        out_shape=[
            jax.ShapeDtypeStruct((n, d), F32),
            jax.ShapeDtypeStruct((DEPTH, n, W_A), F32),
            jax.ShapeDtypeStruct((DEPTH, n, W_B), F32),
        ],
        compiler_params=pltpu.CompilerParams(vmem_limit_bytes=VMEM_LIMIT_BYTES),
        name="sample_trunk",
    )(x, sp_t, norm_g, w_in_b, w_pool_b, pool_scale, v_norm_g, ws00, b0, w_out_b, fin_g)


def kernel(x_prompt, x_sample, state_pool, norm_g, w_in, w_pool, pool_scale, v_norm_g, w_s, b_s, w_out, final_norm_g):
    dec_batch, dec_seq, _ = x_sample.shape
    assert dec_seq == 1 and state_pool.shape[2] == POOL_BUF
    past_len = SAMPLE_PAST_LEN

    w_in_b = w_in.astype(BF16)
    w_out_b = w_out.astype(BF16)
    w_pool_b = w_pool.astype(BF16)
    norm_g3 = norm_g.reshape(DEPTH, 1, D_MODEL)
    pool_scale3 = pool_scale.reshape(DEPTH, 1, W_A)
    v_norm_g3 = v_norm_g.reshape(DEPTH, 1, W_B)
    fin_g2 = final_norm_g.reshape(1, D_MODEL)
    bias_full = jnp.repeat(jnp.transpose(b_s, (0, 2, 1)), HEAD_DIM_B, axis=2)

    y_prompt, pool_p16, chunk_v_prompt = _prompt_call(
        x_prompt, norm_g3, w_in_b, w_pool_b, pool_scale3, v_norm_g3, w_s, bias_full, w_out_b, fin_g2)
    pool_prompt = pool_p16[:, :, HIST - POOL_BUF:, :]

    sp_t = jnp.transpose(state_pool, (0, 2, 1, 3))
    ws00 = jnp.repeat(w_s[:, :, 0, 0], HEAD_DIM_B, axis=1).reshape(DEPTH, 1, W_B)
    b0 = jnp.repeat(b_s[:, :, 0], HEAD_DIM_B, axis=1).reshape(DEPTH, 1, W_B)
    y_s, a_in_s, vn_s = _sample_call(
        x_sample.reshape(dec_batch, D_MODEL), sp_t, norm_g3, w_in_b, w_pool_b, pool_scale3, v_norm_g3,
        ws00, b0, w_out_b, fin_g2, past_len)
    y_sample = y_s.reshape(dec_batch, 1, D_MODEL)
    pool_sample = jnp.concatenate([state_pool[:, :, 1:, :], a_in_s[:, :, None, :]], axis=2)
    chunk_v_sample = vn_s[:, :, None, :]
    return (y_prompt, y_sample, pool_prompt, pool_sample, chunk_v_prompt, chunk_v_sample)
```

```python
import functools

import jax
import jax.numpy as jnp
from jax import lax
from jax.experimental import pallas as pl
from jax.experimental.pallas import tpu as pltpu

D_MODEL = 1024
DEPTH = 2
W_A = 512
W_B = 512
D_IN = 2 * W_A + 3 * W_B
POOL_WINDOWS = (2, 4, 8, 16)
N_POOL_GROUPS = len(POOL_WINDOWS)
POOL_GROUP_DIM = W_A // N_POOL_GROUPS
POOL_BUF = max(POOL_WINDOWS) - 1
CHUNK = 128
N_HEADS_B = 4
HEAD_DIM_B = W_B // N_HEADS_B
EPS = 1e-6
SAMPLE_PAST_LEN = 16384

COL_A_IN = 0
COL_A_GATE = W_A
COL_U = 2 * W_A
COL_V = 2 * W_A + W_B
COL_B_GATE = 2 * W_A + 2 * W_B

HIST = 16
ROW_TILE = 512
ROW_BLOCKS = 2
WEIGHT_CHUNK_ROWS = 256
VMEM_LIMIT_BYTES = 56 * 1024 * 1024

BF16 = jnp.bfloat16
F32 = jnp.float32


def _dot(a, b):
    return jnp.dot(a.astype(BF16), b.astype(BF16), preferred_element_type=F32)


def _rmsnorm(x, g):
    ms = jnp.mean(x * x, axis=-1, keepdims=True)
    return (x * lax.rsqrt(ms + EPS)) * g


def _silu(x):
    hx = 0.5 * x
    return hx * jnp.tanh(hx) + hx


def _head_rmsnorm(v, g):
    parts = []
    for h in range(N_HEADS_B):
        sl = slice(h * HEAD_DIM_B, (h + 1) * HEAD_DIM_B)
        parts.append(_rmsnorm(v[:, sl], g[:, sl]))
    return jnp.concatenate(parts, axis=-1)


def _masked_ws(ws_ref, l):
    row = lax.broadcasted_iota(jnp.int32, (CHUNK, CHUNK), 0)
    col = lax.broadcasted_iota(jnp.int32, (CHUNK, CHUNK), 1)
    keep = col <= row
    return [jnp.where(keep, ws_ref[l, h], 0.0).astype(BF16) for h in range(N_HEADS_B)]


class _Refs:
    def __init__(self, **kw):
        self.__dict__.update(kw)


def _load_weights_as_bf16(w_hbm, w_bf, stage, sems):
    _, k, n = w_hbm.shape
    chunks = [(l, r) for l in range(DEPTH) for r in range(0, k, WEIGHT_CHUNK_ROWS)]

    def copy(i):
        l, r = chunks[i]
        slot = i % 2
        return pltpu.make_async_copy(
            w_hbm.at[l, pl.ds(r, WEIGHT_CHUNK_ROWS), :], stage.at[slot, :, pl.ds(0, n)], sems.at[slot])

    copy(0).start()
    for i, (l, r) in enumerate(chunks):
        if i + 1 < len(chunks):
            copy(i + 1).start()
        copy(i).wait()
        w_bf[l, r:r + WEIGHT_CHUNK_ROWS, :] = stage[i % 2, :, 0:n].astype(BF16)


class _Chain:
    def __init__(self, l, r0, rows, t, tm, R):
        self.l, self.r0, self.rows, self.t, self.tm, self.R = l, r0, rows, t, tm, R

    def _in(self, col, width):
        R = self.R
        return jnp.dot(R.h_buf[self.r0:self.r0 + self.rows, :], R.w_in[self.l, :, col:col + width],
                       preferred_element_type=F32)

    def p0(self):
        self.v = self._in(COL_V, W_B)
        self.a_in = self._in(COL_A_IN, W_A)

    def p1(self):
        self.a_gate = self._in(COL_A_GATE, W_A)
        self.u = self._in(COL_U, W_B)

    def p2(self):
        l, r0, rows, R = self.l, self.r0, self.rows, self.R
        n_chunks = rows // CHUNK
        a_in = self.a_in
        base = HIST + r0

        vn = _head_rmsnorm(self.v, R.v_norm_g[l])
        self.cv_rows = vn[rows - CHUNK:, :]
        vn_b = vn.astype(BF16)

        R.a_hist[l, base:base + rows, :] = a_in
        s2 = a_in + R.a_hist[l, base - 1:base - 1 + rows, :]
        R.s2_hist[l, base:base + rows, :] = s2[:, POOL_GROUP_DIM:]
        s4 = s2[:, POOL_GROUP_DIM:] + R.s2_hist[l, base - 2:base - 2 + rows, :]
        R.s4_hist[l, base:base + rows, :] = s4[:, POOL_GROUP_DIM:]
        s8 = s4[:, POOL_GROUP_DIM:] + R.s4_hist[l, base - 4:base - 4 + rows, :]
        R.s8_hist[l, base:base + rows, :] = s8[:, POOL_GROUP_DIM:]
        s16 = s8[:, POOL_GROUP_DIM:] + R.s8_hist[l, base - 8:base - 8 + rows, :]
        win_sums = (s2[:, :POOL_GROUP_DIM], s4[:, :POOL_GROUP_DIM], s8[:, :POOL_GROUP_DIM], s16)
        self.pool_rows = a_in[rows - HIST:, :]

        d_parts = []
        for g, w in enumerate(POOL_WINDOWS):
            sl = slice(g * POOL_GROUP_DIM, (g + 1) * POOL_GROUP_DIM)
            ssum = win_sums[g]
            if r0 == 0:
                pos1 = lax.broadcasted_iota(jnp.int32, (HIST, POOL_GROUP_DIM), 0) + self.t * self.tm + 1
                inv_head = 1.0 / jnp.minimum(pos1, w).astype(F32)
                mean = jnp.concatenate([ssum[:HIST] * inv_head, ssum[HIST:] * (1.0 / w)], axis=0)
            else:
                mean = ssum * (1.0 / w)
            d_parts.append((mean - a_in[:, sl]).astype(BF16))

        ws_b = _masked_ws(R.ws, l)
        s_heads = []
        for hh in range(N_HEADS_B):
            sl = slice(hh * HEAD_DIM_B, (hh + 1) * HEAD_DIM_B)
            rhs = jnp.concatenate([vn_b[c * CHUNK:(c + 1) * CHUNK, sl] for c in range(n_chunks)], axis=1)
            res = jnp.dot(ws_b[hh], rhs, preferred_element_type=F32)
            s_heads.append(jnp.concatenate(
                [res[:, c * HEAD_DIM_B:(c + 1) * HEAD_DIM_B] for c in range(n_chunks)], axis=0))
        self.s_heads = s_heads

        self.y_parts = [_dot(d_parts[g], R.w_pool[l, g]) for g in range(N_POOL_GROUPS)]
        self.b_gate = self._in(COL_B_GATE, W_B)

    def p3(self):
        l, rows, R = self.l, self.rows, self.R
        n_chunks = rows // CHUNK
        a_out = ((jnp.concatenate(self.y_parts, axis=-1) * R.pool_scale[l]) * _silu(self.a_gate)).astype(BF16)
        out_a = _dot(a_out, R.w_out[l, 0:W_A, :])
        s = jnp.concatenate(self.s_heads, axis=-1) + jnp.concatenate([R.bias[l]] * n_chunks, axis=0)
        b_out = ((self.u * s) * _silu(self.b_gate)).astype(BF16)
        self.out = out_a + _dot(b_out, R.w_out[l, W_A:, :])


def _sample_group(xs_ref, sp_ref, norm_g_ref, ws00_ref, b0_ref, fin_g_ref, R, ys_ref, a_in_s_ref, vn_s_ref):
    x = xs_ref[...]
    for l in range(DEPTH):
        h = _rmsnorm(x, norm_g_ref[l]).astype(BF16)
        a_in = _dot(h, R.w_in[l, :, COL_A_IN:COL_A_IN + W_A])
        a_in_s_ref[l] = a_in
        a_gate = _dot(h, R.w_in[l, :, COL_A_GATE:COL_A_GATE + W_A])
        a_parts = []
        for g, w in enumerate(POOL_WINDOWS):
            sl = slice(g * POOL_GROUP_DIM, (g + 1) * POOL_GROUP_DIM)
            ssum = a_in[:, sl]
            for k in range(1, w):
                ssum = ssum + sp_ref[l, POOL_BUF - k, :, sl]
            cnt = float(min(SAMPLE_PAST_LEN + 1, w))
            d = ssum / cnt - a_in[:, sl]
            yg = _dot(d, R.w_pool[l, g]) * R.pool_scale[l][:, sl]
            a_parts.append(yg * _silu(a_gate[:, sl]))
        a_out = jnp.concatenate(a_parts, axis=-1)

        v = _dot(h, R.w_in[l, :, COL_V:COL_V + W_B])
        vn = _head_rmsnorm(v, R.v_norm_g[l])
        vn_s_ref[l] = vn
        s = ws00_ref[l] * vn + b0_ref[l]
        u = _dot(h, R.w_in[l, :, COL_U:COL_U + W_B])
        b_gate = _dot(h, R.w_in[l, :, COL_B_GATE:COL_B_GATE + W_B])
        b_out = (u * s) * _silu(b_gate)

        x = x + (_dot(a_out, R.w_out[l, 0:W_A, :]) + _dot(b_out, R.w_out[l, W_A:, :]))
    ys_ref[...] = _rmsnorm(x, fin_g_ref[...])


def _trunk_kernel(x_ref, xs_ref, sp_ref, norm_g_ref, w_in_hbm, w_pool_ref, pool_scale_ref, v_norm_g_ref, ws_ref,
                  bias_ref, ws00_ref, b0_ref, w_out_hbm, fin_g_ref,
                  y_ref, pool_ref, cv_ref, ys_ref, a_in_s_ref, vn_s_ref,
                  a_hist, s2_hist, s4_hist, s8_hist, h_buf, x_buf, w_in_bf, w_out_bf, stage, sems):
    b = pl.program_id(0)
    t = pl.program_id(1)
    tm = x_ref.shape[1]
    rows = tm // ROW_BLOCKS
    R = _Refs(w_in=w_in_bf, w_pool=w_pool_ref, pool_scale=pool_scale_ref, v_norm_g=v_norm_g_ref, ws=ws_ref,
              bias=bias_ref, w_out=w_out_bf, a_hist=a_hist, s2_hist=s2_hist, s4_hist=s4_hist, s8_hist=s8_hist,
              h_buf=h_buf)
    hists = (a_hist, s2_hist, s4_hist, s8_hist)

    @pl.when(jnp.logical_and(b == 0, t == 0))
    def _():
        _load_weights_as_bf16(w_in_hbm, w_in_bf, stage, sems)
        _load_weights_as_bf16(w_out_hbm, w_out_bf, stage, sems)
        _sample_group(xs_ref, sp_ref, norm_g_ref, ws00_ref, b0_ref, fin_g_ref, R, ys_ref, a_in_s_ref, vn_s_ref)

    @pl.when(t == 0)
    def _():
        for buf in hists:
            buf[:, 0:HIST, :] = jnp.zeros((DEPTH, HIST, buf.shape[2]), F32)

    for l in range(DEPTH):
        def x_rows(rs, l=l):
            return x_ref[0, rs, :] if l == 0 else x_buf[rs, :]

        blocks = [slice(k * rows, (k + 1) * rows) for k in range(ROW_BLOCKS)]
        for rs in blocks:
            h_buf[rs, :] = _rmsnorm(x_rows(rs), norm_g_ref[l]).astype(BF16)
        chains = [_Chain(l, k * rows, rows, t, tm, R) for k in range(ROW_BLOCKS)]
        for c in chains:
            c.p0()
        for c in chains:
            c.p1()
        for c in chains:
            c.p2()
        for buf in hists:
            buf[l, 0:HIST, :] = buf[l, tm:tm + HIST, :]
        pool_ref[l, 0] = chains[-1].pool_rows
        cv_ref[l, 0] = chains[-1].cv_rows
        for c, rs in zip(chains, blocks):
            c.p3()
            x_new = x_rows(rs) + c.out
            if l + 1 < DEPTH:
                x_buf[rs, :] = x_new
            else:
                y_ref[0, rs, :] = _rmsnorm(x_new, fin_g_ref[...])


def _const_spec(shape):
    zeros = (0,) * len(shape)
    return pl.BlockSpec(shape, lambda b, t: zeros, pipeline_mode=pl.Buffered(1))


def _trunk_call(x, xs, sp_t, norm_g, w_in, w_pool, pool_scale, v_norm_g, w_s, bias_full, ws00, b0, w_out, fin_g):
    batch, seq, d = x.shape
    n_s = xs.shape[0]
    tm = ROW_TILE
    assert seq % tm == 0 and tm % (ROW_BLOCKS * CHUNK) == 0
    assert d % WEIGHT_CHUNK_ROWS == 0 and (W_A + W_B) % WEIGHT_CHUNK_ROWS == 0
    hbm = pl.BlockSpec(memory_space=pl.ANY)
    vmem_inputs = {1: xs, 2: sp_t, 3: norm_g, 5: w_pool, 6: pool_scale, 7: v_norm_g, 8: w_s, 9: bias_full,
                   10: ws00, 11: b0, 13: fin_g}
    args = (x, xs, sp_t, norm_g, w_in, w_pool, pool_scale, v_norm_g, w_s, bias_full, ws00, b0, w_out, fin_g)
    in_specs = []
    for i, a in enumerate(args):
        if i == 0:
            in_specs.append(pl.BlockSpec((1, tm, d), lambda b, t: (b, t, 0)))
        elif i in vmem_inputs:
            in_specs.append(_const_spec(a.shape))
        else:
            in_specs.append(hbm)
    return pl.pallas_call(
        _trunk_kernel,
        grid=(batch, seq // tm),
        in_specs=in_specs,
        out_specs=[
            pl.BlockSpec((1, tm, d), lambda b, t: (b, t, 0)),
            pl.BlockSpec((DEPTH, 1, HIST, W_A), lambda b, t: (0, b, 0, 0)),
            pl.BlockSpec((DEPTH, 1, CHUNK, W_B), lambda b, t: (0, b, 0, 0)),
            pl.BlockSpec((n_s, d), lambda b, t: (0, 0)),
            pl.BlockSpec((DEPTH, n_s, W_A), lambda b, t: (0, 0, 0)),
            pl.BlockSpec((DEPTH, n_s, W_B), lambda b, t: (0, 0, 0)),
        ],
        out_shape=[
            jax.ShapeDtypeStruct((batch, seq, d), F32),
            jax.ShapeDtypeStruct((DEPTH, batch, HIST, W_A), F32),
            jax.ShapeDtypeStruct((DEPTH, batch, CHUNK, W_B), F32),
            jax.ShapeDtypeStruct((n_s, d), F32),
            jax.ShapeDtypeStruct((DEPTH, n_s, W_A), F32),
            jax.ShapeDtypeStruct((DEPTH, n_s, W_B), F32),
        ],
        scratch_shapes=[
            pltpu.VMEM((DEPTH, HIST + tm, W_A), F32),
            pltpu.VMEM((DEPTH, HIST + tm, W_A - POOL_GROUP_DIM), F32),
            pltpu.VMEM((DEPTH, HIST + tm, W_A - 2 * POOL_GROUP_DIM), F32),
            pltpu.VMEM((DEPTH, HIST + tm, W_A - 3 * POOL_GROUP_DIM), F32),
            pltpu.VMEM((tm, d), BF16),
            pltpu.VMEM((tm, d), F32),
            pltpu.VMEM((DEPTH, d, D_IN), BF16),
            pltpu.VMEM((DEPTH, W_A + W_B, d), BF16),
            pltpu.VMEM((2, WEIGHT_CHUNK_ROWS, D_IN), F32),
            pltpu.SemaphoreType.DMA((2,)),
        ],
        compiler_params=pltpu.CompilerParams(
            dimension_semantics=("arbitrary", "arbitrary"),
            vmem_limit_bytes=VMEM_LIMIT_BYTES,
        ),
        name="trunk",
    )(*args)


def kernel(x_prompt, x_sample, state_pool, norm_g, w_in, w_pool, pool_scale, v_norm_g, w_s, b_s, w_out, final_norm_g):
    dec_batch, dec_seq, _ = x_sample.shape
    assert dec_seq == 1 and state_pool.shape[2] == POOL_BUF

    norm_g3 = norm_g.reshape(DEPTH, 1, D_MODEL)
    pool_scale3 = pool_scale.reshape(DEPTH, 1, W_A)
    v_norm_g3 = v_norm_g.reshape(DEPTH, 1, W_B)
    fin_g2 = final_norm_g.reshape(1, D_MODEL)
    bias_full = jnp.repeat(jnp.transpose(b_s, (0, 2, 1)), HEAD_DIM_B, axis=2)
    sp_t = jnp.transpose(state_pool, (0, 2, 1, 3))
    ws00 = jnp.repeat(w_s[:, :, 0, 0], HEAD_DIM_B, axis=1).reshape(DEPTH, 1, W_B)
    b0 = jnp.repeat(b_s[:, :, 0], HEAD_DIM_B, axis=1).reshape(DEPTH, 1, W_B)

    y_prompt, pool_p16, chunk_v_prompt, y_s, a_in_s, vn_s = _trunk_call(
        x_prompt, x_sample.reshape(dec_batch, D_MODEL), sp_t, norm_g3, w_in, w_pool, pool_scale3, v_norm_g3,
        w_s, bias_full, ws00, b0, w_out, fin_g2)

    pool_prompt = pool_p16[:, :, HIST - POOL_BUF:, :]
    y_sample = y_s.reshape(dec_batch, 1, D_MODEL)
    pool_sample = jnp.concatenate([state_pool[:, :, 1:, :], a_in_s[:, :, None, :]], axis=2)
    chunk_v_sample = vn_s[:, :, None, :]
    return (y_prompt, y_sample, pool_prompt, pool_sample, chunk_v_prompt, chunk_v_sample)
```

```python
import jax
import jax.numpy as jnp
from jax import lax
from jax.experimental import pallas as pl
from jax.experimental.pallas import tpu as pltpu

D_MODEL = 1024
DEPTH = 2
W_A = 512
W_B = 512
D_IN = 2 * W_A + 3 * W_B
POOL_WINDOWS = (2, 4, 8, 16)
N_POOL_GROUPS = len(POOL_WINDOWS)
POOL_GROUP_DIM = W_A // N_POOL_GROUPS
POOL_BUF = max(POOL_WINDOWS) - 1
CHUNK = 128
N_HEADS_B = 4
HEAD_DIM_B = W_B // N_HEADS_B
EPS = 1e-6
SAMPLE_PAST_LEN = 16384

COL_A_IN = 0
COL_A_GATE = W_A
COL_U = 2 * W_A
COL_V = 2 * W_A + W_B
COL_B_GATE = 2 * W_A + 2 * W_B

HIST = 16
ROW_TILE = 512
ROW_BLOCKS = 2
WEIGHT_CHUNK_ROWS = 128
VMEM_LIMIT_BYTES = 56 * 1024 * 1024

BF16 = jnp.bfloat16
F32 = jnp.float32


def _dot(a, b):
    return jnp.dot(a.astype(BF16), b.astype(BF16), preferred_element_type=F32)


def _rmsnorm(x, g):
    ms = jnp.mean(x * x, axis=-1, keepdims=True)
    return (x * lax.rsqrt(ms + EPS)) * g


def _silu(x):
    hx = 0.5 * x
    return hx * jnp.tanh(hx) + hx


def _head_rmsnorm(v, g):
    parts = []
    for h in range(N_HEADS_B):
        sl = slice(h * HEAD_DIM_B, (h + 1) * HEAD_DIM_B)
        parts.append(_rmsnorm(v[:, sl], g[:, sl]))
    return jnp.concatenate(parts, axis=-1)


def _masked_ws(ws_ref, l):
    row = lax.broadcasted_iota(jnp.int32, (CHUNK, CHUNK), 0)
    col = lax.broadcasted_iota(jnp.int32, (CHUNK, CHUNK), 1)
    keep = col <= row
    return [jnp.where(keep, ws_ref[l, h], 0.0).astype(BF16) for h in range(N_HEADS_B)]


class _Refs:
    def __init__(self, **kw):
        self.__dict__.update(kw)


def _row(ref, l):
    return ref[l:l + 1, :]


def _load_weights_as_bf16(w_hbm, w_bf, stage, sems):
    _, k, n = w_hbm.shape
    chunks = [(l, r) for l in range(DEPTH) for r in range(0, k, WEIGHT_CHUNK_ROWS)]

    def copy(i):
        l, r = chunks[i]
        slot = i % 2
        return pltpu.make_async_copy(
            w_hbm.at[l, pl.ds(r, WEIGHT_CHUNK_ROWS), :], stage.at[slot, :, pl.ds(0, n)], sems.at[slot])

    copy(0).start()
    for i, (l, r) in enumerate(chunks):
        if i + 1 < len(chunks):
            copy(i + 1).start()
        copy(i).wait()
        w_bf[l, r:r + WEIGHT_CHUNK_ROWS, :] = stage[i % 2, :, 0:n].astype(BF16)


class _Chain:
    def __init__(self, l, r0, rows, t, tm, R):
        self.l, self.r0, self.rows, self.t, self.tm, self.R = l, r0, rows, t, tm, R

    def _in(self, col, width):
        R = self.R
        return jnp.dot(R.h_buf[self.r0:self.r0 + self.rows, :], R.w_in[self.l, :, col:col + width],
                       preferred_element_type=F32)

    def p0(self):
        self.v = self._in(COL_V, W_B)
        self.a_in = self._in(COL_A_IN, W_A)

    def p1(self):
        self.a_gate = self._in(COL_A_GATE, W_A)
        self.u = self._in(COL_U, W_B)

    def p2(self):
        l, r0, rows, R = self.l, self.r0, self.rows, self.R
        n_chunks = rows // CHUNK
        a_in = self.a_in
        base = HIST + r0

        vn = _head_rmsnorm(self.v, _row(R.v_norm_g, l))
        self.cv_rows = vn[rows - CHUNK:, :]
        vn_b = vn.astype(BF16)

        R.a_hist[l, base:base + rows, :] = a_in
        s2 = a_in + R.a_hist[l, base - 1:base - 1 + rows, :]
        R.s2_hist[l, base:base + rows, :] = s2[:, POOL_GROUP_DIM:]
        s4 = s2[:, POOL_GROUP_DIM:] + R.s2_hist[l, base - 2:base - 2 + rows, :]
        R.s4_hist[l, base:base + rows, :] = s4[:, POOL_GROUP_DIM:]
        s8 = s4[:, POOL_GROUP_DIM:] + R.s4_hist[l, base - 4:base - 4 + rows, :]
        R.s8_hist[l, base:base + rows, :] = s8[:, POOL_GROUP_DIM:]
        s16 = s8[:, POOL_GROUP_DIM:] + R.s8_hist[l, base - 8:base - 8 + rows, :]
        win_sums = (s2[:, :POOL_GROUP_DIM], s4[:, :POOL_GROUP_DIM], s8[:, :POOL_GROUP_DIM], s16)
        self.pool_rows = a_in[rows - POOL_BUF:, :]

        d_parts = []
        for g, w in enumerate(POOL_WINDOWS):
            sl = slice(g * POOL_GROUP_DIM, (g + 1) * POOL_GROUP_DIM)
            ssum = win_sums[g]
            if r0 == 0:
                pos1 = lax.broadcasted_iota(jnp.int32, (HIST, POOL_GROUP_DIM), 0) + self.t * self.tm + 1
                inv_head = 1.0 / jnp.minimum(pos1, w).astype(F32)
                mean = jnp.concatenate([ssum[:HIST] * inv_head, ssum[HIST:] * (1.0 / w)], axis=0)
            else:
                mean = ssum * (1.0 / w)
            d_parts.append((mean - a_in[:, sl]).astype(BF16))

        ws_b = _masked_ws(R.ws, l)
        s_heads = []
        for hh in range(N_HEADS_B):
            sl = slice(hh * HEAD_DIM_B, (hh + 1) * HEAD_DIM_B)
            rhs = jnp.concatenate([vn_b[c * CHUNK:(c + 1) * CHUNK, sl] for c in range(n_chunks)], axis=1)
            res = jnp.dot(ws_b[hh], rhs, preferred_element_type=F32)
            s_heads.append(jnp.concatenate(
                [res[:, c * HEAD_DIM_B:(c + 1) * HEAD_DIM_B] for c in range(n_chunks)], axis=0))
        self.s_heads = s_heads

        self.y_parts = [_dot(d_parts[g], R.w_pool[l, g]) for g in range(N_POOL_GROUPS)]
        self.b_gate = self._in(COL_B_GATE, W_B)

    def p3(self):
        l, rows, R = self.l, self.rows, self.R
        n_chunks = rows // CHUNK
        a_out = ((jnp.concatenate(self.y_parts, axis=-1) * _row(R.pool_scale, l)) * _silu(self.a_gate)).astype(BF16)
        out_a = _dot(a_out, R.w_out[l, 0:W_A, :])
        s = jnp.concatenate(self.s_heads, axis=-1) + jnp.concatenate([R.bias[l]] * n_chunks, axis=0)
        b_out = ((self.u * s) * _silu(self.b_gate)).astype(BF16)
        self.out = out_a + _dot(b_out, R.w_out[l, W_A:, :])


def _per_head_lanes(vals):
    return jnp.concatenate([jnp.broadcast_to(v, (1, HEAD_DIM_B)) for v in vals], axis=-1)


def _sample_group(xs_ref, sp_ref, norm_g_ref, b_s_ref, fin_g, R, ys_ref, a_in_s_ref, vn_s_ref):
    x = xs_ref[...]
    for l in range(DEPTH):
        h = _rmsnorm(x, _row(norm_g_ref, l)).astype(BF16)
        a_in = _dot(h, R.w_in[l, :, COL_A_IN:COL_A_IN + W_A])
        a_in_s_ref[l] = a_in
        a_gate = _dot(h, R.w_in[l, :, COL_A_GATE:COL_A_GATE + W_A])
        a_parts = []
        for g, w in enumerate(POOL_WINDOWS):
            sl = slice(g * POOL_GROUP_DIM, (g + 1) * POOL_GROUP_DIM)
            ssum = a_in[:, sl]
            for k in range(1, w):
                ssum = ssum + sp_ref[l, POOL_BUF - k, :, sl]
            cnt = float(min(SAMPLE_PAST_LEN + 1, w))
            d = ssum / cnt - a_in[:, sl]
            yg = _dot(d, R.w_pool[l, g]) * _row(R.pool_scale, l)[:, sl]
            a_parts.append(yg * _silu(a_gate[:, sl]))
        a_out = jnp.concatenate(a_parts, axis=-1)

        v = _dot(h, R.w_in[l, :, COL_V:COL_V + W_B])
        vn = _head_rmsnorm(v, _row(R.v_norm_g, l))
        vn_s_ref[l] = vn
        ws00 = _per_head_lanes([R.ws[l, hh, 0:1, 0:1] for hh in range(N_HEADS_B)])
        b0 = _per_head_lanes([b_s_ref[l, hh:hh + 1, 0:1] for hh in range(N_HEADS_B)])
        s = ws00 * vn + b0
        u = _dot(h, R.w_in[l, :, COL_U:COL_U + W_B])
        b_gate = _dot(h, R.w_in[l, :, COL_B_GATE:COL_B_GATE + W_B])
        b_out = (u * s) * _silu(b_gate)

        x = x + (_dot(a_out, R.w_out[l, 0:W_A, :]) + _dot(b_out, R.w_out[l, W_A:, :]))
    ys_ref[...] = _rmsnorm(x, fin_g)


def _trunk_kernel(x_ref, xs_ref, sp_ref, norm_g_ref, w_in_hbm, w_pool_ref, pool_scale_ref, v_norm_g_ref, ws_ref,
                  b_s_ref, w_out_hbm, fin_g_ref,
                  y_ref, pool_ref, cv_ref, ys_ref, a_in_s_ref, vn_s_ref,
                  a_hist, s2_hist, s4_hist, s8_hist, h_buf, x_buf, w_in_bf, w_out_bf, bias_buf, stage, sems):
    b = pl.program_id(0)
    t = pl.program_id(1)
    tm = x_ref.shape[1]
    rows = tm // ROW_BLOCKS
    R = _Refs(w_in=w_in_bf, w_pool=w_pool_ref, pool_scale=pool_scale_ref, v_norm_g=v_norm_g_ref, ws=ws_ref,
              bias=bias_buf, w_out=w_out_bf, a_hist=a_hist, s2_hist=s2_hist, s4_hist=s4_hist, s8_hist=s8_hist,
              h_buf=h_buf)
    hists = (a_hist, s2_hist, s4_hist, s8_hist)
    fin_g = fin_g_ref[...].reshape(1, D_MODEL)

    @pl.when(jnp.logical_and(b == 0, t == 0))
    def _():
        _load_weights_as_bf16(w_in_hbm, w_in_bf, stage, sems)
        _load_weights_as_bf16(w_out_hbm, w_out_bf, stage, sems)
        for l in range(DEPTH):
            for hh in range(N_HEADS_B):
                bias_buf[l, :, hh * HEAD_DIM_B:(hh + 1) * HEAD_DIM_B] = jnp.broadcast_to(
                    b_s_ref[l, hh:hh + 1, :], (CHUNK, CHUNK)).T
        _sample_group(xs_ref, sp_ref, norm_g_ref, b_s_ref, fin_g, R, ys_ref, a_in_s_ref, vn_s_ref)

    @pl.when(t == 0)
    def _():
        for buf in hists:
            buf[:, 0:HIST, :] = jnp.zeros((DEPTH, HIST, buf.shape[2]), F32)

    for l in range(DEPTH):
        def x_rows(rs, l=l):
            return x_ref[0, rs, :] if l == 0 else x_buf[rs, :]

        blocks = [slice(k * rows, (k + 1) * rows) for k in range(ROW_BLOCKS)]
        for rs in blocks:
            h_buf[rs, :] = _rmsnorm(x_rows(rs), _row(norm_g_ref, l)).astype(BF16)
        chains = [_Chain(l, k * rows, rows, t, tm, R) for k in range(ROW_BLOCKS)]
        for c in chains:
            c.p0()
        for c in chains:
            c.p1()
        for c in chains:
            c.p2()
        for buf in hists:
            buf[l, 0:HIST, :] = buf[l, tm:tm + HIST, :]
        pool_ref[l, 0] = chains[-1].pool_rows
        cv_ref[l, 0] = chains[-1].cv_rows
        for c, rs in zip(chains, blocks):
            c.p3()
            x_new = x_rows(rs) + c.out
            if l + 1 < DEPTH:
                x_buf[rs, :] = x_new
            else:
                y_ref[0, rs, :] = _rmsnorm(x_new, fin_g)


def _const_spec(shape):
    zeros = (0,) * len(shape)
    return pl.BlockSpec(shape, lambda b, t: zeros, pipeline_mode=pl.Buffered(1))


def _trunk_call(x, xs, sp_t, norm_g, w_in, w_pool, pool_scale, v_norm_g, w_s, b_s, w_out, fin_g):
    batch, seq, d = x.shape
    n_s = xs.shape[0]
    tm = ROW_TILE
    assert seq % tm == 0 and tm % (ROW_BLOCKS * CHUNK) == 0
    assert d % WEIGHT_CHUNK_ROWS == 0 and (W_A + W_B) % WEIGHT_CHUNK_ROWS == 0
    hbm = pl.BlockSpec(memory_space=pl.ANY)
    args = (x, xs, sp_t, norm_g, w_in, w_pool, pool_scale, v_norm_g, w_s, b_s, w_out, fin_g)
    in_specs = []
    for a in args:
        if a is x:
            in_specs.append(pl.BlockSpec((1, tm, d), lambda b, t: (b, t, 0)))
        elif a is w_in or a is w_out:
            in_specs.append(hbm)
        else:
            in_specs.append(_const_spec(a.shape))
    return pl.pallas_call(
        _trunk_kernel,
        grid=(batch, seq // tm),
        in_specs=in_specs,
        out_specs=[
            pl.BlockSpec((1, tm, d), lambda b, t: (b, t, 0)),
            pl.BlockSpec((DEPTH, 1, POOL_BUF, W_A), lambda b, t: (0, b, 0, 0)),
            pl.BlockSpec((DEPTH, 1, CHUNK, W_B), lambda b, t: (0, b, 0, 0)),
            pl.BlockSpec((n_s, d), lambda b, t: (0, 0)),
            pl.BlockSpec((DEPTH, n_s, W_A), lambda b, t: (0, 0, 0)),
            pl.BlockSpec((DEPTH, n_s, W_B), lambda b, t: (0, 0, 0)),
        ],
        out_shape=[
            jax.ShapeDtypeStruct((batch, seq, d), F32),
            jax.ShapeDtypeStruct((DEPTH, batch, POOL_BUF, W_A), F32),
            jax.ShapeDtypeStruct((DEPTH, batch, CHUNK, W_B), F32),
            jax.ShapeDtypeStruct((n_s, d), F32),
            jax.ShapeDtypeStruct((DEPTH, n_s, W_A), F32),
            jax.ShapeDtypeStruct((DEPTH, n_s, W_B), F32),
        ],
        scratch_shapes=[
            pltpu.VMEM((DEPTH, HIST + tm, W_A), F32),
            pltpu.VMEM((DEPTH, HIST + tm, W_A - POOL_GROUP_DIM), F32),
            pltpu.VMEM((DEPTH, HIST + tm, W_A - 2 * POOL_GROUP_DIM), F32),
            pltpu.VMEM((DEPTH, HIST + tm, W_A - 3 * POOL_GROUP_DIM), F32),
            pltpu.VMEM((tm, d), BF16),
            pltpu.VMEM((tm, d), F32),
            pltpu.VMEM((DEPTH, d, D_IN), BF16),
            pltpu.VMEM((DEPTH, W_A + W_B, d), BF16),
            pltpu.VMEM((DEPTH, CHUNK, W_B), F32),
            pltpu.VMEM((2, WEIGHT_CHUNK_ROWS, D_IN), F32),
            pltpu.SemaphoreType.DMA((2,)),
        ],
        compiler_params=pltpu.CompilerParams(
            dimension_semantics=("arbitrary", "arbitrary"),
            vmem_limit_bytes=VMEM_LIMIT_BYTES,
        ),
        name="trunk",
    )(*args)


def kernel(x_prompt, x_sample, state_pool, norm_g, w_in, w_pool, pool_scale, v_norm_g, w_s, b_s, w_out, final_norm_g):
    dec_batch, dec_seq, _ = x_sample.shape
    assert dec_seq == 1 and state_pool.shape[2] == POOL_BUF

    sp_t = jnp.transpose(state_pool, (0, 2, 1, 3))
    y_prompt, pool_prompt, chunk_v_prompt, y_s, a_in_s, vn_s = _trunk_call(
        x_prompt, x_sample.reshape(dec_batch, D_MODEL), sp_t, norm_g, w_in, w_pool, pool_scale, v_norm_g,
        w_s, b_s, w_out, final_norm_g)

    y_sample = y_s.reshape(dec_batch, 1, D_MODEL)
    pool_sample = jnp.concatenate([state_pool[:, :, 1:, :], a_in_s[:, :, None, :]], axis=2)
    chunk_v_sample = vn_s[:, :, None, :]
    return (y_prompt, y_sample, pool_prompt, pool_sample, chunk_v_prompt, chunk_v_sample)
```

```python
import jax
import jax.numpy as jnp
from jax import lax
from jax.experimental import pallas as pl
from jax.experimental.pallas import tpu as pltpu

D_MODEL = 1024
DEPTH = 2
W_A = 512
W_B = 512
D_IN = 2 * W_A + 3 * W_B
POOL_WINDOWS = (2, 4, 8, 16)
N_POOL_GROUPS = len(POOL_WINDOWS)
POOL_GROUP_DIM = W_A // N_POOL_GROUPS
POOL_BUF = max(POOL_WINDOWS) - 1
CHUNK = 128
N_HEADS_B = 4
HEAD_DIM_B = W_B // N_HEADS_B
EPS = 1e-6
SAMPLE_PAST_LEN = 16384

COL_A_IN = 0
COL_A_GATE = W_A
COL_U = 2 * W_A
COL_V = 2 * W_A + W_B
COL_B_GATE = 2 * W_A + 2 * W_B

HIST = 16
ROW_TILE = 512
ROW_BLOCKS = 2
WEIGHT_CHUNK_ROWS = 256
VMEM_LIMIT_BYTES = 56 * 1024 * 1024

BF16 = jnp.bfloat16
F32 = jnp.float32


def _dot(a, b):
    return jnp.dot(a.astype(BF16), b.astype(BF16), preferred_element_type=F32)


def _rmsnorm(x, g):
    ms = jnp.mean(x * x, axis=-1, keepdims=True)
    return (x * lax.rsqrt(ms + EPS)) * g


def _silu(x):
    hx = 0.5 * x
    return hx * jnp.tanh(hx) + hx


def _head_rmsnorm(v, g):
    parts = []
    for h in range(N_HEADS_B):
        sl = slice(h * HEAD_DIM_B, (h + 1) * HEAD_DIM_B)
        parts.append(_rmsnorm(v[:, sl], g[:, sl]))
    return jnp.concatenate(parts, axis=-1)


def _masked_ws(ws_ref, l):
    row = lax.broadcasted_iota(jnp.int32, (CHUNK, CHUNK), 0)
    col = lax.broadcasted_iota(jnp.int32, (CHUNK, CHUNK), 1)
    keep = col <= row
    return [jnp.where(keep, ws_ref[l, h], 0.0).astype(BF16) for h in range(N_HEADS_B)]


class _Refs:
    def __init__(self, **kw):
        self.__dict__.update(kw)


def _row(ref, l):
    return ref[l:l + 1, :]


def _load_weights_as_bf16(w_hbm, w_bf, stage, sems):
    _, k, n = w_hbm.shape
    chunks = [(l, r) for l in range(DEPTH) for r in range(0, k, WEIGHT_CHUNK_ROWS)]

    def copy(i):
        l, r = chunks[i]
        slot = i % 2
        return pltpu.make_async_copy(
            w_hbm.at[l, pl.ds(r, WEIGHT_CHUNK_ROWS), :], stage.at[slot, :, pl.ds(0, n)], sems.at[slot])

    copy(0).start()
    for i, (l, r) in enumerate(chunks):
        if i + 1 < len(chunks):
            copy(i + 1).start()
        copy(i).wait()
        w_bf[l, r:r + WEIGHT_CHUNK_ROWS, :] = stage[i % 2, :, 0:n].astype(BF16)


class _Chain:
    def __init__(self, l, r0, rows, t, tm, R):
        self.l, self.r0, self.rows, self.t, self.tm, self.R = l, r0, rows, t, tm, R

    def _in(self, col, width):
        R = self.R
        return jnp.dot(R.h_buf[self.r0:self.r0 + self.rows, :], R.w_in[self.l, :, col:col + width],
                       preferred_element_type=F32)

    def p0(self):
        self.v = self._in(COL_V, W_B)
        self.a_in = self._in(COL_A_IN, W_A)

    def p1(self):
        self.a_gate = self._in(COL_A_GATE, W_A)
        self.u = self._in(COL_U, W_B)

    def p2(self):
        l, r0, rows, R = self.l, self.r0, self.rows, self.R
        n_chunks = rows // CHUNK
        a_in = self.a_in
        base = HIST + r0

        vn = _head_rmsnorm(self.v, _row(R.v_norm_g, l))
        self.cv_rows = vn[rows - CHUNK:, :]
        vn_b = vn.astype(BF16)

        R.a_hist[l, base:base + rows, :] = a_in
        s2 = a_in + R.a_hist[l, base - 1:base - 1 + rows, :]
        R.s2_hist[l, base:base + rows, :] = s2[:, POOL_GROUP_DIM:]
        s4 = s2[:, POOL_GROUP_DIM:] + R.s2_hist[l, base - 2:base - 2 + rows, :]
        R.s4_hist[l, base:base + rows, :] = s4[:, POOL_GROUP_DIM:]
        s8 = s4[:, POOL_GROUP_DIM:] + R.s4_hist[l, base - 4:base - 4 + rows, :]
        R.s8_hist[l, base:base + rows, :] = s8[:, POOL_GROUP_DIM:]
        s16 = s8[:, POOL_GROUP_DIM:] + R.s8_hist[l, base - 8:base - 8 + rows, :]
        win_sums = (s2[:, :POOL_GROUP_DIM], s4[:, :POOL_GROUP_DIM], s8[:, :POOL_GROUP_DIM], s16)
        self.pool_rows = a_in[rows - POOL_BUF:, :]

        d_parts = []
        for g, w in enumerate(POOL_WINDOWS):
            sl = slice(g * POOL_GROUP_DIM, (g + 1) * POOL_GROUP_DIM)
            ssum = win_sums[g]
            if r0 == 0:
                pos1 = lax.broadcasted_iota(jnp.int32, (HIST, POOL_GROUP_DIM), 0) + self.t * self.tm + 1
                inv_head = 1.0 / jnp.minimum(pos1, w).astype(F32)
                mean = jnp.concatenate([ssum[:HIST] * inv_head, ssum[HIST:] * (1.0 / w)], axis=0)
            else:
                mean = ssum * (1.0 / w)
            d_parts.append((mean - a_in[:, sl]).astype(BF16))

        ws_b = _masked_ws(R.ws, l)
        s_heads = []
        for hh in range(N_HEADS_B):
            sl = slice(hh * HEAD_DIM_B, (hh + 1) * HEAD_DIM_B)
            rhs = jnp.concatenate([vn_b[c * CHUNK:(c + 1) * CHUNK, sl] for c in range(n_chunks)], axis=1)
            res = jnp.dot(ws_b[hh], rhs, preferred_element_type=F32)
            s_heads.append(jnp.concatenate(
                [res[:, c * HEAD_DIM_B:(c + 1) * HEAD_DIM_B] for c in range(n_chunks)], axis=0))
        self.s_heads = s_heads

        self.y_parts = [_dot(d_parts[g], R.w_pool[l, g]) for g in range(N_POOL_GROUPS)]
        self.b_gate = self._in(COL_B_GATE, W_B)

    def p3(self):
        l, rows, R = self.l, self.rows, self.R
        n_chunks = rows // CHUNK
        a_out = ((jnp.concatenate(self.y_parts, axis=-1) * _row(R.pool_scale, l)) * _silu(self.a_gate)).astype(BF16)
        out_a = _dot(a_out, R.w_out[l, 0:W_A, :])
        s = jnp.concatenate(self.s_heads, axis=-1) + jnp.concatenate([R.bias[l]] * n_chunks, axis=0)
        b_out = ((self.u * s) * _silu(self.b_gate)).astype(BF16)
        self.out = out_a + _dot(b_out, R.w_out[l, W_A:, :])


def _per_head_lanes(vals):
    return jnp.concatenate([jnp.broadcast_to(v, (1, HEAD_DIM_B)) for v in vals], axis=-1)


def _sample_group(xs_ref, sp_ref, norm_g_ref, b_s_ref, fin_g, R, ys_ref, a_in_s_ref, vn_s_ref):
    x = xs_ref[...]
    for l in range(DEPTH):
        h = _rmsnorm(x, _row(norm_g_ref, l)).astype(BF16)
        a_in = _dot(h, R.w_in[l, :, COL_A_IN:COL_A_IN + W_A])
        a_in_s_ref[l] = a_in
        a_gate = _dot(h, R.w_in[l, :, COL_A_GATE:COL_A_GATE + W_A])
        a_parts = []
        for g, w in enumerate(POOL_WINDOWS):
            sl = slice(g * POOL_GROUP_DIM, (g + 1) * POOL_GROUP_DIM)
            ssum = a_in[:, sl]
            for k in range(1, w):
                ssum = ssum + sp_ref[l, POOL_BUF - k, :, sl]
            cnt = float(min(SAMPLE_PAST_LEN + 1, w))
            d = ssum / cnt - a_in[:, sl]
            yg = _dot(d, R.w_pool[l, g]) * _row(R.pool_scale, l)[:, sl]
            a_parts.append(yg * _silu(a_gate[:, sl]))
        a_out = jnp.concatenate(a_parts, axis=-1)

        v = _dot(h, R.w_in[l, :, COL_V:COL_V + W_B])
        vn = _head_rmsnorm(v, _row(R.v_norm_g, l))
        vn_s_ref[l] = vn
        ws00 = _per_head_lanes([R.ws[l, hh, 0:1, 0:1] for hh in range(N_HEADS_B)])
        b0 = _per_head_lanes([b_s_ref[l, hh:hh + 1, 0:1] for hh in range(N_HEADS_B)])
        s = ws00 * vn + b0
        u = _dot(h, R.w_in[l, :, COL_U:COL_U + W_B])
        b_gate = _dot(h, R.w_in[l, :, COL_B_GATE:COL_B_GATE + W_B])
        b_out = (u * s) * _silu(b_gate)

        x = x + (_dot(a_out, R.w_out[l, 0:W_A, :]) + _dot(b_out, R.w_out[l, W_A:, :]))
    ys_ref[...] = _rmsnorm(x, fin_g)


def _trunk_kernel(x_ref, xs_ref, sp_ref, norm_g_ref, w_in_hbm, w_pool_ref, pool_scale_ref, v_norm_g_ref, ws_ref,
                  b_s_ref, w_out_hbm, fin_g_ref,
                  y_ref, pool_ref, cv_ref, ys_ref, a_in_s_ref, vn_s_ref,
                  a_hist, s2_hist, s4_hist, s8_hist, h_buf, x_buf, w_in_bf, w_out_bf, bias_buf, stage, sems):
    b = pl.program_id(0)
    t = pl.program_id(1)
    tm = x_ref.shape[1]
    rows = tm // ROW_BLOCKS
    R = _Refs(w_in=w_in_bf, w_pool=w_pool_ref, pool_scale=pool_scale_ref, v_norm_g=v_norm_g_ref, ws=ws_ref,
              bias=bias_buf, w_out=w_out_bf, a_hist=a_hist, s2_hist=s2_hist, s4_hist=s4_hist, s8_hist=s8_hist,
              h_buf=h_buf)
    hists = (a_hist, s2_hist, s4_hist, s8_hist)
    fin_g = fin_g_ref[...].reshape(1, D_MODEL)

    @pl.when(jnp.logical_and(b == 0, t == 0))
    def _():
        _load_weights_as_bf16(w_in_hbm, w_in_bf, stage, sems)
        _load_weights_as_bf16(w_out_hbm, w_out_bf, stage, sems)
        for l in range(DEPTH):
            for hh in range(N_HEADS_B):
                bias_buf[l, :, hh * HEAD_DIM_B:(hh + 1) * HEAD_DIM_B] = jnp.broadcast_to(
                    b_s_ref[l, hh:hh + 1, :], (CHUNK, CHUNK)).T
        _sample_group(xs_ref, sp_ref, norm_g_ref, b_s_ref, fin_g, R, ys_ref, a_in_s_ref, vn_s_ref)

    @pl.when(t == 0)
    def _():
        for buf in hists:
            buf[:, 0:HIST, :] = jnp.zeros((DEPTH, HIST, buf.shape[2]), F32)

    for l in range(DEPTH):
        def x_rows(rs, l=l):
            return x_ref[0, rs, :] if l == 0 else x_buf[rs, :]

        blocks = [slice(k * rows, (k + 1) * rows) for k in range(ROW_BLOCKS)]
        for rs in blocks:
            h_buf[rs, :] = _rmsnorm(x_rows(rs), _row(norm_g_ref, l)).astype(BF16)
        chains = [_Chain(l, k * rows, rows, t, tm, R) for k in range(ROW_BLOCKS)]
        for c in chains:
            c.p0()
        for c in chains:
            c.p1()
        for c in chains:
            c.p2()
        for buf in hists:
            buf[l, 0:HIST, :] = buf[l, tm:tm + HIST, :]
        pool_ref[l, 0] = chains[-1].pool_rows
        cv_ref[l, 0] = chains[-1].cv_rows
        for c, rs in zip(chains, blocks):
            c.p3()
            x_new = x_rows(rs) + c.out
            if l + 1 < DEPTH:
                x_buf[rs, :] = x_new
            else:
                y_ref[0, rs, :] = _rmsnorm(x_new, fin_g)


def _const_spec(shape):
    zeros = (0,) * len(shape)
    return pl.BlockSpec(shape, lambda b, t: zeros, pipeline_mode=pl.Buffered(1))


def _trunk_call(x, xs, sp_t, norm_g, w_in, w_pool, pool_scale, v_norm_g, w_s, b_s, w_out, fin_g):
    batch, seq, d = x.shape
    n_s = xs.shape[0]
    tm = ROW_TILE
    assert seq % tm == 0 and tm % (ROW_BLOCKS * CHUNK) == 0
    assert d % WEIGHT_CHUNK_ROWS == 0 and (W_A + W_B) % WEIGHT_CHUNK_ROWS == 0
    hbm = pl.BlockSpec(memory_space=pl.ANY)
    args = (x, xs, sp_t, norm_g, w_in, w_pool, pool_scale, v_norm_g, w_s, b_s, w_out, fin_g)
    in_specs = []
    for a in args:
        if a is x:
            in_specs.append(pl.BlockSpec((1, tm, d), lambda b, t: (b, t, 0)))
        elif a is w_in or a is w_out:
            in_specs.append(hbm)
        else:
            in_specs.append(_const_spec(a.shape))
    return pl.pallas_call(
        _trunk_kernel,
        grid=(batch, seq // tm),
        in_specs=in_specs,
        out_specs=[
            pl.BlockSpec((1, tm, d), lambda b, t: (b, t, 0)),
            pl.BlockSpec((DEPTH, 1, POOL_BUF, W_A), lambda b, t: (0, b, 0, 0)),
            pl.BlockSpec((DEPTH, 1, CHUNK, W_B), lambda b, t: (0, b, 0, 0)),
            pl.BlockSpec((n_s, d), lambda b, t: (0, 0)),
            pl.BlockSpec((DEPTH, n_s, W_A), lambda b, t: (0, 0, 0)),
            pl.BlockSpec((DEPTH, n_s, W_B), lambda b, t: (0, 0, 0)),
        ],
        out_shape=[
            jax.ShapeDtypeStruct((batch, seq, d), F32),
            jax.ShapeDtypeStruct((DEPTH, batch, POOL_BUF, W_A), F32),
            jax.ShapeDtypeStruct((DEPTH, batch, CHUNK, W_B), F32),
            jax.ShapeDtypeStruct((n_s, d), F32),
            jax.ShapeDtypeStruct((DEPTH, n_s, W_A), F32),
            jax.ShapeDtypeStruct((DEPTH, n_s, W_B), F32),
        ],
        scratch_shapes=[
            pltpu.VMEM((DEPTH, HIST + tm, W_A), F32),
            pltpu.VMEM((DEPTH, HIST + tm, W_A - POOL_GROUP_DIM), F32),
            pltpu.VMEM((DEPTH, HIST + tm, W_A - 2 * POOL_GROUP_DIM), F32),
            pltpu.VMEM((DEPTH, HIST + tm, W_A - 3 * POOL_GROUP_DIM), F32),
            pltpu.VMEM((tm, d), BF16),
            pltpu.VMEM((tm, d), F32),
            pltpu.VMEM((DEPTH, d, D_IN), BF16),
            pltpu.VMEM((DEPTH, W_A + W_B, d), BF16),
            pltpu.VMEM((DEPTH, CHUNK, W_B), F32),
            pltpu.VMEM((2, WEIGHT_CHUNK_ROWS, D_IN), F32),
            pltpu.SemaphoreType.DMA((2,)),
        ],
        compiler_params=pltpu.CompilerParams(
            dimension_semantics=("arbitrary", "arbitrary"),
            vmem_limit_bytes=VMEM_LIMIT_BYTES,
        ),
        name="trunk",
    )(*args)


def kernel(x_prompt, x_sample, state_pool, norm_g, w_in, w_pool, pool_scale, v_norm_g, w_s, b_s, w_out, final_norm_g):
    dec_batch, dec_seq, _ = x_sample.shape
    assert dec_seq == 1 and state_pool.shape[2] == POOL_BUF

    sp_t = jnp.transpose(state_pool, (0, 2, 1, 3))
    y_prompt, pool_prompt, chunk_v_prompt, y_s, a_in_s, vn_s = _trunk_call(
        x_prompt, x_sample.reshape(dec_batch, D_MODEL), sp_t, norm_g, w_in, w_pool, pool_scale, v_norm_g,
        w_s, b_s, w_out, final_norm_g)

    y_sample = y_s.reshape(dec_batch, 1, D_MODEL)
    is_new_row = lax.broadcasted_iota(jnp.int32, (1, 1, POOL_BUF, 1), 2) == POOL_BUF - 1
    pool_sample = jnp.where(is_new_row, a_in_s[:, :, None, :], jnp.roll(state_pool, -1, axis=2))
    chunk_v_sample = vn_s[:, :, None, :]
    return (y_prompt, y_sample, pool_prompt, pool_sample, chunk_v_prompt, chunk_v_sample)
```

```python
import jax
import jax.numpy as jnp
from jax import lax
from jax.experimental import pallas as pl
from jax.experimental.pallas import tpu as pltpu

D_MODEL = 1024
DEPTH = 2
W_A = 512
W_B = 512
D_IN = 2 * W_A + 3 * W_B
POOL_WINDOWS = (2, 4, 8, 16)
N_POOL_GROUPS = len(POOL_WINDOWS)
POOL_GROUP_DIM = W_A // N_POOL_GROUPS
POOL_BUF = max(POOL_WINDOWS) - 1
CHUNK = 128
N_HEADS_B = 4
HEAD_DIM_B = W_B // N_HEADS_B
EPS = 1e-6
SAMPLE_PAST_LEN = 16384

COL_A_IN = 0
COL_A_GATE = W_A
COL_U = 2 * W_A
COL_V = 2 * W_A + W_B
COL_B_GATE = 2 * W_A + 2 * W_B

HIST = 16
ROW_TILE = 512
ROW_BLOCKS = 2
WEIGHT_CHUNK_ROWS = 256
VMEM_LIMIT_BYTES = 56 * 1024 * 1024

BF16 = jnp.bfloat16
F32 = jnp.float32


def _dot(a, b):
    return jnp.dot(a.astype(BF16), b.astype(BF16), preferred_element_type=F32)


def _rmsnorm(x, g):
    ms = jnp.mean(x * x, axis=-1, keepdims=True)
    return (x * lax.rsqrt(ms + EPS)) * g


def _silu(x):
    hx = 0.5 * x
    return hx * jnp.tanh(hx) + hx


def _head_rmsnorm(v, g):
    parts = []
    for h in range(N_HEADS_B):
        sl = slice(h * HEAD_DIM_B, (h + 1) * HEAD_DIM_B)
        parts.append(_rmsnorm(v[:, sl], g[:, sl]))
    return jnp.concatenate(parts, axis=-1)


def _masked_ws(ws_ref, l):
    row = lax.broadcasted_iota(jnp.int32, (CHUNK, CHUNK), 0)
    col = lax.broadcasted_iota(jnp.int32, (CHUNK, CHUNK), 1)
    keep = col <= row
    return [jnp.where(keep, ws_ref[l, h], 0.0).astype(BF16) for h in range(N_HEADS_B)]


class _Refs:
    def __init__(self, **kw):
        self.__dict__.update(kw)


def _row(ref, l):
    return ref[l:l + 1, :]


def _load_weights_as_bf16(w_hbm, w_bf, stage, sems):
    _, k, n = w_hbm.shape
    chunks = [(l, r) for l in range(DEPTH) for r in range(0, k, WEIGHT_CHUNK_ROWS)]

    def copy(i):
        l, r = chunks[i]
        slot = i % 2
        return pltpu.make_async_copy(
            w_hbm.at[l, pl.ds(r, WEIGHT_CHUNK_ROWS), :], stage.at[slot, :, pl.ds(0, n)], sems.at[slot])

    copy(0).start()
    for i, (l, r) in enumerate(chunks):
        if i + 1 < len(chunks):
            copy(i + 1).start()
        copy(i).wait()
        w_bf[l, r:r + WEIGHT_CHUNK_ROWS, :] = stage[i % 2, :, 0:n].astype(BF16)


class _Chain:
    def __init__(self, l, r0, rows, t, tm, R):
        self.l, self.r0, self.rows, self.t, self.tm, self.R = l, r0, rows, t, tm, R

    def _in(self, col, width):
        R = self.R
        return jnp.dot(R.h_buf[self.r0:self.r0 + self.rows, :], R.w_in[self.l, :, col:col + width],
                       preferred_element_type=F32)

    def p0(self):
        self.v = self._in(COL_V, W_B)
        self.a_in = self._in(COL_A_IN, W_A)

    def p1(self):
        self.a_gate = self._in(COL_A_GATE, W_A)
        self.u = self._in(COL_U, W_B)

    def p2(self):
        l, r0, rows, R = self.l, self.r0, self.rows, self.R
        n_chunks = rows // CHUNK
        a_in = self.a_in
        base = HIST + r0

        vn = _head_rmsnorm(self.v, _row(R.v_norm_g, l))
        self.cv_rows = vn[rows - CHUNK:, :]
        vn_b = vn.astype(BF16)

        R.a_hist[l, base:base + rows, :] = a_in
        s2 = a_in + R.a_hist[l, base - 1:base - 1 + rows, :]
        R.s2_hist[l, base:base + rows, :] = s2[:, POOL_GROUP_DIM:]
        s4 = s2[:, POOL_GROUP_DIM:] + R.s2_hist[l, base - 2:base - 2 + rows, :]
        R.s4_hist[l, base:base + rows, :] = s4[:, POOL_GROUP_DIM:]
        s8 = s4[:, POOL_GROUP_DIM:] + R.s4_hist[l, base - 4:base - 4 + rows, :]
        R.s8_hist[l, base:base + rows, :] = s8[:, POOL_GROUP_DIM:]
        s16 = s8[:, POOL_GROUP_DIM:] + R.s8_hist[l, base - 8:base - 8 + rows, :]
        win_sums = (s2[:, :POOL_GROUP_DIM], s4[:, :POOL_GROUP_DIM], s8[:, :POOL_GROUP_DIM], s16)
        self.pool_rows = a_in[rows - POOL_BUF:, :]

        d_parts = []
        for g, w in enumerate(POOL_WINDOWS):
            sl = slice(g * POOL_GROUP_DIM, (g + 1) * POOL_GROUP_DIM)
            ssum = win_sums[g]
            if r0 == 0:
                pos1 = lax.broadcasted_iota(jnp.int32, (HIST, POOL_GROUP_DIM), 0) + self.t * self.tm + 1
                inv_head = 1.0 / jnp.minimum(pos1, w).astype(F32)
                mean = jnp.concatenate([ssum[:HIST] * inv_head, ssum[HIST:] * (1.0 / w)], axis=0)
            else:
                mean = ssum * (1.0 / w)
            d_parts.append((mean - a_in[:, sl]).astype(BF16))

        ws_b = _masked_ws(R.ws, l)
        s_heads = []
        for hh in range(N_HEADS_B):
            sl = slice(hh * HEAD_DIM_B, (hh + 1) * HEAD_DIM_B)
            rhs = jnp.concatenate([vn_b[c * CHUNK:(c + 1) * CHUNK, sl] for c in range(n_chunks)], axis=1)
            res = jnp.dot(ws_b[hh], rhs, preferred_element_type=F32)
            s_heads.append(jnp.concatenate(
                [res[:, c * HEAD_DIM_B:(c + 1) * HEAD_DIM_B] for c in range(n_chunks)], axis=0))
        self.s_heads = s_heads

        self.y_parts = [_dot(d_parts[g], R.w_pool[l, g]) for g in range(N_POOL_GROUPS)]
        self.b_gate = self._in(COL_B_GATE, W_B)

    def p3(self):
        l, rows, R = self.l, self.rows, self.R
        n_chunks = rows // CHUNK
        a_out = ((jnp.concatenate(self.y_parts, axis=-1) * _row(R.pool_scale, l)) * _silu(self.a_gate)).astype(BF16)
        out_a = _dot(a_out, R.w_out[l, 0:W_A, :])
        s = jnp.concatenate(self.s_heads, axis=-1) + jnp.concatenate([R.bias[l]] * n_chunks, axis=0)
        b_out = ((self.u * s) * _silu(self.b_gate)).astype(BF16)
        self.out = out_a + _dot(b_out, R.w_out[l, W_A:, :])


def _per_head_lanes(vals):
    return jnp.concatenate([jnp.broadcast_to(v, (1, HEAD_DIM_B)) for v in vals], axis=-1)


def _sample_group(xs_ref, sp_ref, norm_g_ref, b_s_ref, fin_g, R, ys_ref, a_in_s_ref, vn_s_ref):
    x = xs_ref[...]
    for l in range(DEPTH):
        h = _rmsnorm(x, _row(norm_g_ref, l)).astype(BF16)
        a_in = _dot(h, R.w_in[l, :, COL_A_IN:COL_A_IN + W_A])
        a_in_s_ref[l] = a_in
        a_gate = _dot(h, R.w_in[l, :, COL_A_GATE:COL_A_GATE + W_A])
        a_parts = []
        for g, w in enumerate(POOL_WINDOWS):
            sl = slice(g * POOL_GROUP_DIM, (g + 1) * POOL_GROUP_DIM)
            ssum = a_in[:, sl]
            for k in range(1, w):
                ssum = ssum + sp_ref[l, POOL_BUF - k, :, sl]
            cnt = float(min(SAMPLE_PAST_LEN + 1, w))
            d = ssum / cnt - a_in[:, sl]
            yg = _dot(d, R.w_pool[l, g]) * _row(R.pool_scale, l)[:, sl]
            a_parts.append(yg * _silu(a_gate[:, sl]))
        a_out = jnp.concatenate(a_parts, axis=-1)

        v = _dot(h, R.w_in[l, :, COL_V:COL_V + W_B])
        vn = _head_rmsnorm(v, _row(R.v_norm_g, l))
        vn_s_ref[l] = vn
        ws00 = _per_head_lanes([R.ws[l, hh, 0:1, 0:1] for hh in range(N_HEADS_B)])
        b0 = _per_head_lanes([b_s_ref[l, hh:hh + 1, 0:1] for hh in range(N_HEADS_B)])
        s = ws00 * vn + b0
        u = _dot(h, R.w_in[l, :, COL_U:COL_U + W_B])
        b_gate = _dot(h, R.w_in[l, :, COL_B_GATE:COL_B_GATE + W_B])
        b_out = (u * s) * _silu(b_gate)

        x = x + (_dot(a_out, R.w_out[l, 0:W_A, :]) + _dot(b_out, R.w_out[l, W_A:, :]))
    ys_ref[...] = _rmsnorm(x, fin_g)


def _trunk_kernel(x_ref, xs_ref, sp_hbm, norm_g_ref, w_in_hbm, w_pool_ref, pool_scale_ref, v_norm_g_ref, ws_ref,
                  b_s_ref, w_out_hbm, fin_g_ref,
                  y_ref, pool_ref, cv_ref, ys_ref, a_in_s_ref, vn_s_ref,
                  a_hist, s2_hist, s4_hist, s8_hist, h_buf, x_buf, w_in_bf, w_out_bf, bias_buf, sp_buf,
                  stage, sems, sp_sems):
    b = pl.program_id(0)
    t = pl.program_id(1)
    tm = x_ref.shape[1]
    rows = tm // ROW_BLOCKS
    R = _Refs(w_in=w_in_bf, w_pool=w_pool_ref, pool_scale=pool_scale_ref, v_norm_g=v_norm_g_ref, ws=ws_ref,
              bias=bias_buf, w_out=w_out_bf, a_hist=a_hist, s2_hist=s2_hist, s4_hist=s4_hist, s8_hist=s8_hist,
              h_buf=h_buf)
    hists = (a_hist, s2_hist, s4_hist, s8_hist)
    fin_g = fin_g_ref[...].reshape(1, D_MODEL)

    @pl.when(jnp.logical_and(b == 0, t == 0))
    def _():
        sp_copy = pltpu.make_async_copy(sp_hbm, sp_buf, sp_sems.at[0])
        sp_copy.start()
        _load_weights_as_bf16(w_in_hbm, w_in_bf, stage, sems)
        _load_weights_as_bf16(w_out_hbm, w_out_bf, stage, sems)
        sp_copy.wait()
        for l in range(DEPTH):
            for hh in range(N_HEADS_B):
                bias_buf[l, :, hh * HEAD_DIM_B:(hh + 1) * HEAD_DIM_B] = jnp.broadcast_to(
                    b_s_ref[l, hh:hh + 1, :], (CHUNK, CHUNK)).T
        _sample_group(xs_ref, sp_buf, norm_g_ref, b_s_ref, fin_g, R, ys_ref, a_in_s_ref, vn_s_ref)

    @pl.when(t == 0)
    def _():
        for buf in hists:
            buf[:, 0:HIST, :] = jnp.zeros((DEPTH, HIST, buf.shape[2]), F32)

    for l in range(DEPTH):
        def x_rows(rs, l=l):
            return x_ref[0, rs, :] if l == 0 else x_buf[rs, :]

        blocks = [slice(k * rows, (k + 1) * rows) for k in range(ROW_BLOCKS)]
        for rs in blocks:
            h_buf[rs, :] = _rmsnorm(x_rows(rs), _row(norm_g_ref, l)).astype(BF16)
        chains = [_Chain(l, k * rows, rows, t, tm, R) for k in range(ROW_BLOCKS)]
        for c in chains:
            c.p0()
        for c in chains:
            c.p1()
        for c in chains:
            c.p2()
        for buf in hists:
            buf[l, 0:HIST, :] = buf[l, tm:tm + HIST, :]
        pool_ref[l, 0] = chains[-1].pool_rows
        cv_ref[l, 0] = chains[-1].cv_rows
        for c, rs in zip(chains, blocks):
            c.p3()
            x_new = x_rows(rs) + c.out
            if l + 1 < DEPTH:
                x_buf[rs, :] = x_new
            else:
                y_ref[0, rs, :] = _rmsnorm(x_new, fin_g)


def _const_spec(shape):
    zeros = (0,) * len(shape)
    return pl.BlockSpec(shape, lambda b, t: zeros, pipeline_mode=pl.Buffered(1))


def _trunk_call(x, xs, sp_t, norm_g, w_in, w_pool, pool_scale, v_norm_g, w_s, b_s, w_out, fin_g):
    batch, seq, d = x.shape
    n_s = xs.shape[0]
    tm = ROW_TILE
    assert seq % tm == 0 and tm % (ROW_BLOCKS * CHUNK) == 0
    assert d % WEIGHT_CHUNK_ROWS == 0 and (W_A + W_B) % WEIGHT_CHUNK_ROWS == 0
    hbm = pl.BlockSpec(memory_space=pl.ANY)
    args = (x, xs, sp_t, norm_g, w_in, w_pool, pool_scale, v_norm_g, w_s, b_s, w_out, fin_g)
    in_specs = []
    for a in args:
        if a is x:
            in_specs.append(pl.BlockSpec((1, tm, d), lambda b, t: (b, t, 0)))
        elif a is w_in or a is w_out or a is sp_t:
            in_specs.append(hbm)
        else:
            in_specs.append(_const_spec(a.shape))
    return pl.pallas_call(
        _trunk_kernel,
        grid=(batch, seq // tm),
        in_specs=in_specs,
        out_specs=[
            pl.BlockSpec((1, tm, d), lambda b, t: (b, t, 0)),
            pl.BlockSpec((DEPTH, 1, POOL_BUF, W_A), lambda b, t: (0, b, 0, 0)),
            pl.BlockSpec((DEPTH, 1, CHUNK, W_B), lambda b, t: (0, b, 0, 0)),
            pl.BlockSpec((n_s, d), lambda b, t: (0, 0)),
            pl.BlockSpec((DEPTH, n_s, W_A), lambda b, t: (0, 0, 0)),
            pl.BlockSpec((DEPTH, n_s, W_B), lambda b, t: (0, 0, 0)),
        ],
        out_shape=[
            jax.ShapeDtypeStruct((batch, seq, d), F32),
            jax.ShapeDtypeStruct((DEPTH, batch, POOL_BUF, W_A), F32),
            jax.ShapeDtypeStruct((DEPTH, batch, CHUNK, W_B), F32),
            jax.ShapeDtypeStruct((n_s, d), F32),
            jax.ShapeDtypeStruct((DEPTH, n_s, W_A), F32),
            jax.ShapeDtypeStruct((DEPTH, n_s, W_B), F32),
        ],
        scratch_shapes=[
            pltpu.VMEM((DEPTH, HIST + tm, W_A), F32),
            pltpu.VMEM((DEPTH, HIST + tm, W_A - POOL_GROUP_DIM), F32),
            pltpu.VMEM((DEPTH, HIST + tm, W_A - 2 * POOL_GROUP_DIM), F32),
            pltpu.VMEM((DEPTH, HIST + tm, W_A - 3 * POOL_GROUP_DIM), F32),
            pltpu.VMEM((tm, d), BF16),
            pltpu.VMEM((tm, d), F32),
            pltpu.VMEM((DEPTH, d, D_IN), BF16),
            pltpu.VMEM((DEPTH, W_A + W_B, d), BF16),
            pltpu.VMEM((DEPTH, CHUNK, W_B), F32),
            pltpu.VMEM((DEPTH, POOL_BUF, n_s, W_A), F32),
            pltpu.VMEM((2, WEIGHT_CHUNK_ROWS, D_IN), F32),
            pltpu.SemaphoreType.DMA((2,)),
            pltpu.SemaphoreType.DMA((1,)),
        ],
        compiler_params=pltpu.CompilerParams(
            dimension_semantics=("arbitrary", "arbitrary"),
            vmem_limit_bytes=VMEM_LIMIT_BYTES,
        ),
        name="trunk",
    )(*args)


def kernel(x_prompt, x_sample, state_pool, norm_g, w_in, w_pool, pool_scale, v_norm_g, w_s, b_s, w_out, final_norm_g):
    dec_batch, dec_seq, _ = x_sample.shape
    assert dec_seq == 1 and state_pool.shape[2] == POOL_BUF

    sp_t = jnp.transpose(state_pool, (0, 2, 1, 3))
    y_prompt, pool_prompt, chunk_v_prompt, y_s, a_in_s, vn_s = _trunk_call(
        x_prompt, x_sample.reshape(dec_batch, D_MODEL), sp_t, norm_g, w_in, w_pool, pool_scale, v_norm_g,
        w_s, b_s, w_out, final_norm_g)

    y_sample = y_s.reshape(dec_batch, 1, D_MODEL)
    is_new_row = lax.broadcasted_iota(jnp.int32, (1, 1, POOL_BUF, 1), 2) == POOL_BUF - 1
    pool_sample = jnp.where(is_new_row, a_in_s[:, :, None, :], jnp.roll(state_pool, -1, axis=2))
    chunk_v_sample = vn_s[:, :, None, :]
    return (y_prompt, y_sample, pool_prompt, pool_sample, chunk_v_prompt, chunk_v_sample)
```

```python
import jax
import jax.numpy as jnp
from jax import lax
from jax.experimental import pallas as pl
from jax.experimental.pallas import tpu as pltpu

D_MODEL = 1024
DEPTH = 2
W_A = 512
W_B = 512
D_IN = 2 * W_A + 3 * W_B
POOL_WINDOWS = (2, 4, 8, 16)
N_POOL_GROUPS = len(POOL_WINDOWS)
POOL_GROUP_DIM = W_A // N_POOL_GROUPS
POOL_BUF = max(POOL_WINDOWS) - 1
CHUNK = 128
N_HEADS_B = 4
HEAD_DIM_B = W_B // N_HEADS_B
EPS = 1e-6
SAMPLE_PAST_LEN = 16384

COL_A_IN = 0
COL_A_GATE = W_A
COL_U = 2 * W_A
COL_V = 2 * W_A + W_B
COL_B_GATE = 2 * W_A + 2 * W_B

HIST = 16
ROW_TILE = 512
ROW_BLOCKS = 2
WEIGHT_CHUNK_ROWS = 256
VMEM_LIMIT_BYTES = 56 * 1024 * 1024

BF16 = jnp.bfloat16
F32 = jnp.float32


def _dot(a, b):
    return jnp.dot(a.astype(BF16), b.astype(BF16), preferred_element_type=F32)


def _rmsnorm(x, g):
    ms = jnp.mean(x * x, axis=-1, keepdims=True)
    return (x * lax.rsqrt(ms + EPS)) * g


def _silu(x):
    hx = 0.5 * x
    return hx * jnp.tanh(hx) + hx


def _head_rmsnorm(v, g):
    parts = []
    for h in range(N_HEADS_B):
        sl = slice(h * HEAD_DIM_B, (h + 1) * HEAD_DIM_B)
        parts.append(_rmsnorm(v[:, sl], g[:, sl]))
    return jnp.concatenate(parts, axis=-1)


def _masked_ws(ws_ref, l):
    row = lax.broadcasted_iota(jnp.int32, (CHUNK, CHUNK), 0)
    col = lax.broadcasted_iota(jnp.int32, (CHUNK, CHUNK), 1)
    keep = col <= row
    return [jnp.where(keep, ws_ref[l, h], 0.0).astype(BF16) for h in range(N_HEADS_B)]


class _Refs:
    def __init__(self, **kw):
        self.__dict__.update(kw)


def _row(ref, l):
    return ref[l:l + 1, :]


def _load_weights_as_bf16(w_hbm, w_bf, stage, sems):
    _, k, n = w_hbm.shape
    chunks = [(l, r) for l in range(DEPTH) for r in range(0, k, WEIGHT_CHUNK_ROWS)]

    def copy(i):
        l, r = chunks[i]
        slot = i % 2
        return pltpu.make_async_copy(
            w_hbm.at[l, pl.ds(r, WEIGHT_CHUNK_ROWS), :], stage.at[slot, :, pl.ds(0, n)], sems.at[slot])

    copy(0).start()
    for i, (l, r) in enumerate(chunks):
        if i + 1 < len(chunks):
            copy(i + 1).start()
        copy(i).wait()
        w_bf[l, r:r + WEIGHT_CHUNK_ROWS, :] = stage[i % 2, :, 0:n].astype(BF16)


class _Chain:
    def __init__(self, l, r0, rows, t, tm, R):
        self.l, self.r0, self.rows, self.t, self.tm, self.R = l, r0, rows, t, tm, R

    def _in(self, col, width):
        R = self.R
        return jnp.dot(R.h_buf[self.r0:self.r0 + self.rows, :], R.w_in[self.l, :, col:col + width],
                       preferred_element_type=F32)

    def p0(self):
        self.v = self._in(COL_V, W_B)
        self.a_in = self._in(COL_A_IN, W_A)

    def p1(self):
        self.a_gate = self._in(COL_A_GATE, W_A)
        self.u = self._in(COL_U, W_B)

    def p2(self):
        l, r0, rows, R = self.l, self.r0, self.rows, self.R
        n_chunks = rows // CHUNK
        a_in = self.a_in
        base = HIST + r0

        vn = _head_rmsnorm(self.v, _row(R.v_norm_g, l))
        self.cv_rows = vn[rows - CHUNK:, :]
        vn_b = vn.astype(BF16)

        R.a_hist[l, base:base + rows, :] = a_in
        s2 = a_in + R.a_hist[l, base - 1:base - 1 + rows, :]
        R.s2_hist[l, base:base + rows, :] = s2[:, POOL_GROUP_DIM:]
        s4 = s2[:, POOL_GROUP_DIM:] + R.s2_hist[l, base - 2:base - 2 + rows, :]
        R.s4_hist[l, base:base + rows, :] = s4[:, POOL_GROUP_DIM:]
        s8 = s4[:, POOL_GROUP_DIM:] + R.s4_hist[l, base - 4:base - 4 + rows, :]
        R.s8_hist[l, base:base + rows, :] = s8[:, POOL_GROUP_DIM:]
        s16 = s8[:, POOL_GROUP_DIM:] + R.s8_hist[l, base - 8:base - 8 + rows, :]
        win_sums = (s2[:, :POOL_GROUP_DIM], s4[:, :POOL_GROUP_DIM], s8[:, :POOL_GROUP_DIM], s16)
        self.pool_rows = a_in[rows - POOL_BUF:, :]

        d_parts = []
        for g, w in enumerate(POOL_WINDOWS):
            sl = slice(g * POOL_GROUP_DIM, (g + 1) * POOL_GROUP_DIM)
            ssum = win_sums[g]
            if r0 == 0:
                pos1 = lax.broadcasted_iota(jnp.int32, (HIST, POOL_GROUP_DIM), 0) + self.t * self.tm + 1
                inv_head = 1.0 / jnp.minimum(pos1, w).astype(F32)
                mean = jnp.concatenate([ssum[:HIST] * inv_head, ssum[HIST:] * (1.0 / w)], axis=0)
            else:
                mean = ssum * (1.0 / w)
            d_parts.append((mean - a_in[:, sl]).astype(BF16))

        ws_b = _masked_ws(R.ws, l)
        s_heads = []
        for hh in range(N_HEADS_B):
            sl = slice(hh * HEAD_DIM_B, (hh + 1) * HEAD_DIM_B)
            rhs = jnp.concatenate([vn_b[c * CHUNK:(c + 1) * CHUNK, sl] for c in range(n_chunks)], axis=1)
            res = jnp.dot(ws_b[hh], rhs, preferred_element_type=F32)
            s_heads.append(jnp.concatenate(
                [res[:, c * HEAD_DIM_B:(c + 1) * HEAD_DIM_B] for c in range(n_chunks)], axis=0))
        self.s_heads = s_heads

        self.y_parts = [_dot(d_parts[g], R.w_pool[l, g]) for g in range(N_POOL_GROUPS)]
        self.b_gate = self._in(COL_B_GATE, W_B)

    def p3(self):
        l, rows, R = self.l, self.rows, self.R
        n_chunks = rows // CHUNK
        a_out = ((jnp.concatenate(self.y_parts, axis=-1) * _row(R.pool_scale, l)) * _silu(self.a_gate)).astype(BF16)
        out_a = _dot(a_out, R.w_out[l, 0:W_A, :])
        s = jnp.concatenate(self.s_heads, axis=-1) + jnp.concatenate([R.bias[l]] * n_chunks, axis=0)
        b_out = ((self.u * s) * _silu(self.b_gate)).astype(BF16)
        self.out = out_a + _dot(b_out, R.w_out[l, W_A:, :])


def _per_head_lanes(vals):
    return jnp.concatenate([jnp.broadcast_to(v, (1, HEAD_DIM_B)) for v in vals], axis=-1)


def _sample_group(xs_ref, sp_ref, norm_g_ref, b_s_ref, fin_g, R, ys_ref, a_in_s_ref, vn_s_ref):
    x = xs_ref[...]
    for l in range(DEPTH):
        h = _rmsnorm(x, _row(norm_g_ref, l)).astype(BF16)
        a_in = _dot(h, R.w_in[l, :, COL_A_IN:COL_A_IN + W_A])
        a_in_s_ref[l] = a_in
        a_gate = _dot(h, R.w_in[l, :, COL_A_GATE:COL_A_GATE + W_A])
        a_parts = []
        for g, w in enumerate(POOL_WINDOWS):
            sl = slice(g * POOL_GROUP_DIM, (g + 1) * POOL_GROUP_DIM)
            ssum = a_in[:, sl]
            for k in range(1, w):
                ssum = ssum + sp_ref[l, POOL_BUF - k, :, sl]
            cnt = float(min(SAMPLE_PAST_LEN + 1, w))
            d = ssum / cnt - a_in[:, sl]
            yg = _dot(d, R.w_pool[l, g]) * _row(R.pool_scale, l)[:, sl]
            a_parts.append(yg * _silu(a_gate[:, sl]))
        a_out = jnp.concatenate(a_parts, axis=-1)

        v = _dot(h, R.w_in[l, :, COL_V:COL_V + W_B])
        vn = _head_rmsnorm(v, _row(R.v_norm_g, l))
        vn_s_ref[l] = vn
        ws00 = _per_head_lanes([R.ws[l, hh, 0:1, 0:1] for hh in range(N_HEADS_B)])
        b0 = _per_head_lanes([b_s_ref[l, hh:hh + 1, 0:1] for hh in range(N_HEADS_B)])
        s = ws00 * vn + b0
        u = _dot(h, R.w_in[l, :, COL_U:COL_U + W_B])
        b_gate = _dot(h, R.w_in[l, :, COL_B_GATE:COL_B_GATE + W_B])
        b_out = (u * s) * _silu(b_gate)

        x = x + (_dot(a_out, R.w_out[l, 0:W_A, :]) + _dot(b_out, R.w_out[l, W_A:, :]))
    ys_ref[...] = _rmsnorm(x, fin_g)


def _trunk_kernel(x_ref, xs_hbm, sp_hbm, norm_g_ref, w_in_hbm, w_pool_ref, pool_scale_ref, v_norm_g_ref,
                  ws_ref, b_s_ref, w_out_hbm, fin_g_ref,
                  y_ref, pool_ref, cv_ref, ys_hbm, a_in_s_ref, vn_s_hbm,
                  a_hist, s2_hist, s4_hist, s8_hist, h_buf, x_buf, w_in_bf, w_out_bf, bias_buf, sp_buf,
                  xs_buf, ys_buf, vn_s_buf, stage, sems, io_sems):
    b = pl.program_id(0)
    t = pl.program_id(1)
    tm = x_ref.shape[1]
    rows = tm // ROW_BLOCKS
    R = _Refs(w_in=w_in_bf, w_pool=w_pool_ref, pool_scale=pool_scale_ref, v_norm_g=v_norm_g_ref, ws=ws_ref,
              bias=bias_buf, w_out=w_out_bf, a_hist=a_hist, s2_hist=s2_hist, s4_hist=s4_hist, s8_hist=s8_hist,
              h_buf=h_buf)
    hists = (a_hist, s2_hist, s4_hist, s8_hist)
    fin_g = fin_g_ref[...].reshape(1, D_MODEL)

    @pl.when(jnp.logical_and(b == 0, t == 0))
    def _():
        in_copies = [
            pltpu.make_async_copy(sp_hbm, sp_buf, io_sems.at[0]),
            pltpu.make_async_copy(xs_hbm.at[:, 0, :], xs_buf, io_sems.at[1]),
        ]
        for cp in in_copies:
            cp.start()
        _load_weights_as_bf16(w_in_hbm, w_in_bf, stage, sems)
        _load_weights_as_bf16(w_out_hbm, w_out_bf, stage, sems)
        for cp in in_copies:
            cp.wait()
        for l in range(DEPTH):
            for hh in range(N_HEADS_B):
                bias_buf[l, :, hh * HEAD_DIM_B:(hh + 1) * HEAD_DIM_B] = jnp.broadcast_to(
                    b_s_ref[l, hh:hh + 1, :], (CHUNK, CHUNK)).T
        _sample_group(xs_buf, sp_buf, norm_g_ref, b_s_ref, fin_g, R, ys_buf, a_in_s_ref, vn_s_buf)
        out_copies = [
            pltpu.make_async_copy(ys_buf, ys_hbm.at[:, 0, :], io_sems.at[0]),
            pltpu.make_async_copy(vn_s_buf, vn_s_hbm.at[:, :, 0, :], io_sems.at[1]),
        ]
        for cp in out_copies:
            cp.start()
        for cp in out_copies:
            cp.wait()

    @pl.when(t == 0)
    def _():
        for buf in hists:
            buf[:, 0:HIST, :] = jnp.zeros((DEPTH, HIST, buf.shape[2]), F32)

    for l in range(DEPTH):
        def x_rows(rs, l=l):
            return x_ref[0, rs, :] if l == 0 else x_buf[rs, :]

        blocks = [slice(k * rows, (k + 1) * rows) for k in range(ROW_BLOCKS)]
        for rs in blocks:
            h_buf[rs, :] = _rmsnorm(x_rows(rs), _row(norm_g_ref, l)).astype(BF16)
        chains = [_Chain(l, k * rows, rows, t, tm, R) for k in range(ROW_BLOCKS)]
        for c in chains:
            c.p0()
        for c in chains:
            c.p1()
        for c in chains:
            c.p2()
        for buf in hists:
            buf[l, 0:HIST, :] = buf[l, tm:tm + HIST, :]
        pool_ref[l, 0] = chains[-1].pool_rows
        cv_ref[l, 0] = chains[-1].cv_rows
        for c, rs in zip(chains, blocks):
            c.p3()
            x_new = x_rows(rs) + c.out
            if l + 1 < DEPTH:
                x_buf[rs, :] = x_new
            else:
                y_ref[0, rs, :] = _rmsnorm(x_new, fin_g)


def _const_spec(shape):
    zeros = (0,) * len(shape)
    return pl.BlockSpec(shape, lambda b, t: zeros, pipeline_mode=pl.Buffered(1))


def _trunk_call(x, xs, sp_t, norm_g, w_in, w_pool, pool_scale, v_norm_g, w_s, b_s, w_out, fin_g):
    batch, seq, d = x.shape
    n_s = xs.shape[0]
    tm = ROW_TILE
    assert seq % tm == 0 and tm % (ROW_BLOCKS * CHUNK) == 0
    assert d % WEIGHT_CHUNK_ROWS == 0 and (W_A + W_B) % WEIGHT_CHUNK_ROWS == 0
    hbm = pl.BlockSpec(memory_space=pl.ANY)
    args = (x, xs, sp_t, norm_g, w_in, w_pool, pool_scale, v_norm_g, w_s, b_s, w_out, fin_g)
    in_specs = []
    for a in args:
        if a is x:
            in_specs.append(pl.BlockSpec((1, tm, d), lambda b, t: (b, t, 0)))
        elif a is w_in or a is w_out or a is sp_t or a is xs:
            in_specs.append(hbm)
        else:
            in_specs.append(_const_spec(a.shape))
    return pl.pallas_call(
        _trunk_kernel,
        grid=(batch, seq // tm),
        in_specs=in_specs,
        out_specs=[
            pl.BlockSpec((1, tm, d), lambda b, t: (b, t, 0)),
            pl.BlockSpec((DEPTH, 1, POOL_BUF, W_A), lambda b, t: (0, b, 0, 0)),
            pl.BlockSpec((DEPTH, 1, CHUNK, W_B), lambda b, t: (0, b, 0, 0)),
            hbm,
            pl.BlockSpec((DEPTH, n_s, W_A), lambda b, t: (0, 0, 0)),
            hbm,
        ],
        out_shape=[
            jax.ShapeDtypeStruct((batch, seq, d), F32),
            jax.ShapeDtypeStruct((DEPTH, batch, POOL_BUF, W_A), F32),
            jax.ShapeDtypeStruct((DEPTH, batch, CHUNK, W_B), F32),
            jax.ShapeDtypeStruct((n_s, 1, d), F32),
            jax.ShapeDtypeStruct((DEPTH, n_s, W_A), F32),
            jax.ShapeDtypeStruct((DEPTH, n_s, 1, W_B), F32),
        ],
        scratch_shapes=[
            pltpu.VMEM((DEPTH, HIST + tm, W_A), F32),
            pltpu.VMEM((DEPTH, HIST + tm, W_A - POOL_GROUP_DIM), F32),
            pltpu.VMEM((DEPTH, HIST + tm, W_A - 2 * POOL_GROUP_DIM), F32),
            pltpu.VMEM((DEPTH, HIST + tm, W_A - 3 * POOL_GROUP_DIM), F32),
            pltpu.VMEM((tm, d), BF16),
            pltpu.VMEM((tm, d), F32),
            pltpu.VMEM((DEPTH, d, D_IN), BF16),
            pltpu.VMEM((DEPTH, W_A + W_B, d), BF16),
            pltpu.VMEM((DEPTH, CHUNK, W_B), F32),
            pltpu.VMEM((DEPTH, POOL_BUF, n_s, W_A), F32),
            pltpu.VMEM((n_s, d), F32),
            pltpu.VMEM((n_s, d), F32),
            pltpu.VMEM((DEPTH, n_s, W_B), F32),
            pltpu.VMEM((2, WEIGHT_CHUNK_ROWS, D_IN), F32),
            pltpu.SemaphoreType.DMA((2,)),
            pltpu.SemaphoreType.DMA((2,)),
        ],
        compiler_params=pltpu.CompilerParams(
            dimension_semantics=("arbitrary", "arbitrary"),
            vmem_limit_bytes=VMEM_LIMIT_BYTES,
        ),
        name="trunk",
    )(*args)


def kernel(x_prompt, x_sample, state_pool, norm_g, w_in, w_pool, pool_scale, v_norm_g, w_s, b_s, w_out, final_norm_g):
    assert x_sample.shape[1] == 1 and state_pool.shape[2] == POOL_BUF
    sp_t = jnp.transpose(state_pool, (0, 2, 1, 3))
    y_prompt, pool_prompt, chunk_v_prompt, y_sample, a_in_s, chunk_v_sample = _trunk_call(
        x_prompt, x_sample, sp_t, norm_g, w_in, w_pool, pool_scale, v_norm_g, w_s, b_s, w_out, final_norm_g)
    is_new_row = lax.broadcasted_iota(jnp.int32, (1, 1, POOL_BUF, 1), 2) == POOL_BUF - 1
    pool_sample = jnp.where(is_new_row, a_in_s[:, :, None, :], jnp.roll(state_pool, -1, axis=2))
    return (y_prompt, y_sample, pool_prompt, pool_sample, chunk_v_prompt, chunk_v_sample)
```

```python
import jax
import jax.numpy as jnp
from jax import lax
from jax.experimental import pallas as pl
from jax.experimental.pallas import tpu as pltpu

D_MODEL = 1024
DEPTH = 2
W_A = 512
W_B = 512
D_IN = 2 * W_A + 3 * W_B
POOL_WINDOWS = (2, 4, 8, 16)
N_POOL_GROUPS = len(POOL_WINDOWS)
POOL_GROUP_DIM = W_A // N_POOL_GROUPS
POOL_BUF = max(POOL_WINDOWS) - 1
CHUNK = 128
N_HEADS_B = 4
HEAD_DIM_B = W_B // N_HEADS_B
EPS = 1e-6
SAMPLE_PAST_LEN = 16384

COL_A_IN = 0
COL_A_GATE = W_A
COL_U = 2 * W_A
COL_V = 2 * W_A + W_B
COL_B_GATE = 2 * W_A + 2 * W_B

HIST = 16
ROW_TILE = 512
ROW_BLOCKS = 2
WEIGHT_CHUNK_ROWS = 256
VMEM_LIMIT_BYTES = 56 * 1024 * 1024

BF16 = jnp.bfloat16
F32 = jnp.float32


def _dot(a, b):
    return jnp.dot(a.astype(BF16), b.astype(BF16), preferred_element_type=F32)


def _rmsnorm(x, g):
    ms = jnp.mean(x * x, axis=-1, keepdims=True)
    return (x * lax.rsqrt(ms + EPS)) * g


def _silu(x):
    hx = 0.5 * x
    return hx * jnp.tanh(hx) + hx


def _head_rmsnorm(v, g):
    parts = []
    for h in range(N_HEADS_B):
        sl = slice(h * HEAD_DIM_B, (h + 1) * HEAD_DIM_B)
        parts.append(_rmsnorm(v[:, sl], g[:, sl]))
    return jnp.concatenate(parts, axis=-1)


def _masked_ws(ws_ref, l):
    row = lax.broadcasted_iota(jnp.int32, (CHUNK, CHUNK), 0)
    col = lax.broadcasted_iota(jnp.int32, (CHUNK, CHUNK), 1)
    keep = col <= row
    return [jnp.where(keep, ws_ref[l, h], 0.0).astype(BF16) for h in range(N_HEADS_B)]


class _Refs:
    def __init__(self, **kw):
        self.__dict__.update(kw)


def _row(ref, l):
    return ref[l:l + 1, :]


class _WeightStream:
    def __init__(self, w_in_hbm, w_out_hbm, w_in_bf, w_out_bf, stage, sems):
        self.chunks = []
        for l in range(DEPTH):
            for src, dst in ((w_in_hbm, w_in_bf), (w_out_hbm, w_out_bf)):
                _, k, n = src.shape
                self.chunks += [(src, dst, l, r, n) for r in range(0, k, WEIGHT_CHUNK_ROWS)]
        self.per_layer = len(self.chunks) // DEPTH
        self.stage, self.sems = stage, sems
        self.done = 0

    def _copy(self, i):
        src, _, l, r, n = self.chunks[i]
        slot = i % 2
        return pltpu.make_async_copy(
            src.at[l, pl.ds(r, WEIGHT_CHUNK_ROWS), :], self.stage.at[slot, :, pl.ds(0, n)], self.sems.at[slot])

    def start(self):
        self._copy(0).start()

    def service(self):
        i = self.done
        if i == len(self.chunks):
            return
        if i + 1 < len(self.chunks):
            self._copy(i + 1).start()
        self._copy(i).wait()
        _, dst, l, r, n = self.chunks[i]
        dst[l, r:r + WEIGHT_CHUNK_ROWS, :] = self.stage[i % 2, :, 0:n].astype(BF16)
        self.done += 1

    def finish_layer(self, l):
        while self.done < self.per_layer * (l + 1):
            self.service()


class _Chain:
    def __init__(self, l, r0, rows, t, tm, R):
        self.l, self.r0, self.rows, self.t, self.tm, self.R = l, r0, rows, t, tm, R

    def _in(self, col, width):
        R = self.R
        return jnp.dot(R.h_buf[self.r0:self.r0 + self.rows, :], R.w_in[self.l, :, col:col + width],
                       preferred_element_type=F32)

    def p0(self):
        self.v = self._in(COL_V, W_B)
        self.a_in = self._in(COL_A_IN, W_A)

    def p1(self):
        self.a_gate = self._in(COL_A_GATE, W_A)
        self.u = self._in(COL_U, W_B)

    def p2(self):
        l, r0, rows, R = self.l, self.r0, self.rows, self.R
        n_chunks = rows // CHUNK
        a_in = self.a_in
        base = HIST + r0

        vn = _head_rmsnorm(self.v, _row(R.v_norm_g, l))
        self.cv_rows = vn[rows - CHUNK:, :]
        vn_b = vn.astype(BF16)

        R.a_hist[l, base:base + rows, :] = a_in
        s2 = a_in + R.a_hist[l, base - 1:base - 1 + rows, :]
        R.s2_hist[l, base:base + rows, :] = s2[:, POOL_GROUP_DIM:]
        s4 = s2[:, POOL_GROUP_DIM:] + R.s2_hist[l, base - 2:base - 2 + rows, :]
        R.s4_hist[l, base:base + rows, :] = s4[:, POOL_GROUP_DIM:]
        s8 = s4[:, POOL_GROUP_DIM:] + R.s4_hist[l, base - 4:base - 4 + rows, :]
        R.s8_hist[l, base:base + rows, :] = s8[:, POOL_GROUP_DIM:]
        s16 = s8[:, POOL_GROUP_DIM:] + R.s8_hist[l, base - 8:base - 8 + rows, :]
        win_sums = (s2[:, :POOL_GROUP_DIM], s4[:, :POOL_GROUP_DIM], s8[:, :POOL_GROUP_DIM], s16)
        self.pool_rows = a_in[rows - POOL_BUF:, :]

        d_parts = []
        for g, w in enumerate(POOL_WINDOWS):
            sl = slice(g * POOL_GROUP_DIM, (g + 1) * POOL_GROUP_DIM)
            ssum = win_sums[g]
            if r0 == 0:
                pos1 = lax.broadcasted_iota(jnp.int32, (HIST, POOL_GROUP_DIM), 0) + self.t * self.tm + 1
                inv_head = 1.0 / jnp.minimum(pos1, w).astype(F32)
                mean = jnp.concatenate([ssum[:HIST] * inv_head, ssum[HIST:] * (1.0 / w)], axis=0)
            else:
                mean = ssum * (1.0 / w)
            d_parts.append((mean - a_in[:, sl]).astype(BF16))

        ws_b = _masked_ws(R.ws, l)
        s_heads = []
        for hh in range(N_HEADS_B):
            sl = slice(hh * HEAD_DIM_B, (hh + 1) * HEAD_DIM_B)
            rhs = jnp.concatenate([vn_b[c * CHUNK:(c + 1) * CHUNK, sl] for c in range(n_chunks)], axis=1)
            res = jnp.dot(ws_b[hh], rhs, preferred_element_type=F32)
            s_heads.append(jnp.concatenate(
                [res[:, c * HEAD_DIM_B:(c + 1) * HEAD_DIM_B] for c in range(n_chunks)], axis=0))
        self.s_heads = s_heads

        self.y_parts = [_dot(d_parts[g], R.w_pool[l, g]) for g in range(N_POOL_GROUPS)]
        self.b_gate = self._in(COL_B_GATE, W_B)

    def p3(self):
        l, rows, R = self.l, self.rows, self.R
        n_chunks = rows // CHUNK
        a_out = ((jnp.concatenate(self.y_parts, axis=-1) * _row(R.pool_scale, l)) * _silu(self.a_gate)).astype(BF16)
        out_a = _dot(a_out, R.w_out[l, 0:W_A, :])
        s = jnp.concatenate(self.s_heads, axis=-1) + jnp.concatenate([R.bias[l]] * n_chunks, axis=0)
        b_out = ((self.u * s) * _silu(self.b_gate)).astype(BF16)
        self.out = out_a + _dot(b_out, R.w_out[l, W_A:, :])


def _per_head_lanes(vals):
    return jnp.concatenate([jnp.broadcast_to(v, (1, HEAD_DIM_B)) for v in vals], axis=-1)


def _sample_group(xs_ref, sp_ref, norm_g_ref, b_s_ref, fin_g, R, ys_ref, a_in_s_ref, vn_s_ref):
    x = xs_ref[...]
    for l in range(DEPTH):
        h = _rmsnorm(x, _row(norm_g_ref, l)).astype(BF16)
        a_in = _dot(h, R.w_in[l, :, COL_A_IN:COL_A_IN + W_A])
        a_in_s_ref[l] = a_in
        a_gate = _dot(h, R.w_in[l, :, COL_A_GATE:COL_A_GATE + W_A])
        a_parts = []
        for g, w in enumerate(POOL_WINDOWS):
            sl = slice(g * POOL_GROUP_DIM, (g + 1) * POOL_GROUP_DIM)
            ssum = a_in[:, sl]
            for k in range(1, w):
                ssum = ssum + sp_ref[l, POOL_BUF - k, :, sl]
            cnt = float(min(SAMPLE_PAST_LEN + 1, w))
            d = ssum / cnt - a_in[:, sl]
            yg = _dot(d, R.w_pool[l, g]) * _row(R.pool_scale, l)[:, sl]
            a_parts.append(yg * _silu(a_gate[:, sl]))
        a_out = jnp.concatenate(a_parts, axis=-1)

        v = _dot(h, R.w_in[l, :, COL_V:COL_V + W_B])
        vn = _head_rmsnorm(v, _row(R.v_norm_g, l))
        vn_s_ref[l] = vn
        ws00 = _per_head_lanes([R.ws[l, hh, 0:1, 0:1] for hh in range(N_HEADS_B)])
        b0 = _per_head_lanes([b_s_ref[l, hh:hh + 1, 0:1] for hh in range(N_HEADS_B)])
        s = ws00 * vn + b0
        u = _dot(h, R.w_in[l, :, COL_U:COL_U + W_B])
        b_gate = _dot(h, R.w_in[l, :, COL_B_GATE:COL_B_GATE + W_B])
        b_out = (u * s) * _silu(b_gate)

        x = x + (_dot(a_out, R.w_out[l, 0:W_A, :]) + _dot(b_out, R.w_out[l, W_A:, :]))
    ys_ref[...] = _rmsnorm(x, fin_g)


def _trunk_kernel(x_ref, xs_hbm, sp_hbm, norm_g_ref, w_in_hbm, w_pool_ref, pool_scale_ref, v_norm_g_ref,
                  ws_ref, b_s_ref, w_out_hbm, fin_g_ref,
                  y_ref, pool_ref, cv_ref, ys_hbm, a_in_s_ref, vn_s_hbm,
                  a_hist, s2_hist, s4_hist, s8_hist, h_buf, x_buf, w_in_bf, w_out_bf, bias_buf, sp_buf,
                  xs_buf, ys_buf, vn_s_buf, stage, sems, io_sems):
    b = pl.program_id(0)
    t = pl.program_id(1)
    tm = x_ref.shape[1]
    rows = tm // ROW_BLOCKS
    R = _Refs(w_in=w_in_bf, w_pool=w_pool_ref, pool_scale=pool_scale_ref, v_norm_g=v_norm_g_ref, ws=ws_ref,
              bias=bias_buf, w_out=w_out_bf, a_hist=a_hist, s2_hist=s2_hist, s4_hist=s4_hist, s8_hist=s8_hist,
              h_buf=h_buf)
    hists = (a_hist, s2_hist, s4_hist, s8_hist)
    fin_g = fin_g_ref[...].reshape(1, D_MODEL)

    def tile_body(before_layer=None, between_phases=None):
        for l in range(DEPTH):
            if before_layer is not None:
                before_layer(l)

            def x_rows(rs, l=l):
                return x_ref[0, rs, :] if l == 0 else x_buf[rs, :]

            def phase_done():
                if between_phases is not None:
                    between_phases()

            blocks = [slice(k * rows, (k + 1) * rows) for k in range(ROW_BLOCKS)]
            for rs in blocks:
                h_buf[rs, :] = _rmsnorm(x_rows(rs), _row(norm_g_ref, l)).astype(BF16)
            chains = [_Chain(l, k * rows, rows, t, tm, R) for k in range(ROW_BLOCKS)]
            for c in chains:
                c.p0()
            phase_done()
            for c in chains:
                c.p1()
            phase_done()
            for c in chains:
                c.p2()
                phase_done()
            for buf in hists:
                buf[l, 0:HIST, :] = buf[l, tm:tm + HIST, :]
            pool_ref[l, 0] = chains[-1].pool_rows
            cv_ref[l, 0] = chains[-1].cv_rows
            for c, rs in zip(chains, blocks):
                c.p3()
                x_new = x_rows(rs) + c.out
                if l + 1 < DEPTH:
                    x_buf[rs, :] = x_new
                else:
                    y_ref[0, rs, :] = _rmsnorm(x_new, fin_g)
                phase_done()

    def zero_hist():
        for buf in hists:
            buf[:, 0:HIST, :] = jnp.zeros((DEPTH, HIST, buf.shape[2]), F32)

    first = jnp.logical_and(b == 0, t == 0)

    @pl.when(first)
    def _():
        stream = _WeightStream(w_in_hbm, w_out_hbm, w_in_bf, w_out_bf, stage, sems)
        stream.start()
        in_copies = [
            pltpu.make_async_copy(sp_hbm, sp_buf, io_sems.at[0]),
            pltpu.make_async_copy(xs_hbm.at[:, 0, :], xs_buf, io_sems.at[1]),
        ]
        for l in range(DEPTH):
            for hh in range(N_HEADS_B):
                bias_buf[l, :, hh * HEAD_DIM_B:(hh + 1) * HEAD_DIM_B] = jnp.broadcast_to(
                    b_s_ref[l, hh:hh + 1, :], (CHUNK, CHUNK)).T
        zero_hist()

        def before_layer(l):
            stream.finish_layer(l)
            if l == 0:
                for cp in in_copies:
                    cp.start()

        tile_body(before_layer, stream.service)
        for cp in in_copies:
            cp.wait()
        _sample_group(xs_buf, sp_buf, norm_g_ref, b_s_ref, fin_g, R, ys_buf, a_in_s_ref, vn_s_buf)
        out_copies = [
            pltpu.make_async_copy(ys_buf, ys_hbm.at[:, 0, :], io_sems.at[0]),
            pltpu.make_async_copy(vn_s_buf, vn_s_hbm.at[:, :, 0, :], io_sems.at[1]),
        ]
        for cp in out_copies:
            cp.start()
        for cp in out_copies:
            cp.wait()

    @pl.when(jnp.logical_not(first))
    def _():
        pl.when(t == 0)(zero_hist)
        tile_body()


def _const_spec(shape):
    zeros = (0,) * len(shape)
    return pl.BlockSpec(shape, lambda b, t: zeros, pipeline_mode=pl.Buffered(1))


def _trunk_call(x, xs, sp_t, norm_g, w_in, w_pool, pool_scale, v_norm_g, w_s, b_s, w_out, fin_g):
    batch, seq, d = x.shape
    n_s = xs.shape[0]
    tm = ROW_TILE
    assert seq % tm == 0 and tm % (ROW_BLOCKS * CHUNK) == 0
    assert d % WEIGHT_CHUNK_ROWS == 0 and (W_A + W_B) % WEIGHT_CHUNK_ROWS == 0
    hbm = pl.BlockSpec(memory_space=pl.ANY)
    args = (x, xs, sp_t, norm_g, w_in, w_pool, pool_scale, v_norm_g, w_s, b_s, w_out, fin_g)
    in_specs = []
    for a in args:
        if a is x:
            in_specs.append(pl.BlockSpec((1, tm, d), lambda b, t: (b, t, 0)))
        elif a is w_in or a is w_out or a is sp_t or a is xs:
            in_specs.append(hbm)
        else:
            in_specs.append(_const_spec(a.shape))
    return pl.pallas_call(
        _trunk_kernel,
        grid=(batch, seq // tm),
        in_specs=in_specs,
        out_specs=[
            pl.BlockSpec((1, tm, d), lambda b, t: (b, t, 0)),
            pl.BlockSpec((DEPTH, 1, POOL_BUF, W_A), lambda b, t: (0, b, 0, 0)),
            pl.BlockSpec((DEPTH, 1, CHUNK, W_B), lambda b, t: (0, b, 0, 0)),
            hbm,
            pl.BlockSpec((DEPTH, n_s, W_A), lambda b, t: (0, 0, 0)),
            hbm,
        ],
        out_shape=[
            jax.ShapeDtypeStruct((batch, seq, d), F32),
            jax.ShapeDtypeStruct((DEPTH, batch, POOL_BUF, W_A), F32),
            jax.ShapeDtypeStruct((DEPTH, batch, CHUNK, W_B), F32),
            jax.ShapeDtypeStruct((n_s, 1, d), F32),
            jax.ShapeDtypeStruct((DEPTH, n_s, W_A), F32),
            jax.ShapeDtypeStruct((DEPTH, n_s, 1, W_B), F32),
        ],
        scratch_shapes=[
            pltpu.VMEM((DEPTH, HIST + tm, W_A), F32),
            pltpu.VMEM((DEPTH, HIST + tm, W_A - POOL_GROUP_DIM), F32),
            pltpu.VMEM((DEPTH, HIST + tm, W_A - 2 * POOL_GROUP_DIM), F32),
            pltpu.VMEM((DEPTH, HIST + tm, W_A - 3 * POOL_GROUP_DIM), F32),
            pltpu.VMEM((tm, d), BF16),
            pltpu.VMEM((tm, d), F32),
            pltpu.VMEM((DEPTH, d, D_IN), BF16),
            pltpu.VMEM((DEPTH, W_A + W_B, d), BF16),
            pltpu.VMEM((DEPTH, CHUNK, W_B), F32),
            pltpu.VMEM((DEPTH, POOL_BUF, n_s, W_A), F32),
            pltpu.VMEM((n_s, d), F32),
            pltpu.VMEM((n_s, d), F32),
            pltpu.VMEM((DEPTH, n_s, W_B), F32),
            pltpu.VMEM((2, WEIGHT_CHUNK_ROWS, D_IN), F32),
            pltpu.SemaphoreType.DMA((2,)),
            pltpu.SemaphoreType.DMA((2,)),
        ],
        compiler_params=pltpu.CompilerParams(
            dimension_semantics=("arbitrary", "arbitrary"),
            vmem_limit_bytes=VMEM_LIMIT_BYTES,
        ),
        name="trunk",
    )(*args)


def kernel(x_prompt, x_sample, state_pool, norm_g, w_in, w_pool, pool_scale, v_norm_g, w_s, b_s, w_out, final_norm_g):
    assert x_sample.shape[1] == 1 and state_pool.shape[2] == POOL_BUF
    sp_t = jnp.transpose(state_pool, (0, 2, 1, 3))
    y_prompt, pool_prompt, chunk_v_prompt, y_sample, a_in_s, chunk_v_sample = _trunk_call(
        x_prompt, x_sample, sp_t, norm_g, w_in, w_pool, pool_scale, v_norm_g, w_s, b_s, w_out, final_norm_g)
    is_new_row = lax.broadcasted_iota(jnp.int32, (1, 1, POOL_BUF, 1), 2) == POOL_BUF - 1
    pool_sample = jnp.where(is_new_row, a_in_s[:, :, None, :], jnp.roll(state_pool, -1, axis=2))
    return (y_prompt, y_sample, pool_prompt, pool_sample, chunk_v_prompt, chunk_v_sample)
```

```python
import jax
import jax.numpy as jnp
from jax import lax
from jax.experimental import pallas as pl
from jax.experimental.pallas import tpu as pltpu

D_MODEL = 1024
DEPTH = 2
W_A = 512
W_B = 512
D_IN = 2 * W_A + 3 * W_B
POOL_WINDOWS = (2, 4, 8, 16)
N_POOL_GROUPS = len(POOL_WINDOWS)
POOL_GROUP_DIM = W_A // N_POOL_GROUPS
POOL_BUF = max(POOL_WINDOWS) - 1
CHUNK = 128
N_HEADS_B = 4
HEAD_DIM_B = W_B // N_HEADS_B
EPS = 1e-6
SAMPLE_PAST_LEN = 16384

COL_A_IN = 0
COL_A_GATE = W_A
COL_U = 2 * W_A
COL_V = 2 * W_A + W_B
COL_B_GATE = 2 * W_A + 2 * W_B

HIST = 16
ROW_TILE = 512
ROW_BLOCKS = 2
WEIGHT_CHUNK_ROWS = 256
VMEM_LIMIT_BYTES = 56 * 1024 * 1024

BF16 = jnp.bfloat16
F32 = jnp.float32


def _dot(a, b):
    return jnp.dot(a.astype(BF16), b.astype(BF16), preferred_element_type=F32)


def _rmsnorm(x, g):
    ms = jnp.mean(x * x, axis=-1, keepdims=True)
    return (x * lax.rsqrt(ms + EPS)) * g


def _silu(x):
    hx = 0.5 * x
    return hx * jnp.tanh(hx) + hx


def _head_rmsnorm(v, g):
    parts = []
    for h in range(N_HEADS_B):
        sl = slice(h * HEAD_DIM_B, (h + 1) * HEAD_DIM_B)
        parts.append(_rmsnorm(v[:, sl], g[:, sl]))
    return jnp.concatenate(parts, axis=-1)


def _masked_ws(ws_ref, l):
    row = lax.broadcasted_iota(jnp.int32, (CHUNK, CHUNK), 0)
    col = lax.broadcasted_iota(jnp.int32, (CHUNK, CHUNK), 1)
    keep = col <= row
    return [jnp.where(keep, ws_ref[l, h], 0.0).astype(BF16) for h in range(N_HEADS_B)]


class _Refs:
    def __init__(self, **kw):
        self.__dict__.update(kw)


def _row(ref, l):
    return ref[l:l + 1, :]


class _WeightStream:
    def __init__(self, w_in_hbm, w_out_hbm, w_in_bf, w_out_bf, stage, sems):
        self.chunks = []
        for l in range(DEPTH):
            for src, dst in ((w_in_hbm, w_in_bf), (w_out_hbm, w_out_bf)):
                _, k, n = src.shape
                self.chunks += [(src, dst, l, r, n) for r in range(0, k, WEIGHT_CHUNK_ROWS)]
        self.per_layer = len(self.chunks) // DEPTH
        self.stage, self.sems = stage, sems
        self.done = 0

    def _copy(self, i):
        src, _, l, r, n = self.chunks[i]
        slot = i % 2
        return pltpu.make_async_copy(
            src.at[l, pl.ds(r, WEIGHT_CHUNK_ROWS), :], self.stage.at[slot, :, pl.ds(0, n)], self.sems.at[slot])

    def start(self):
        self._copy(0).start()

    def service(self):
        i = self.done
        if i == len(self.chunks):
            return
        if i + 1 < len(self.chunks):
            self._copy(i + 1).start()
        self._copy(i).wait()
        _, dst, l, r, n = self.chunks[i]
        dst[l, r:r + WEIGHT_CHUNK_ROWS, :] = self.stage[i % 2, :, 0:n].astype(BF16)
        self.done += 1

    def finish_layer(self, l):
        while self.done < self.per_layer * (l + 1):
            self.service()


class _Chain:
    def __init__(self, l, r0, rows, t, tm, R):
        self.l, self.r0, self.rows, self.t, self.tm, self.R = l, r0, rows, t, tm, R

    def _in(self, col, width):
        R = self.R
        return jnp.dot(R.h_buf[self.r0:self.r0 + self.rows, :], R.w_in[self.l, :, col:col + width],
                       preferred_element_type=F32)

    def p0(self):
        self.v = self._in(COL_V, W_B)
        self.a_in = self._in(COL_A_IN, W_A)

    def p1(self):
        self.a_gate = self._in(COL_A_GATE, W_A)
        self.u = self._in(COL_U, W_B)

    def p2(self):
        l, r0, rows, R = self.l, self.r0, self.rows, self.R
        n_chunks = rows // CHUNK
        a_in = self.a_in
        base = HIST + r0

        vn = _head_rmsnorm(self.v, _row(R.v_norm_g, l))
        self.cv_rows = vn[rows - CHUNK:, :]
        vn_b = vn.astype(BF16)

        R.a_hist[l, base:base + rows, :] = a_in
        s2 = a_in + R.a_hist[l, base - 1:base - 1 + rows, :]
        R.s2_hist[l, base:base + rows, :] = s2[:, POOL_GROUP_DIM:]
        s4 = s2[:, POOL_GROUP_DIM:] + R.s2_hist[l, base - 2:base - 2 + rows, :]
        R.s4_hist[l, base:base + rows, :] = s4[:, POOL_GROUP_DIM:]
        s8 = s4[:, POOL_GROUP_DIM:] + R.s4_hist[l, base - 4:base - 4 + rows, :]
        R.s8_hist[l, base:base + rows, :] = s8[:, POOL_GROUP_DIM:]
        s16 = s8[:, POOL_GROUP_DIM:] + R.s8_hist[l, base - 8:base - 8 + rows, :]
        win_sums = (s2[:, :POOL_GROUP_DIM], s4[:, :POOL_GROUP_DIM], s8[:, :POOL_GROUP_DIM], s16)
        self.pool_rows = a_in[rows - POOL_BUF:, :]

        d_parts = []
        for g, w in enumerate(POOL_WINDOWS):
            sl = slice(g * POOL_GROUP_DIM, (g + 1) * POOL_GROUP_DIM)
            ssum = win_sums[g]
            if r0 == 0:
                pos1 = lax.broadcasted_iota(jnp.int32, (HIST, POOL_GROUP_DIM), 0) + self.t * self.tm + 1
                inv_head = 1.0 / jnp.minimum(pos1, w).astype(F32)
                mean = jnp.concatenate([ssum[:HIST] * inv_head, ssum[HIST:] * (1.0 / w)], axis=0)
            else:
                mean = ssum * (1.0 / w)
            d_parts.append((mean - a_in[:, sl]).astype(BF16))

        ws_b = _masked_ws(R.ws, l)
        s_heads = []
        for hh in range(N_HEADS_B):
            sl = slice(hh * HEAD_DIM_B, (hh + 1) * HEAD_DIM_B)
            rhs = jnp.concatenate([vn_b[c * CHUNK:(c + 1) * CHUNK, sl] for c in range(n_chunks)], axis=1)
            res = jnp.dot(ws_b[hh], rhs, preferred_element_type=F32)
            s_heads.append(jnp.concatenate(
                [res[:, c * HEAD_DIM_B:(c + 1) * HEAD_DIM_B] for c in range(n_chunks)], axis=0))
        self.s_heads = s_heads

        self.y_parts = [_dot(d_parts[g], R.w_pool[l, g]) for g in range(N_POOL_GROUPS)]
        self.b_gate = self._in(COL_B_GATE, W_B)

    def p3(self):
        l, rows, R = self.l, self.rows, self.R
        n_chunks = rows // CHUNK
        a_out = ((jnp.concatenate(self.y_parts, axis=-1) * _row(R.pool_scale, l)) * _silu(self.a_gate)).astype(BF16)
        out_a = _dot(a_out, R.w_out[l, 0:W_A, :])
        s = jnp.concatenate(self.s_heads, axis=-1) + jnp.concatenate([R.bias[l]] * n_chunks, axis=0)
        b_out = ((self.u * s) * _silu(self.b_gate)).astype(BF16)
        self.out = out_a + _dot(b_out, R.w_out[l, W_A:, :])


def _per_head_lanes(vals):
    return jnp.concatenate([jnp.broadcast_to(v, (1, HEAD_DIM_B)) for v in vals], axis=-1)


def _sample_group(xs_ref, sp_ref, norm_g_ref, b_s_ref, fin_g, R, ys_ref, a_in_s_ref, vn_s_ref):
    x = xs_ref[...]
    for l in range(DEPTH):
        h = _rmsnorm(x, _row(norm_g_ref, l)).astype(BF16)
        a_in = _dot(h, R.w_in[l, :, COL_A_IN:COL_A_IN + W_A])
        a_in_s_ref[l] = a_in
        a_gate = _dot(h, R.w_in[l, :, COL_A_GATE:COL_A_GATE + W_A])
        a_parts = []
        for g, w in enumerate(POOL_WINDOWS):
            sl = slice(g * POOL_GROUP_DIM, (g + 1) * POOL_GROUP_DIM)
            ssum = a_in[:, sl]
            for k in range(1, w):
                ssum = ssum + sp_ref[l, POOL_BUF - k, :, sl]
            cnt = float(min(SAMPLE_PAST_LEN + 1, w))
            d = ssum / cnt - a_in[:, sl]
            yg = _dot(d, R.w_pool[l, g]) * _row(R.pool_scale, l)[:, sl]
            a_parts.append(yg * _silu(a_gate[:, sl]))
        a_out = jnp.concatenate(a_parts, axis=-1)

        v = _dot(h, R.w_in[l, :, COL_V:COL_V + W_B])
        vn = _head_rmsnorm(v, _row(R.v_norm_g, l))
        vn_s_ref[l] = vn
        ws00 = _per_head_lanes([R.ws[l, hh, 0:1, 0:1] for hh in range(N_HEADS_B)])
        b0 = _per_head_lanes([b_s_ref[l, hh:hh + 1, 0:1] for hh in range(N_HEADS_B)])
        s = ws00 * vn + b0
        u = _dot(h, R.w_in[l, :, COL_U:COL_U + W_B])
        b_gate = _dot(h, R.w_in[l, :, COL_B_GATE:COL_B_GATE + W_B])
        b_out = (u * s) * _silu(b_gate)

        x = x + (_dot(a_out, R.w_out[l, 0:W_A, :]) + _dot(b_out, R.w_out[l, W_A:, :]))
    ys_ref[...] = _rmsnorm(x, fin_g)


def _trunk_kernel(x_ref, xs_hbm, sp_hbm, norm_g_ref, w_in_hbm, w_pool_ref, pool_scale_ref, v_norm_g_ref,
                  ws_ref, b_s_ref, w_out_hbm, fin_g_ref,
                  y_ref, pool_ref, cv_ref, ys_hbm, a_in_s_ref, vn_s_hbm,
                  a_hist, s2_hist, s4_hist, s8_hist, h_buf, x_buf, w_in_bf, w_out_bf, bias_buf, sp_buf,
                  xs_buf, ys_buf, vn_s_buf, stage, sems, io_sems):
    b = pl.program_id(0)
    t = pl.program_id(1)
    tm = x_ref.shape[1]
    rows = tm // ROW_BLOCKS
    R = _Refs(w_in=w_in_bf, w_pool=w_pool_ref, pool_scale=pool_scale_ref, v_norm_g=v_norm_g_ref, ws=ws_ref,
              bias=bias_buf, w_out=w_out_bf, a_hist=a_hist, s2_hist=s2_hist, s4_hist=s4_hist, s8_hist=s8_hist,
              h_buf=h_buf)
    hists = (a_hist, s2_hist, s4_hist, s8_hist)
    fin_g = fin_g_ref[...].reshape(1, D_MODEL)

    def tile_body(before_layer=None, between_phases=None):
        for l in range(DEPTH):
            if before_layer is not None:
                before_layer(l)

            def x_rows(rs, l=l):
                return x_ref[0, rs, :] if l == 0 else x_buf[rs, :]

            def phase_done():
                if between_phases is not None:
                    between_phases()

            blocks = [slice(k * rows, (k + 1) * rows) for k in range(ROW_BLOCKS)]
            for rs in blocks:
                h_buf[rs, :] = _rmsnorm(x_rows(rs), _row(norm_g_ref, l)).astype(BF16)
            chains = [_Chain(l, k * rows, rows, t, tm, R) for k in range(ROW_BLOCKS)]
            for c in chains:
                c.p0()
            phase_done()
            for c in chains:
                c.p1()
            phase_done()
            for c in chains:
                c.p2()
                phase_done()
            for buf in hists:
                buf[l, 0:HIST, :] = buf[l, tm:tm + HIST, :]
            pool_ref[l, 0] = chains[-1].pool_rows
            cv_ref[l, 0] = chains[-1].cv_rows
            for c, rs in zip(chains, blocks):
                c.p3()
                x_new = x_rows(rs) + c.out
                if l + 1 < DEPTH:
                    x_buf[rs, :] = x_new
                else:
                    y_ref[0, rs, :] = _rmsnorm(x_new, fin_g)
                phase_done()

    def zero_hist():
        for buf in hists:
            buf[:, 0:HIST, :] = jnp.zeros((DEPTH, HIST, buf.shape[2]), F32)

    first = jnp.logical_and(b == 0, t == 0)

    @pl.when(first)
    def _():
        stream = _WeightStream(w_in_hbm, w_out_hbm, w_in_bf, w_out_bf, stage, sems)
        stream.start()
        in_copies = [
            pltpu.make_async_copy(sp_hbm, sp_buf, io_sems.at[0]),
            pltpu.make_async_copy(xs_hbm.at[:, 0, :], xs_buf, io_sems.at[1]),
        ]
        for l in range(DEPTH):
            for hh in range(N_HEADS_B):
                bias_buf[l, :, hh * HEAD_DIM_B:(hh + 1) * HEAD_DIM_B] = jnp.broadcast_to(
                    b_s_ref[l, hh:hh + 1, :], (CHUNK, CHUNK)).T
        zero_hist()

        def before_layer(l):
            stream.finish_layer(l)
            if l == DEPTH - 1:
                for cp in in_copies:
                    cp.start()

        tile_body(before_layer, stream.service)
        for cp in in_copies:
            cp.wait()
        _sample_group(xs_buf, sp_buf, norm_g_ref, b_s_ref, fin_g, R, ys_buf, a_in_s_ref, vn_s_buf)
        out_copies = [
            pltpu.make_async_copy(ys_buf, ys_hbm.at[:, 0, :], io_sems.at[0]),
            pltpu.make_async_copy(vn_s_buf, vn_s_hbm.at[:, :, 0, :], io_sems.at[1]),
        ]
        for cp in out_copies:
            cp.start()
        for cp in out_copies:
            cp.wait()

    @pl.when(jnp.logical_not(first))
    def _():
        pl.when(t == 0)(zero_hist)
        tile_body()


def _const_spec(shape):
    zeros = (0,) * len(shape)
    return pl.BlockSpec(shape, lambda b, t: zeros, pipeline_mode=pl.Buffered(1))


def _trunk_call(x, xs, sp_t, norm_g, w_in, w_pool, pool_scale, v_norm_g, w_s, b_s, w_out, fin_g):
    batch, seq, d = x.shape
    n_s = xs.shape[0]
    tm = ROW_TILE
    assert seq % tm == 0 and tm % (ROW_BLOCKS * CHUNK) == 0
    assert d % WEIGHT_CHUNK_ROWS == 0 and (W_A + W_B) % WEIGHT_CHUNK_ROWS == 0
    hbm = pl.BlockSpec(memory_space=pl.ANY)
    args = (x, xs, sp_t, norm_g, w_in, w_pool, pool_scale, v_norm_g, w_s, b_s, w_out, fin_g)
    in_specs = []
    for a in args:
        if a is x:
            in_specs.append(pl.BlockSpec((1, tm, d), lambda b, t: (b, t, 0)))
        elif a is w_in or a is w_out or a is sp_t or a is xs:
            in_specs.append(hbm)
        else:
            in_specs.append(_const_spec(a.shape))
    return pl.pallas_call(
        _trunk_kernel,
        grid=(batch, seq // tm),
        in_specs=in_specs,
        out_specs=[
            pl.BlockSpec((1, tm, d), lambda b, t: (b, t, 0)),
            pl.BlockSpec((DEPTH, 1, POOL_BUF, W_A), lambda b, t: (0, b, 0, 0)),
            pl.BlockSpec((DEPTH, 1, CHUNK, W_B), lambda b, t: (0, b, 0, 0)),
            hbm,
            pl.BlockSpec((DEPTH, n_s, W_A), lambda b, t: (0, 0, 0)),
            hbm,
        ],
        out_shape=[
            jax.ShapeDtypeStruct((batch, seq, d), F32),
            jax.ShapeDtypeStruct((DEPTH, batch, POOL_BUF, W_A), F32),
            jax.ShapeDtypeStruct((DEPTH, batch, CHUNK, W_B), F32),
            jax.ShapeDtypeStruct((n_s, 1, d), F32),
            jax.ShapeDtypeStruct((DEPTH, n_s, W_A), F32),
            jax.ShapeDtypeStruct((DEPTH, n_s, 1, W_B), F32),
        ],
        scratch_shapes=[
            pltpu.VMEM((DEPTH, HIST + tm, W_A), F32),
            pltpu.VMEM((DEPTH, HIST + tm, W_A - POOL_GROUP_DIM), F32),
            pltpu.VMEM((DEPTH, HIST + tm, W_A - 2 * POOL_GROUP_DIM), F32),
            pltpu.VMEM((DEPTH, HIST + tm, W_A - 3 * POOL_GROUP_DIM), F32),
            pltpu.VMEM((tm, d), BF16),
            pltpu.VMEM((tm, d), F32),
            pltpu.VMEM((DEPTH, d, D_IN), BF16),
            pltpu.VMEM((DEPTH, W_A + W_B, d), BF16),
            pltpu.VMEM((DEPTH, CHUNK, W_B), F32),
            pltpu.VMEM((DEPTH, POOL_BUF, n_s, W_A), F32),
            pltpu.VMEM((n_s, d), F32),
            pltpu.VMEM((n_s, d), F32),
            pltpu.VMEM((DEPTH, n_s, W_B), F32),
            pltpu.VMEM((2, WEIGHT_CHUNK_ROWS, D_IN), F32),
            pltpu.SemaphoreType.DMA((2,)),
            pltpu.SemaphoreType.DMA((2,)),
        ],
        compiler_params=pltpu.CompilerParams(
            dimension_semantics=("arbitrary", "arbitrary"),
            vmem_limit_bytes=VMEM_LIMIT_BYTES,
        ),
        name="trunk",
    )(*args)


def kernel(x_prompt, x_sample, state_pool, norm_g, w_in, w_pool, pool_scale, v_norm_g, w_s, b_s, w_out, final_norm_g):
    assert x_sample.shape[1] == 1 and state_pool.shape[2] == POOL_BUF
    sp_t = jnp.transpose(state_pool, (0, 2, 1, 3))
    y_prompt, pool_prompt, chunk_v_prompt, y_sample, a_in_s, chunk_v_sample = _trunk_call(
        x_prompt, x_sample, sp_t, norm_g, w_in, w_pool, pool_scale, v_norm_g, w_s, b_s, w_out, final_norm_g)
    is_new_row = lax.broadcasted_iota(jnp.int32, (1, 1, POOL_BUF, 1), 2) == POOL_BUF - 1
    pool_sample = jnp.where(is_new_row, a_in_s[:, :, None, :], jnp.roll(state_pool, -1, axis=2))
    return (y_prompt, y_sample, pool_prompt, pool_sample, chunk_v_prompt, chunk_v_sample)
```

```python
import jax
import jax.numpy as jnp
from jax import lax
from jax.experimental import pallas as pl
from jax.experimental.pallas import tpu as pltpu

D_MODEL = 1024
DEPTH = 2
W_A = 512
W_B = 512
D_IN = 2 * W_A + 3 * W_B
POOL_WINDOWS = (2, 4, 8, 16)
N_POOL_GROUPS = len(POOL_WINDOWS)
POOL_GROUP_DIM = W_A // N_POOL_GROUPS
POOL_BUF = max(POOL_WINDOWS) - 1
CHUNK = 128
N_HEADS_B = 4
HEAD_DIM_B = W_B // N_HEADS_B
EPS = 1e-6
SAMPLE_PAST_LEN = 16384

COL_A_IN = 0
COL_A_GATE = W_A
COL_U = 2 * W_A
COL_V = 2 * W_A + W_B
COL_B_GATE = 2 * W_A + 2 * W_B

HIST = 16
ROW_TILE = 512
ROW_BLOCKS = 2
WEIGHT_CHUNK_ROWS = 256
WEIGHT_SLOTS = 3
VMEM_LIMIT_BYTES = 56 * 1024 * 1024

BF16 = jnp.bfloat16
F32 = jnp.float32


def _dot(a, b):
    return jnp.dot(a.astype(BF16), b.astype(BF16), preferred_element_type=F32)


def _rmsnorm(x, g):
    ms = jnp.mean(x * x, axis=-1, keepdims=True)
    return (x * lax.rsqrt(ms + EPS)) * g


def _silu(x):
    hx = 0.5 * x
    return hx * jnp.tanh(hx) + hx


def _head_rmsnorm(v, g):
    parts = []
    for h in range(N_HEADS_B):
        sl = slice(h * HEAD_DIM_B, (h + 1) * HEAD_DIM_B)
        parts.append(_rmsnorm(v[:, sl], g[:, sl]))
    return jnp.concatenate(parts, axis=-1)


def _masked_ws(ws_ref, l):
    row = lax.broadcasted_iota(jnp.int32, (CHUNK, CHUNK), 0)
    col = lax.broadcasted_iota(jnp.int32, (CHUNK, CHUNK), 1)
    keep = col <= row
    return [jnp.where(keep, ws_ref[l, h], 0.0).astype(BF16) for h in range(N_HEADS_B)]


class _Refs:
    def __init__(self, **kw):
        self.__dict__.update(kw)


def _row(ref, l):
    return ref[l:l + 1, :]


class _WeightStream:
    def __init__(self, w_in_hbm, w_out_hbm, w_in_bf, w_out_bf, stage, sems):
        self.chunks = []
        for l in range(DEPTH):
            for src, dst in ((w_in_hbm, w_in_bf), (w_out_hbm, w_out_bf)):
                _, k, n = src.shape
                self.chunks += [(src, dst, l, r, n) for r in range(0, k, WEIGHT_CHUNK_ROWS)]
        self.per_layer = len(self.chunks) // DEPTH
        self.stage, self.sems = stage, sems
        self.done = 0

    def _copy(self, i):
        src, _, l, r, n = self.chunks[i]
        slot = i % WEIGHT_SLOTS
        return pltpu.make_async_copy(
            src.at[l, pl.ds(r, WEIGHT_CHUNK_ROWS), :], self.stage.at[slot, :, pl.ds(0, n)], self.sems.at[slot])

    def start(self):
        for i in range(WEIGHT_SLOTS - 1):
            self._copy(i).start()

    def service(self):
        i = self.done
        if i == len(self.chunks):
            return
        if i + WEIGHT_SLOTS - 1 < len(self.chunks):
            self._copy(i + WEIGHT_SLOTS - 1).start()
        self._copy(i).wait()
        _, dst, l, r, n = self.chunks[i]
        dst[l, r:r + WEIGHT_CHUNK_ROWS, :] = self.stage[i % WEIGHT_SLOTS, :, 0:n].astype(BF16)
        self.done += 1

    def finish_layer(self, l):
        while self.done < self.per_layer * (l + 1):
            self.service()


class _Chain:
    def __init__(self, l, r0, rows, t, tm, R):
        self.l, self.r0, self.rows, self.t, self.tm, self.R = l, r0, rows, t, tm, R

    def _in(self, col, width):
        R = self.R
        return jnp.dot(R.h_buf[self.r0:self.r0 + self.rows, :], R.w_in[self.l, :, col:col + width],
                       preferred_element_type=F32)

    def p0(self):
        self.v = self._in(COL_V, W_B)
        self.a_in = self._in(COL_A_IN, W_A)

    def p1(self):
        self.a_gate = self._in(COL_A_GATE, W_A)
        self.u = self._in(COL_U, W_B)

    def p2(self):
        l, r0, rows, R = self.l, self.r0, self.rows, self.R
        n_chunks = rows // CHUNK
        a_in = self.a_in
        base = HIST + r0

        vn = _head_rmsnorm(self.v, _row(R.v_norm_g, l))
        self.cv_rows = vn[rows - CHUNK:, :]
        vn_b = vn.astype(BF16)

        R.a_hist[l, base:base + rows, :] = a_in
        s2 = a_in + R.a_hist[l, base - 1:base - 1 + rows, :]
        R.s2_hist[l, base:base + rows, :] = s2[:, POOL_GROUP_DIM:]
        s4 = s2[:, POOL_GROUP_DIM:] + R.s2_hist[l, base - 2:base - 2 + rows, :]
        R.s4_hist[l, base:base + rows, :] = s4[:, POOL_GROUP_DIM:]
        s8 = s4[:, POOL_GROUP_DIM:] + R.s4_hist[l, base - 4:base - 4 + rows, :]
        R.s8_hist[l, base:base + rows, :] = s8[:, POOL_GROUP_DIM:]
        s16 = s8[:, POOL_GROUP_DIM:] + R.s8_hist[l, base - 8:base - 8 + rows, :]
        win_sums = (s2[:, :POOL_GROUP_DIM], s4[:, :POOL_GROUP_DIM], s8[:, :POOL_GROUP_DIM], s16)
        self.pool_rows = a_in[rows - POOL_BUF:, :]

        d_parts = []
        for g, w in enumerate(POOL_WINDOWS):
            sl = slice(g * POOL_GROUP_DIM, (g + 1) * POOL_GROUP_DIM)
            ssum = win_sums[g]
            if r0 == 0:
                pos1 = lax.broadcasted_iota(jnp.int32, (HIST, POOL_GROUP_DIM), 0) + self.t * self.tm + 1
                inv_head = 1.0 / jnp.minimum(pos1, w).astype(F32)
                mean = jnp.concatenate([ssum[:HIST] * inv_head, ssum[HIST:] * (1.0 / w)], axis=0)
            else:
                mean = ssum * (1.0 / w)
            d_parts.append((mean - a_in[:, sl]).astype(BF16))

        ws_b = _masked_ws(R.ws, l)
        s_heads = []
        for hh in range(N_HEADS_B):
            sl = slice(hh * HEAD_DIM_B, (hh + 1) * HEAD_DIM_B)
            rhs = jnp.concatenate([vn_b[c * CHUNK:(c + 1) * CHUNK, sl] for c in range(n_chunks)], axis=1)
            res = jnp.dot(ws_b[hh], rhs, preferred_element_type=F32)
            s_heads.append(jnp.concatenate(
                [res[:, c * HEAD_DIM_B:(c + 1) * HEAD_DIM_B] for c in range(n_chunks)], axis=0))
        self.s_heads = s_heads

        self.y_parts = [_dot(d_parts[g], R.w_pool[l, g]) for g in range(N_POOL_GROUPS)]
        self.b_gate = self._in(COL_B_GATE, W_B)

    def p3(self):
        l, rows, R = self.l, self.rows, self.R
        n_chunks = rows // CHUNK
        a_out = ((jnp.concatenate(self.y_parts, axis=-1) * _row(R.pool_scale, l)) * _silu(self.a_gate)).astype(BF16)
        out_a = _dot(a_out, R.w_out[l, 0:W_A, :])
        s = jnp.concatenate(self.s_heads, axis=-1) + jnp.concatenate([R.bias[l]] * n_chunks, axis=0)
        b_out = ((self.u * s) * _silu(self.b_gate)).astype(BF16)
        self.out = out_a + _dot(b_out, R.w_out[l, W_A:, :])


def _per_head_lanes(vals):
    return jnp.concatenate([jnp.broadcast_to(v, (1, HEAD_DIM_B)) for v in vals], axis=-1)


def _sample_group(xs_ref, sp_ref, norm_g_ref, b_s_ref, fin_g, R, ys_ref, a_in_s_ref, vn_s_ref):
    x = xs_ref[...]
    for l in range(DEPTH):
        h = _rmsnorm(x, _row(norm_g_ref, l)).astype(BF16)
        a_in = _dot(h, R.w_in[l, :, COL_A_IN:COL_A_IN + W_A])
        a_in_s_ref[l] = a_in
        a_gate = _dot(h, R.w_in[l, :, COL_A_GATE:COL_A_GATE + W_A])
        a_parts = []
        for g, w in enumerate(POOL_WINDOWS):
            sl = slice(g * POOL_GROUP_DIM, (g + 1) * POOL_GROUP_DIM)
            ssum = a_in[:, sl]
            for k in range(1, w):
                ssum = ssum + sp_ref[l, POOL_BUF - k, :, sl]
            cnt = float(min(SAMPLE_PAST_LEN + 1, w))
            d = ssum / cnt - a_in[:, sl]
            yg = _dot(d, R.w_pool[l, g]) * _row(R.pool_scale, l)[:, sl]
            a_parts.append(yg * _silu(a_gate[:, sl]))
        a_out = jnp.concatenate(a_parts, axis=-1)

        v = _dot(h, R.w_in[l, :, COL_V:COL_V + W_B])
        vn = _head_rmsnorm(v, _row(R.v_norm_g, l))
        vn_s_ref[l] = vn
        ws00 = _per_head_lanes([R.ws[l, hh, 0:1, 0:1] for hh in range(N_HEADS_B)])
        b0 = _per_head_lanes([b_s_ref[l, hh:hh + 1, 0:1] for hh in range(N_HEADS_B)])
        s = ws00 * vn + b0
        u = _dot(h, R.w_in[l, :, COL_U:COL_U + W_B])
        b_gate = _dot(h, R.w_in[l, :, COL_B_GATE:COL_B_GATE + W_B])
        b_out = (u * s) * _silu(b_gate)

        x = x + (_dot(a_out, R.w_out[l, 0:W_A, :]) + _dot(b_out, R.w_out[l, W_A:, :]))
    ys_ref[...] = _rmsnorm(x, fin_g)


def _trunk_kernel(x_ref, xs_hbm, sp_hbm, norm_g_ref, w_in_hbm, w_pool_ref, pool_scale_ref, v_norm_g_ref,
                  ws_ref, b_s_ref, w_out_hbm, fin_g_ref,
                  y_ref, pool_ref, cv_ref, ys_hbm, a_in_s_ref, vn_s_hbm,
                  a_hist, s2_hist, s4_hist, s8_hist, h_buf, x_buf, w_in_bf, w_out_bf, bias_buf, sp_buf,
                  xs_buf, ys_buf, vn_s_buf, stage, sems, io_sems):
    b = pl.program_id(0)
    t = pl.program_id(1)
    tm = x_ref.shape[1]
    rows = tm // ROW_BLOCKS
    R = _Refs(w_in=w_in_bf, w_pool=w_pool_ref, pool_scale=pool_scale_ref, v_norm_g=v_norm_g_ref, ws=ws_ref,
              bias=bias_buf, w_out=w_out_bf, a_hist=a_hist, s2_hist=s2_hist, s4_hist=s4_hist, s8_hist=s8_hist,
              h_buf=h_buf)
    hists = (a_hist, s2_hist, s4_hist, s8_hist)
    fin_g = fin_g_ref[...].reshape(1, D_MODEL)

    def tile_body(before_layer=None, between_phases=None):
        for l in range(DEPTH):
            if before_layer is not None:
                before_layer(l)

            def x_rows(rs, l=l):
                return x_ref[0, rs, :] if l == 0 else x_buf[rs, :]

            def phase_done():
                if between_phases is not None:
                    between_phases()

            blocks = [slice(k * rows, (k + 1) * rows) for k in range(ROW_BLOCKS)]
            for rs in blocks:
                h_buf[rs, :] = _rmsnorm(x_rows(rs), _row(norm_g_ref, l)).astype(BF16)
            chains = [_Chain(l, k * rows, rows, t, tm, R) for k in range(ROW_BLOCKS)]
            for c in chains:
                c.p0()
            phase_done()
            for c in chains:
                c.p1()
            phase_done()
            for c in chains:
                c.p2()
                phase_done()
            for buf in hists:
                buf[l, 0:HIST, :] = buf[l, tm:tm + HIST, :]
            pool_ref[l, 0] = chains[-1].pool_rows
            cv_ref[l, 0] = chains[-1].cv_rows
            for c, rs in zip(chains, blocks):
                c.p3()
                x_new = x_rows(rs) + c.out
                if l + 1 < DEPTH:
                    x_buf[rs, :] = x_new
                else:
                    y_ref[0, rs, :] = _rmsnorm(x_new, fin_g)
                phase_done()

    def zero_hist():
        for buf in hists:
            buf[:, 0:HIST, :] = jnp.zeros((DEPTH, HIST, buf.shape[2]), F32)

    first = jnp.logical_and(b == 0, t == 0)

    @pl.when(first)
    def _():
        stream = _WeightStream(w_in_hbm, w_out_hbm, w_in_bf, w_out_bf, stage, sems)
        stream.start()
        in_copies = [
            pltpu.make_async_copy(sp_hbm, sp_buf, io_sems.at[0]),
            pltpu.make_async_copy(xs_hbm.at[:, 0, :], xs_buf, io_sems.at[1]),
        ]
        for l in range(DEPTH):
            for hh in range(N_HEADS_B):
                bias_buf[l, :, hh * HEAD_DIM_B:(hh + 1) * HEAD_DIM_B] = jnp.broadcast_to(
                    b_s_ref[l, hh:hh + 1, :], (CHUNK, CHUNK)).T
        zero_hist()

        def before_layer(l):
            stream.finish_layer(l)
            if l == DEPTH - 1:
                for cp in in_copies:
                    cp.start()

        tile_body(before_layer, stream.service)
        for cp in in_copies:
            cp.wait()
        _sample_group(xs_buf, sp_buf, norm_g_ref, b_s_ref, fin_g, R, ys_buf, a_in_s_ref, vn_s_buf)
        out_copies = [
            pltpu.make_async_copy(ys_buf, ys_hbm.at[:, 0, :], io_sems.at[0]),
            pltpu.make_async_copy(vn_s_buf, vn_s_hbm.at[:, :, 0, :], io_sems.at[1]),
        ]
        for cp in out_copies:
            cp.start()
        for cp in out_copies:
            cp.wait()

    @pl.when(jnp.logical_not(first))
    def _():
        pl.when(t == 0)(zero_hist)
        tile_body()


def _const_spec(shape):
    zeros = (0,) * len(shape)
    return pl.BlockSpec(shape, lambda b, t: zeros, pipeline_mode=pl.Buffered(1))


def _trunk_call(x, xs, sp_t, norm_g, w_in, w_pool, pool_scale, v_norm_g, w_s, b_s, w_out, fin_g):
    batch, seq, d = x.shape
    n_s = xs.shape[0]
    tm = ROW_TILE
    assert seq % tm == 0 and tm % (ROW_BLOCKS * CHUNK) == 0
    assert d % WEIGHT_CHUNK_ROWS == 0 and (W_A + W_B) % WEIGHT_CHUNK_ROWS == 0
    hbm = pl.BlockSpec(memory_space=pl.ANY)
    args = (x, xs, sp_t, norm_g, w_in, w_pool, pool_scale, v_norm_g, w_s, b_s, w_out, fin_g)
    in_specs = []
    for a in args:
        if a is x:
            in_specs.append(pl.BlockSpec((1, tm, d), lambda b, t: (b, t, 0)))
        elif a is w_in or a is w_out or a is sp_t or a is xs:
            in_specs.append(hbm)
        else:
            in_specs.append(_const_spec(a.shape))
    return pl.pallas_call(
        _trunk_kernel,
        grid=(batch, seq // tm),
        in_specs=in_specs,
        out_specs=[
            pl.BlockSpec((1, tm, d), lambda b, t: (b, t, 0)),
            pl.BlockSpec((DEPTH, 1, POOL_BUF, W_A), lambda b, t: (0, b, 0, 0)),
            pl.BlockSpec((DEPTH, 1, CHUNK, W_B), lambda b, t: (0, b, 0, 0)),
            hbm,
            pl.BlockSpec((DEPTH, n_s, W_A), lambda b, t: (0, 0, 0)),
            hbm,
        ],
        out_shape=[
            jax.ShapeDtypeStruct((batch, seq, d), F32),
            jax.ShapeDtypeStruct((DEPTH, batch, POOL_BUF, W_A), F32),
            jax.ShapeDtypeStruct((DEPTH, batch, CHUNK, W_B), F32),
            jax.ShapeDtypeStruct((n_s, 1, d), F32),
            jax.ShapeDtypeStruct((DEPTH, n_s, W_A), F32),
            jax.ShapeDtypeStruct((DEPTH, n_s, 1, W_B), F32),
        ],
        scratch_shapes=[
            pltpu.VMEM((DEPTH, HIST + tm, W_A), F32),
            pltpu.VMEM((DEPTH, HIST + tm, W_A - POOL_GROUP_DIM), F32),
            pltpu.VMEM((DEPTH, HIST + tm, W_A - 2 * POOL_GROUP_DIM), F32),
            pltpu.VMEM((DEPTH, HIST + tm, W_A - 3 * POOL_GROUP_DIM), F32),
            pltpu.VMEM((tm, d), BF16),
            pltpu.VMEM((tm, d), F32),
            pltpu.VMEM((DEPTH, d, D_IN), BF16),
            pltpu.VMEM((DEPTH, W_A + W_B, d), BF16),
            pltpu.VMEM((DEPTH, CHUNK, W_B), F32),
            pltpu.VMEM((DEPTH, POOL_BUF, n_s, W_A), F32),
            pltpu.VMEM((n_s, d), F32),
            pltpu.VMEM((n_s, d), F32),
            pltpu.VMEM((DEPTH, n_s, W_B), F32),
            pltpu.VMEM((WEIGHT_SLOTS, WEIGHT_CHUNK_ROWS, D_IN), F32),
            pltpu.SemaphoreType.DMA((WEIGHT_SLOTS,)),
            pltpu.SemaphoreType.DMA((2,)),
        ],
        compiler_params=pltpu.CompilerParams(
            dimension_semantics=("arbitrary", "arbitrary"),
            vmem_limit_bytes=VMEM_LIMIT_BYTES,
        ),
        name="trunk",
    )(*args)


def kernel(x_prompt, x_sample, state_pool, norm_g, w_in, w_pool, pool_scale, v_norm_g, w_s, b_s, w_out, final_norm_g):
    assert x_sample.shape[1] == 1 and state_pool.shape[2] == POOL_BUF
    sp_t = jnp.transpose(state_pool, (0, 2, 1, 3))
    y_prompt, pool_prompt, chunk_v_prompt, y_sample, a_in_s, chunk_v_sample = _trunk_call(
        x_prompt, x_sample, sp_t, norm_g, w_in, w_pool, pool_scale, v_norm_g, w_s, b_s, w_out, final_norm_g)
    is_new_row = lax.broadcasted_iota(jnp.int32, (1, 1, POOL_BUF, 1), 2) == POOL_BUF - 1
    pool_sample = jnp.where(is_new_row, a_in_s[:, :, None, :], jnp.roll(state_pool, -1, axis=2))
    return (y_prompt, y_sample, pool_prompt, pool_sample, chunk_v_prompt, chunk_v_sample)
```

```python
import jax
import jax.numpy as jnp
from jax import lax
from jax.experimental import pallas as pl
from jax.experimental.pallas import tpu as pltpu

D_MODEL = 1024
DEPTH = 2
W_A = 512
W_B = 512
D_IN = 2 * W_A + 3 * W_B
POOL_WINDOWS = (2, 4, 8, 16)
N_POOL_GROUPS = len(POOL_WINDOWS)
POOL_GROUP_DIM = W_A // N_POOL_GROUPS
POOL_BUF = max(POOL_WINDOWS) - 1
CHUNK = 128
N_HEADS_B = 4
HEAD_DIM_B = W_B // N_HEADS_B
EPS = 1e-6
SAMPLE_PAST_LEN = 16384

COL_A_IN = 0
COL_A_GATE = W_A
COL_U = 2 * W_A
COL_V = 2 * W_A + W_B
COL_B_GATE = 2 * W_A + 2 * W_B

HIST = 16
ROW_TILE = 512
ROW_BLOCKS = 2
WEIGHT_CHUNK_ROWS = 256
WEIGHT_SLOTS = 3
VMEM_LIMIT_BYTES = 60 * 1024 * 1024

BF16 = jnp.bfloat16
F32 = jnp.float32


def _dot(a, b):
    return jnp.dot(a.astype(BF16), b.astype(BF16), preferred_element_type=F32)


def _rmsnorm(x, g):
    ms = jnp.mean(x * x, axis=-1, keepdims=True)
    return (x * lax.rsqrt(ms + EPS)) * g


def _silu(x):
    hx = 0.5 * x
    return hx * jnp.tanh(hx) + hx


def _head_rmsnorm(v, g):
    parts = []
    for h in range(N_HEADS_B):
        sl = slice(h * HEAD_DIM_B, (h + 1) * HEAD_DIM_B)
        parts.append(_rmsnorm(v[:, sl], g[:, sl]))
    return jnp.concatenate(parts, axis=-1)


def _masked_ws(ws_ref, l):
    row = lax.broadcasted_iota(jnp.int32, (CHUNK, CHUNK), 0)
    col = lax.broadcasted_iota(jnp.int32, (CHUNK, CHUNK), 1)
    keep = col <= row
    return [jnp.where(keep, ws_ref[l, h], 0.0).astype(BF16) for h in range(N_HEADS_B)]


class _Refs:
    def __init__(self, **kw):
        self.__dict__.update(kw)


def _row(ref, l):
    return ref[l:l + 1, :]


class _WeightStream:
    def __init__(self, w_in_hbm, w_out_hbm, w_in_bf, w_out_bf, stage, sems):
        self.chunks = []
        for l in range(DEPTH):
            for src, dst in ((w_in_hbm, w_in_bf), (w_out_hbm, w_out_bf)):
                _, k, n = src.shape
                self.chunks += [(src, dst, l, r, n) for r in range(0, k, WEIGHT_CHUNK_ROWS)]
        self.per_layer = len(self.chunks) // DEPTH
        self.stage, self.sems = stage, sems
        self.done = 0

    def _copy(self, i):
        src, _, l, r, n = self.chunks[i]
        slot = i % WEIGHT_SLOTS
        return pltpu.make_async_copy(
            src.at[l, pl.ds(r, WEIGHT_CHUNK_ROWS), :], self.stage.at[slot, :, pl.ds(0, n)], self.sems.at[slot])

    def start(self):
        for i in range(WEIGHT_SLOTS - 1):
            self._copy(i).start()

    def service(self):
        i = self.done
        if i == len(self.chunks):
            return
        if i + WEIGHT_SLOTS - 1 < len(self.chunks):
            self._copy(i + WEIGHT_SLOTS - 1).start()
        self._copy(i).wait()
        _, dst, l, r, n = self.chunks[i]
        dst[l, r:r + WEIGHT_CHUNK_ROWS, :] = self.stage[i % WEIGHT_SLOTS, :, 0:n].astype(BF16)
        self.done += 1

    def finish_layer(self, l):
        while self.done < self.per_layer * (l + 1):
            self.service()


class _Chain:
    def __init__(self, l, r0, rows, t, tm, R, lead=False, gates_done=False):
        self.l, self.r0, self.rows, self.t, self.tm, self.R = l, r0, rows, t, tm, R
        self.lead, self.gates_done = lead, gates_done

    def _in(self, col, width):
        R = self.R
        h = R.hn_buf[...] if self.lead else R.h_buf[self.r0:self.r0 + self.rows, :]
        return jnp.dot(h, R.w_in[self.l, :, col:col + width], preferred_element_type=F32)

    def p0(self):
        self.v = self._in(COL_V, W_B)
        self.a_in = self._in(COL_A_IN, W_A)

    def p1(self):
        self.a_gate = self.R.gate_buf[0] if self.gates_done else self._in(COL_A_GATE, W_A)
        self.u = self._in(COL_U, W_B)

    def p2(self):
        l, r0, rows, R = self.l, self.r0, self.rows, self.R
        n_chunks = rows // CHUNK
        a_in = self.a_in
        base = HIST + r0

        vn = _head_rmsnorm(self.v, _row(R.v_norm_g, l))
        self.cv_rows = vn[rows - CHUNK:, :]
        vn_b = vn.astype(BF16)

        R.a_hist[l, base:base + rows, :] = a_in
        s2 = a_in + R.a_hist[l, base - 1:base - 1 + rows, :]
        R.s2_hist[l, base:base + rows, :] = s2[:, POOL_GROUP_DIM:]
        s4 = s2[:, POOL_GROUP_DIM:] + R.s2_hist[l, base - 2:base - 2 + rows, :]
        R.s4_hist[l, base:base + rows, :] = s4[:, POOL_GROUP_DIM:]
        s8 = s4[:, POOL_GROUP_DIM:] + R.s4_hist[l, base - 4:base - 4 + rows, :]
        R.s8_hist[l, base:base + rows, :] = s8[:, POOL_GROUP_DIM:]
        s16 = s8[:, POOL_GROUP_DIM:] + R.s8_hist[l, base - 8:base - 8 + rows, :]
        win_sums = (s2[:, :POOL_GROUP_DIM], s4[:, :POOL_GROUP_DIM], s8[:, :POOL_GROUP_DIM], s16)
        self.pool_rows = a_in[rows - POOL_BUF:, :]

        d_parts = []
        for g, w in enumerate(POOL_WINDOWS):
            sl = slice(g * POOL_GROUP_DIM, (g + 1) * POOL_GROUP_DIM)
            ssum = win_sums[g]
            if r0 == 0:
                pos1 = lax.broadcasted_iota(jnp.int32, (HIST, POOL_GROUP_DIM), 0) + self.t * self.tm + 1
                inv_head = 1.0 / jnp.minimum(pos1, w).astype(F32)
                mean = jnp.concatenate([ssum[:HIST] * inv_head, ssum[HIST:] * (1.0 / w)], axis=0)
            else:
                mean = ssum * (1.0 / w)
            d_parts.append((mean - a_in[:, sl]).astype(BF16))

        ws_b = _masked_ws(R.ws, l)
        s_heads = []
        for hh in range(N_HEADS_B):
            sl = slice(hh * HEAD_DIM_B, (hh + 1) * HEAD_DIM_B)
            rhs = jnp.concatenate([vn_b[c * CHUNK:(c + 1) * CHUNK, sl] for c in range(n_chunks)], axis=1)
            res = jnp.dot(ws_b[hh], rhs, preferred_element_type=F32)
            s_heads.append(jnp.concatenate(
                [res[:, c * HEAD_DIM_B:(c + 1) * HEAD_DIM_B] for c in range(n_chunks)], axis=0))
        self.s_heads = s_heads

        self.y_parts = [_dot(d_parts[g], R.w_pool[l, g]) for g in range(N_POOL_GROUPS)]
        self.b_gate = self.R.gate_buf[1] if self.gates_done else self._in(COL_B_GATE, W_B)

    def p3(self):
        l, rows, R = self.l, self.rows, self.R
        n_chunks = rows // CHUNK
        a_out = ((jnp.concatenate(self.y_parts, axis=-1) * _row(R.pool_scale, l)) * _silu(self.a_gate)).astype(BF16)
        out_a = _dot(a_out, R.w_out[l, 0:W_A, :])
        s = jnp.concatenate(self.s_heads, axis=-1) + jnp.concatenate([R.bias[l]] * n_chunks, axis=0)
        b_out = ((self.u * s) * _silu(self.b_gate)).astype(BF16)
        self.out = out_a + _dot(b_out, R.w_out[l, W_A:, :])


def _per_head_lanes(vals):
    return jnp.concatenate([jnp.broadcast_to(v, (1, HEAD_DIM_B)) for v in vals], axis=-1)


def _sample_group(xs_ref, sp_ref, norm_g_ref, b_s_ref, fin_g, R, ys_ref, a_in_s_ref, vn_s_ref):
    x = xs_ref[...]
    for l in range(DEPTH):
        h = _rmsnorm(x, _row(norm_g_ref, l)).astype(BF16)
        a_in = _dot(h, R.w_in[l, :, COL_A_IN:COL_A_IN + W_A])
        a_in_s_ref[l] = a_in
        a_gate = _dot(h, R.w_in[l, :, COL_A_GATE:COL_A_GATE + W_A])
        a_parts = []
        for g, w in enumerate(POOL_WINDOWS):
            sl = slice(g * POOL_GROUP_DIM, (g + 1) * POOL_GROUP_DIM)
            ssum = a_in[:, sl]
            for k in range(1, w):
                ssum = ssum + sp_ref[l, POOL_BUF - k, :, sl]
            cnt = float(min(SAMPLE_PAST_LEN + 1, w))
            d = ssum / cnt - a_in[:, sl]
            yg = _dot(d, R.w_pool[l, g]) * _row(R.pool_scale, l)[:, sl]
            a_parts.append(yg * _silu(a_gate[:, sl]))
        a_out = jnp.concatenate(a_parts, axis=-1)

        v = _dot(h, R.w_in[l, :, COL_V:COL_V + W_B])
        vn = _head_rmsnorm(v, _row(R.v_norm_g, l))
        vn_s_ref[l] = vn
        ws00 = _per_head_lanes([R.ws[l, hh, 0:1, 0:1] for hh in range(N_HEADS_B)])
        b0 = _per_head_lanes([b_s_ref[l, hh:hh + 1, 0:1] for hh in range(N_HEADS_B)])
        s = ws00 * vn + b0
        u = _dot(h, R.w_in[l, :, COL_U:COL_U + W_B])
        b_gate = _dot(h, R.w_in[l, :, COL_B_GATE:COL_B_GATE + W_B])
        b_out = (u * s) * _silu(b_gate)

        x = x + (_dot(a_out, R.w_out[l, 0:W_A, :]) + _dot(b_out, R.w_out[l, W_A:, :]))
    ys_ref[...] = _rmsnorm(x, fin_g)


def _trunk_kernel(x_ref, x_lead_next_ref, xs_hbm, sp_hbm, norm_g_ref, w_in_hbm, w_pool_ref, pool_scale_ref, v_norm_g_ref,
                  ws_ref, b_s_ref, w_out_hbm, fin_g_ref,
                  y_ref, pool_ref, cv_ref, ys_hbm, a_in_s_ref, vn_s_hbm,
                  a_hist, s2_hist, s4_hist, s8_hist, h_buf, x_buf, w_in_bf, w_out_bf, bias_buf, sp_buf,
                  xs_buf, ys_buf, vn_s_buf, hn_buf, gate_buf, stage, sems, io_sems):
    b = pl.program_id(0)
    t = pl.program_id(1)
    tm = x_ref.shape[1]
    rows = tm // ROW_BLOCKS
    R = _Refs(w_in=w_in_bf, w_pool=w_pool_ref, pool_scale=pool_scale_ref, v_norm_g=v_norm_g_ref, ws=ws_ref,
              bias=bias_buf, w_out=w_out_bf, a_hist=a_hist, s2_hist=s2_hist, s4_hist=s4_hist, s8_hist=s8_hist,
              h_buf=h_buf, hn_buf=hn_buf, gate_buf=gate_buf)
    hists = (a_hist, s2_hist, s4_hist, s8_hist)
    fin_g = fin_g_ref[...].reshape(1, D_MODEL)

    def tile_body(first_tile, before_layer=None, between_phases=None):
        blocks = [slice(k * rows, (k + 1) * rows) for k in range(ROW_BLOCKS)]

        def w_in0(col, width):
            return jnp.dot(hn_buf[...], w_in_bf[0, :, col:col + width], preferred_element_type=F32)

        for l in range(DEPTH):
            if before_layer is not None:
                before_layer(l)

            def x_rows(rs, l=l):
                return x_ref[0, rs, :] if l == 0 else x_buf[rs, :]

            def phase_done():
                if between_phases is not None:
                    between_phases()

            if l == 0:
                if first_tile:
                    hn_buf[...] = _rmsnorm(x_rows(blocks[0]), _row(norm_g_ref, 0)).astype(BF16)
                norm_blocks = blocks[1:]
            else:
                norm_blocks = blocks
            for rs in norm_blocks:
                h_buf[rs, :] = _rmsnorm(x_rows(rs), _row(norm_g_ref, l)).astype(BF16)
            chains = [_Chain(l, k * rows, rows, t, tm, R, lead=(l == 0 and k == 0),
                             gates_done=(l == 0 and k == 0 and not first_tile)) for k in range(ROW_BLOCKS)]
            for c in chains:
                c.p0()
            phase_done()
            for c in chains:
                c.p1()
            phase_done()
            if l == DEPTH - 1:
                hn_buf[...] = _rmsnorm(x_lead_next_ref[0], _row(norm_g_ref, 0)).astype(BF16)
                gate_buf[0] = w_in0(COL_A_GATE, W_A)
            for c in chains:
                c.p2()
                phase_done()
            for buf in hists:
                buf[l, 0:HIST, :] = buf[l, tm:tm + HIST, :]
            pool_ref[l, 0] = chains[-1].pool_rows
            cv_ref[l, 0] = chains[-1].cv_rows
            for c, rs in zip(chains, blocks):
                c.p3()
                x_new = x_rows(rs) + c.out
                if l + 1 < DEPTH:
                    x_buf[rs, :] = x_new
                else:
                    y_ref[0, rs, :] = _rmsnorm(x_new, fin_g)
                phase_done()
        gate_buf[1] = w_in0(COL_B_GATE, W_B)

    def zero_hist():
        for buf in hists:
            buf[:, 0:HIST, :] = jnp.zeros((DEPTH, HIST, buf.shape[2]), F32)

    first = jnp.logical_and(b == 0, t == 0)

    @pl.when(first)
    def _():
        stream = _WeightStream(w_in_hbm, w_out_hbm, w_in_bf, w_out_bf, stage, sems)
        stream.start()
        in_copies = [
            pltpu.make_async_copy(sp_hbm, sp_buf, io_sems.at[0]),
            pltpu.make_async_copy(xs_hbm.at[:, 0, :], xs_buf, io_sems.at[1]),
        ]
        for l in range(DEPTH):
            for hh in range(N_HEADS_B):
                bias_buf[l, :, hh * HEAD_DIM_B:(hh + 1) * HEAD_DIM_B] = jnp.broadcast_to(
                    b_s_ref[l, hh:hh + 1, :], (CHUNK, CHUNK)).T
        zero_hist()

        def before_layer(l):
            stream.finish_layer(l)
            if l == DEPTH - 1:
                for cp in in_copies:
                    cp.start()

        tile_body(True, before_layer, stream.service)
        for cp in in_copies:
            cp.wait()
        _sample_group(xs_buf, sp_buf, norm_g_ref, b_s_ref, fin_g, R, ys_buf, a_in_s_ref, vn_s_buf)
        out_copies = [
            pltpu.make_async_copy(ys_buf, ys_hbm.at[:, 0, :], io_sems.at[0]),
            pltpu.make_async_copy(vn_s_buf, vn_s_hbm.at[:, :, 0, :], io_sems.at[1]),
        ]
        for cp in out_copies:
            cp.start()
        for cp in out_copies:
            cp.wait()

    @pl.when(jnp.logical_not(first))
    def _():
        pl.when(t == 0)(zero_hist)
        tile_body(False)


def _const_spec(shape):
    zeros = (0,) * len(shape)
    return pl.BlockSpec(shape, lambda b, t: zeros, pipeline_mode=pl.Buffered(1))


def _trunk_call(x, xs, sp_t, norm_g, w_in, w_pool, pool_scale, v_norm_g, w_s, b_s, w_out, fin_g):
    batch, seq, d = x.shape
    n_s = xs.shape[0]
    tm = ROW_TILE
    assert seq % tm == 0 and tm % (ROW_BLOCKS * CHUNK) == 0
    assert d % WEIGHT_CHUNK_ROWS == 0 and (W_A + W_B) % WEIGHT_CHUNK_ROWS == 0
    hbm = pl.BlockSpec(memory_space=pl.ANY)
    tiles_per_seq = seq // tm
    n_tiles = batch * tiles_per_seq
    rows = tm // ROW_BLOCKS

    def lead_block_of_next_tile(b, t):
        i = jnp.minimum(b * tiles_per_seq + t + 1, n_tiles - 1)
        return (i // tiles_per_seq, (i % tiles_per_seq) * ROW_BLOCKS, 0)

    consts = (xs, sp_t, norm_g, w_in, w_pool, pool_scale, v_norm_g, w_s, b_s, w_out, fin_g)
    in_specs = [pl.BlockSpec((1, tm, d), lambda b, t: (b, t, 0)),
                pl.BlockSpec((1, rows, d), lead_block_of_next_tile)]
    for a in consts:
        in_specs.append(hbm if (a is w_in or a is w_out or a is sp_t or a is xs) else _const_spec(a.shape))
    args = (x, x) + consts
    return pl.pallas_call(
        _trunk_kernel,
        grid=(batch, seq // tm),
        in_specs=in_specs,
        out_specs=[
            pl.BlockSpec((1, tm, d), lambda b, t: (b, t, 0)),
            pl.BlockSpec((DEPTH, 1, POOL_BUF, W_A), lambda b, t: (0, b, 0, 0)),
            pl.BlockSpec((DEPTH, 1, CHUNK, W_B), lambda b, t: (0, b, 0, 0)),
            hbm,
            pl.BlockSpec((DEPTH, n_s, W_A), lambda b, t: (0, 0, 0)),
            hbm,
        ],
        out_shape=[
            jax.ShapeDtypeStruct((batch, seq, d), F32),
            jax.ShapeDtypeStruct((DEPTH, batch, POOL_BUF, W_A), F32),
            jax.ShapeDtypeStruct((DEPTH, batch, CHUNK, W_B), F32),
            jax.ShapeDtypeStruct((n_s, 1, d), F32),
            jax.ShapeDtypeStruct((DEPTH, n_s, W_A), F32),
            jax.ShapeDtypeStruct((DEPTH, n_s, 1, W_B), F32),
        ],
        scratch_shapes=[
            pltpu.VMEM((DEPTH, HIST + tm, W_A), F32),
            pltpu.VMEM((DEPTH, HIST + tm, W_A - POOL_GROUP_DIM), F32),
            pltpu.VMEM((DEPTH, HIST + tm, W_A - 2 * POOL_GROUP_DIM), F32),
            pltpu.VMEM((DEPTH, HIST + tm, W_A - 3 * POOL_GROUP_DIM), F32),
            pltpu.VMEM((tm, d), BF16),
            pltpu.VMEM((tm, d), F32),
            pltpu.VMEM((DEPTH, d, D_IN), BF16),
            pltpu.VMEM((DEPTH, W_A + W_B, d), BF16),
            pltpu.VMEM((DEPTH, CHUNK, W_B), F32),
            pltpu.VMEM((DEPTH, POOL_BUF, n_s, W_A), F32),
            pltpu.VMEM((n_s, d), F32),
            pltpu.VMEM((n_s, d), F32),
            pltpu.VMEM((DEPTH, n_s, W_B), F32),
            pltpu.VMEM((rows, d), BF16),
            pltpu.VMEM((2, rows, W_B), F32),
            pltpu.VMEM((WEIGHT_SLOTS, WEIGHT_CHUNK_ROWS, D_IN), F32),
            pltpu.SemaphoreType.DMA((WEIGHT_SLOTS,)),
            pltpu.SemaphoreType.DMA((2,)),
        ],
        compiler_params=pltpu.CompilerParams(
            dimension_semantics=("arbitrary", "arbitrary"),
            vmem_limit_bytes=VMEM_LIMIT_BYTES,
        ),
        name="trunk",
    )(*args)


def kernel(x_prompt, x_sample, state_pool, norm_g, w_in, w_pool, pool_scale, v_norm_g, w_s, b_s, w_out, final_norm_g):
    assert x_sample.shape[1] == 1 and state_pool.shape[2] == POOL_BUF
    sp_t = jnp.transpose(state_pool, (0, 2, 1, 3))
    y_prompt, pool_prompt, chunk_v_prompt, y_sample, a_in_s, chunk_v_sample = _trunk_call(
        x_prompt, x_sample, sp_t, norm_g, w_in, w_pool, pool_scale, v_norm_g, w_s, b_s, w_out, final_norm_g)
    is_new_row = lax.broadcasted_iota(jnp.int32, (1, 1, POOL_BUF, 1), 2) == POOL_BUF - 1
    pool_sample = jnp.where(is_new_row, a_in_s[:, :, None, :], jnp.roll(state_pool, -1, axis=2))
    return (y_prompt, y_sample, pool_prompt, pool_sample, chunk_v_prompt, chunk_v_sample)
```

```python
import jax
import jax.numpy as jnp
from jax import lax
from jax.experimental import pallas as pl
from jax.experimental.pallas import tpu as pltpu

D_MODEL = 1024
DEPTH = 2
W_A = 512
W_B = 512
D_IN = 2 * W_A + 3 * W_B
POOL_WINDOWS = (2, 4, 8, 16)
N_POOL_GROUPS = len(POOL_WINDOWS)
POOL_GROUP_DIM = W_A // N_POOL_GROUPS
POOL_BUF = max(POOL_WINDOWS) - 1
CHUNK = 128
N_HEADS_B = 4
HEAD_DIM_B = W_B // N_HEADS_B
EPS = 1e-6
SAMPLE_PAST_LEN = 16384

COL_A_IN = 0
COL_A_GATE = W_A
COL_U = 2 * W_A
COL_V = 2 * W_A + W_B
COL_B_GATE = 2 * W_A + 2 * W_B

HIST = 16
ROW_TILE = 512
ROW_BLOCKS = 2
WEIGHT_CHUNK_ROWS = 256
WEIGHT_SLOTS = 3
VMEM_LIMIT_BYTES = 56 * 1024 * 1024

BF16 = jnp.bfloat16
F32 = jnp.float32


def _dot(a, b):
    return jnp.dot(a.astype(BF16), b.astype(BF16), preferred_element_type=F32)


def _rmsnorm(x, g):
    ms = jnp.mean(x * x, axis=-1, keepdims=True)
    return (x * lax.rsqrt(ms + EPS)) * g


def _silu(x):
    hx = 0.5 * x
    return hx * jnp.tanh(hx) + hx


def _head_rmsnorm(v, g):
    parts = []
    for h in range(N_HEADS_B):
        sl = slice(h * HEAD_DIM_B, (h + 1) * HEAD_DIM_B)
        parts.append(_rmsnorm(v[:, sl], g[:, sl]))
    return jnp.concatenate(parts, axis=-1)


def _masked_ws(ws_ref, l):
    row = lax.broadcasted_iota(jnp.int32, (CHUNK, CHUNK), 0)
    col = lax.broadcasted_iota(jnp.int32, (CHUNK, CHUNK), 1)
    keep = col <= row
    return [jnp.where(keep, ws_ref[l, h], 0.0).astype(BF16) for h in range(N_HEADS_B)]


class _Refs:
    def __init__(self, **kw):
        self.__dict__.update(kw)


def _row(ref, l):
    return ref[l:l + 1, :]


class _WeightStream:
    def __init__(self, w_in_hbm, w_out_hbm, w_in_bf, w_out_bf, stage, sems):
        self.chunks = []
        for l in range(DEPTH):
            for src, dst in ((w_in_hbm, w_in_bf), (w_out_hbm, w_out_bf)):
                _, k, n = src.shape
                self.chunks += [(src, dst, l, r, n) for r in range(0, k, WEIGHT_CHUNK_ROWS)]
        self.stage, self.sems = stage, sems
        self.done = 0

    def _copy(self, i):
        src, _, l, r, n = self.chunks[i]
        slot = i % WEIGHT_SLOTS
        return pltpu.make_async_copy(
            src.at[l, pl.ds(r, WEIGHT_CHUNK_ROWS), :], self.stage.at[slot, :, pl.ds(0, n)], self.sems.at[slot])

    def start(self):
        for i in range(WEIGHT_SLOTS - 1):
            self._copy(i).start()

    def service(self):
        i = self.done
        if i == len(self.chunks):
            return
        if i + WEIGHT_SLOTS - 1 < len(self.chunks):
            self._copy(i + WEIGHT_SLOTS - 1).start()
        self._copy(i).wait()
        _, dst, l, r, n = self.chunks[i]
        dst[l, r:r + WEIGHT_CHUNK_ROWS, :] = self.stage[i % WEIGHT_SLOTS, :, 0:n].astype(BF16)
        self.done += 1

    def finish(self, dst, l):
        last = max(i for i, c in enumerate(self.chunks) if c[1] is dst and c[2] == l)
        while self.done <= last:
            self.service()


class _Chain:
    def __init__(self, l, r0, rows, t, tm, R, lead=False, gates_done=False):
        self.l, self.r0, self.rows, self.t, self.tm, self.R = l, r0, rows, t, tm, R
        self.lead, self.gates_done = lead, gates_done

    def _in(self, col, width):
        R = self.R
        h = R.hn_buf[...] if self.lead else R.h_buf[self.r0:self.r0 + self.rows, :]
        return jnp.dot(h, R.w_in[self.l, :, col:col + width], preferred_element_type=F32)

    def p0(self):
        self.v = self._in(COL_V, W_B)
        self.a_in = self._in(COL_A_IN, W_A)

    def p1(self):
        self.a_gate = self.R.gate_buf[0] if self.gates_done else self._in(COL_A_GATE, W_A)
        self.u = self._in(COL_U, W_B)

    def p2(self):
        l, r0, rows, R = self.l, self.r0, self.rows, self.R
        n_chunks = rows // CHUNK
        a_in = self.a_in
        base = HIST + r0

        vn = _head_rmsnorm(self.v, _row(R.v_norm_g, l))
        self.cv_rows = vn[rows - CHUNK:, :]
        vn_b = vn.astype(BF16)

        a_hist, s2_hist, s4_hist, s8_hist = R.hists
        a_hist[base:base + rows, :] = a_in
        s2 = a_in + a_hist[base - 1:base - 1 + rows, :]
        s2_hist[base:base + rows, :] = s2[:, POOL_GROUP_DIM:]
        s4 = s2[:, POOL_GROUP_DIM:] + s2_hist[base - 2:base - 2 + rows, :]
        s4_hist[base:base + rows, :] = s4[:, POOL_GROUP_DIM:]
        s8 = s4[:, POOL_GROUP_DIM:] + s4_hist[base - 4:base - 4 + rows, :]
        s8_hist[base:base + rows, :] = s8[:, POOL_GROUP_DIM:]
        s16 = s8[:, POOL_GROUP_DIM:] + s8_hist[base - 8:base - 8 + rows, :]
        win_sums = (s2[:, :POOL_GROUP_DIM], s4[:, :POOL_GROUP_DIM], s8[:, :POOL_GROUP_DIM], s16)
        self.pool_rows = a_in[rows - POOL_BUF:, :]

        d_parts = []
        for g, w in enumerate(POOL_WINDOWS):
            sl = slice(g * POOL_GROUP_DIM, (g + 1) * POOL_GROUP_DIM)
            ssum = win_sums[g]
            if r0 == 0:
                pos1 = lax.broadcasted_iota(jnp.int32, (HIST, POOL_GROUP_DIM), 0) + self.t * self.tm + 1
                inv_head = 1.0 / jnp.minimum(pos1, w).astype(F32)
                mean = jnp.concatenate([ssum[:HIST] * inv_head, ssum[HIST:] * (1.0 / w)], axis=0)
            else:
                mean = ssum * (1.0 / w)
            d_parts.append((mean - a_in[:, sl]).astype(BF16))

        ws_b = _masked_ws(R.ws, l)
        s_heads = []
        for hh in range(N_HEADS_B):
            sl = slice(hh * HEAD_DIM_B, (hh + 1) * HEAD_DIM_B)
            rhs = jnp.concatenate([vn_b[c * CHUNK:(c + 1) * CHUNK, sl] for c in range(n_chunks)], axis=1)
            res = jnp.dot(ws_b[hh], rhs, preferred_element_type=F32)
            s_heads.append(jnp.concatenate(
                [res[:, c * HEAD_DIM_B:(c + 1) * HEAD_DIM_B] for c in range(n_chunks)], axis=0))
        self.s_heads = s_heads

        self.y_parts = [_dot(d_parts[g], R.w_pool[l, g]) for g in range(N_POOL_GROUPS)]
        self.b_gate = self.R.gate_buf[1] if self.gates_done else self._in(COL_B_GATE, W_B)

    def p3(self):
        l, rows, R = self.l, self.rows, self.R
        n_chunks = rows // CHUNK
        a_out = ((jnp.concatenate(self.y_parts, axis=-1) * _row(R.pool_scale, l)) * _silu(self.a_gate)).astype(BF16)
        out_a = _dot(a_out, R.w_out[l, 0:W_A, :])
        s = jnp.concatenate(self.s_heads, axis=-1) + jnp.concatenate([R.bias[l]] * n_chunks, axis=0)
        b_out = ((self.u * s) * _silu(self.b_gate)).astype(BF16)
        self.out = out_a + _dot(b_out, R.w_out[l, W_A:, :])


def _per_head_lanes(vals):
    return jnp.concatenate([jnp.broadcast_to(v, (1, HEAD_DIM_B)) for v in vals], axis=-1)


def _sample_group(xs_ref, sp_ref, norm_g_ref, b_s_ref, fin_g, R, ys_ref, a_in_s_ref, vn_s_refs):
    x = xs_ref[...]
    for l in range(DEPTH):
        h = _rmsnorm(x, _row(norm_g_ref, l)).astype(BF16)
        a_in = _dot(h, R.w_in[l, :, COL_A_IN:COL_A_IN + W_A])
        a_in_s_ref[l] = a_in
        a_gate = _dot(h, R.w_in[l, :, COL_A_GATE:COL_A_GATE + W_A])
        a_parts = []
        for g, w in enumerate(POOL_WINDOWS):
            sl = slice(g * POOL_GROUP_DIM, (g + 1) * POOL_GROUP_DIM)
            ssum = a_in[:, sl]
            for k in range(1, w):
                ssum = ssum + sp_ref[l, POOL_BUF - k, :, sl]
            cnt = float(min(SAMPLE_PAST_LEN + 1, w))
            d = ssum / cnt - a_in[:, sl]
            yg = _dot(d, R.w_pool[l, g]) * _row(R.pool_scale, l)[:, sl]
            a_parts.append(yg * _silu(a_gate[:, sl]))
        a_out = jnp.concatenate(a_parts, axis=-1)

        v = _dot(h, R.w_in[l, :, COL_V:COL_V + W_B])
        vn = _head_rmsnorm(v, _row(R.v_norm_g, l))
        vn_s_refs[l][...] = vn
        ws00 = _per_head_lanes([R.ws[l, hh, 0:1, 0:1] for hh in range(N_HEADS_B)])
        b0 = _per_head_lanes([b_s_ref[l, hh:hh + 1, 0:1] for hh in range(N_HEADS_B)])
        s = ws00 * vn + b0
        u = _dot(h, R.w_in[l, :, COL_U:COL_U + W_B])
        b_gate = _dot(h, R.w_in[l, :, COL_B_GATE:COL_B_GATE + W_B])
        b_out = (u * s) * _silu(b_gate)

        x = x + (_dot(a_out, R.w_out[l, 0:W_A, :]) + _dot(b_out, R.w_out[l, W_A:, :]))
    ys_ref[...] = _rmsnorm(x, fin_g)


def _trunk_kernel(x_ref, x_lead_next_ref, xs_hbm, sp_hbm, norm_g_ref, w_in_hbm, w_pool_ref, pool_scale_ref, v_norm_g_ref,
                  ws_ref, b_s_ref, w_out_hbm, fin_g_ref,
                  y_ref, pool_ref, cv_ref, ys_hbm, a_in_s_ref, vn_s_hbm,
                  a_hist, s2_hist, s4_hist, s8_hist, a_carry, s2_carry, s4_carry, s8_carry, h_buf, x_buf, w_in_bf, w_out_bf, bias_buf, sp_buf,
                  hn_buf, gate_buf, stage, sems, io_sems):
    b = pl.program_id(0)
    t = pl.program_id(1)
    tm = x_ref.shape[1]
    rows = tm // ROW_BLOCKS
    R = _Refs(w_in=w_in_bf, w_pool=w_pool_ref, pool_scale=pool_scale_ref, v_norm_g=v_norm_g_ref, ws=ws_ref,
              bias=bias_buf, w_out=w_out_bf, hists=(a_hist, s2_hist, s4_hist, s8_hist), h_buf=h_buf,
              hn_buf=hn_buf, gate_buf=gate_buf)
    hists = R.hists
    carries = (a_carry, s2_carry, s4_carry, s8_carry)
    fin_g = fin_g_ref[...].reshape(1, D_MODEL)

    def tile_body(first_tile, need=None, between_phases=None):
        blocks = [slice(k * rows, (k + 1) * rows) for k in range(ROW_BLOCKS)]

        def next_gate(col, width):
            return jnp.dot(hn_buf[...], w_in_bf[0, :, col:col + width], preferred_element_type=F32)

        for l in range(DEPTH):
            if need is not None:
                need(w_in_bf, l)

            def x_rows(rs, l=l):
                return x_ref[0, rs, :] if l == 0 else x_buf[rs, :]

            def phase_done():
                if between_phases is not None:
                    between_phases()

            if l == 0 and first_tile:
                hn_buf[...] = _rmsnorm(x_rows(blocks[0]), _row(norm_g_ref, 0)).astype(BF16)
            for rs in (blocks[1:] if l == 0 else blocks):
                h_buf[rs, :] = _rmsnorm(x_rows(rs), _row(norm_g_ref, l)).astype(BF16)
            chains = [_Chain(l, k * rows, rows, t, tm, R, lead=(l == 0 and k == 0),
                             gates_done=(l == 0 and k == 0 and not first_tile)) for k in range(ROW_BLOCKS)]
            for c in chains:
                c.p0()
            phase_done()
            for c in chains:
                c.p1()
            phase_done()
            if l == DEPTH - 1:
                hn_buf[...] = _rmsnorm(x_lead_next_ref[0], _row(norm_g_ref, 0)).astype(BF16)
                gate_buf[0] = next_gate(COL_A_GATE, W_A)
            for buf, carry in zip(hists, carries):
                buf[0:HIST, :] = carry[l]
            for c in chains:
                c.p2()
                phase_done()
            if need is not None:
                need(w_out_bf, l)
            for buf, carry in zip(hists, carries):
                carry[l] = buf[tm:tm + HIST, :]
            pool_ref[l, 0] = chains[-1].pool_rows
            cv_ref[l, 0] = chains[-1].cv_rows
            for c, rs in zip(chains, blocks):
                c.p3()
                x_new = x_rows(rs) + c.out
                if l + 1 < DEPTH:
                    x_buf[rs, :] = x_new
                else:
                    y_ref[0, rs, :] = _rmsnorm(x_new, fin_g)
                phase_done()
        gate_buf[1] = next_gate(COL_B_GATE, W_B)

    def zero_hist():
        for carry in carries:
            carry[...] = jnp.zeros(carry.shape, F32)

    first = jnp.logical_and(b == 0, t == 0)

    @pl.when(first)
    def _():
        stream = _WeightStream(w_in_hbm, w_out_hbm, w_in_bf, w_out_bf, stage, sems)
        stream.start()
        n_s = xs_hbm.shape[0]
        xs_buf = x_buf.at[pl.ds(0, n_s), :]
        ys_buf = x_buf.at[pl.ds(n_s, n_s), :]
        vn_s_bufs = [x_buf.at[pl.ds((2 + l) * n_s, n_s), pl.ds(0, W_B)] for l in range(DEPTH)]
        sp_copy = pltpu.make_async_copy(sp_hbm, sp_buf, io_sems.at[0])
        xs_copy = pltpu.make_async_copy(xs_hbm.at[:, 0, :], xs_buf, io_sems.at[1])
        for l in range(DEPTH):
            for hh in range(N_HEADS_B):
                bias_buf[l, :, hh * HEAD_DIM_B:(hh + 1) * HEAD_DIM_B] = jnp.broadcast_to(
                    b_s_ref[l, hh:hh + 1, :], (CHUNK, CHUNK)).T
        zero_hist()

        def need(w_bf, l):
            stream.finish(w_bf, l)
            if w_bf is w_out_bf and l == DEPTH - 1:
                sp_copy.start()

        tile_body(True, need, stream.service)
        xs_copy.start()
        sp_copy.wait()
        xs_copy.wait()
        _sample_group(xs_buf, sp_buf, norm_g_ref, b_s_ref, fin_g, R, ys_buf, a_in_s_ref, vn_s_bufs)
        out_copies = [pltpu.make_async_copy(ys_buf, ys_hbm.at[:, 0, :], io_sems.at[0])]
        out_copies += [pltpu.make_async_copy(vn_s_bufs[l], vn_s_hbm.at[l, :, 0, :], io_sems.at[1 + l])
                       for l in range(DEPTH)]
        for cp in out_copies:
            cp.start()
        for cp in out_copies:
            cp.wait()

    @pl.when(jnp.logical_not(first))
    def _():
        pl.when(t == 0)(zero_hist)
        tile_body(False)


def _const_spec(shape):
    zeros = (0,) * len(shape)
    return pl.BlockSpec(shape, lambda b, t: zeros, pipeline_mode=pl.Buffered(1))


def _trunk_call(x, xs, sp_t, norm_g, w_in, w_pool, pool_scale, v_norm_g, w_s, b_s, w_out, fin_g):
    batch, seq, d = x.shape
    n_s = xs.shape[0]
    tm = ROW_TILE
    assert seq % tm == 0 and tm % (ROW_BLOCKS * CHUNK) == 0
    assert d % WEIGHT_CHUNK_ROWS == 0 and (W_A + W_B) % WEIGHT_CHUNK_ROWS == 0
    assert (2 + DEPTH) * n_s <= tm and W_B <= d
    hbm = pl.BlockSpec(memory_space=pl.ANY)
    tiles_per_seq = seq // tm
    n_tiles = batch * tiles_per_seq
    rows = tm // ROW_BLOCKS

    def lead_block_of_next_tile(b, t):
        i = jnp.minimum(b * tiles_per_seq + t + 1, n_tiles - 1)
        return (i // tiles_per_seq, (i % tiles_per_seq) * ROW_BLOCKS, 0)

    consts = (xs, sp_t, norm_g, w_in, w_pool, pool_scale, v_norm_g, w_s, b_s, w_out, fin_g)
    in_specs = [pl.BlockSpec((1, tm, d), lambda b, t: (b, t, 0)),
                pl.BlockSpec((1, rows, d), lead_block_of_next_tile)]
    for a in consts:
        in_specs.append(hbm if (a is w_in or a is w_out or a is sp_t or a is xs) else _const_spec(a.shape))
    args = (x, x) + consts
    return pl.pallas_call(
        _trunk_kernel,
        grid=(batch, seq // tm),
        in_specs=in_specs,
        out_specs=[
            pl.BlockSpec((1, tm, d), lambda b, t: (b, t, 0)),
            pl.BlockSpec((DEPTH, 1, POOL_BUF, W_A), lambda b, t: (0, b, 0, 0)),
            pl.BlockSpec((DEPTH, 1, CHUNK, W_B), lambda b, t: (0, b, 0, 0)),
            hbm,
            pl.BlockSpec((DEPTH, n_s, W_A), lambda b, t: (0, 0, 0)),
            hbm,
        ],
        out_shape=[
            jax.ShapeDtypeStruct((batch, seq, d), F32),
            jax.ShapeDtypeStruct((DEPTH, batch, POOL_BUF, W_A), F32),
            jax.ShapeDtypeStruct((DEPTH, batch, CHUNK, W_B), F32),
            jax.ShapeDtypeStruct((n_s, 1, d), F32),
            jax.ShapeDtypeStruct((DEPTH, n_s, W_A), F32),
            jax.ShapeDtypeStruct((DEPTH, n_s, 1, W_B), F32),
        ],
        scratch_shapes=[
            pltpu.VMEM((HIST + tm, W_A), F32),
            pltpu.VMEM((HIST + tm, W_A - POOL_GROUP_DIM), F32),
            pltpu.VMEM((HIST + tm, W_A - 2 * POOL_GROUP_DIM), F32),
            pltpu.VMEM((HIST + tm, W_A - 3 * POOL_GROUP_DIM), F32),
            pltpu.VMEM((DEPTH, HIST, W_A), F32),
            pltpu.VMEM((DEPTH, HIST, W_A - POOL_GROUP_DIM), F32),
            pltpu.VMEM((DEPTH, HIST, W_A - 2 * POOL_GROUP_DIM), F32),
            pltpu.VMEM((DEPTH, HIST, W_A - 3 * POOL_GROUP_DIM), F32),
            pltpu.VMEM((tm, d), BF16),
            pltpu.VMEM((tm, d), F32),
            pltpu.VMEM((DEPTH, d, D_IN), BF16),
            pltpu.VMEM((DEPTH, W_A + W_B, d), BF16),
            pltpu.VMEM((DEPTH, CHUNK, W_B), F32),
            pltpu.VMEM((DEPTH, POOL_BUF, n_s, W_A), F32),
            pltpu.VMEM((rows, d), BF16),
            pltpu.VMEM((2, rows, W_B), F32),
            pltpu.VMEM((WEIGHT_SLOTS, WEIGHT_CHUNK_ROWS, D_IN), F32),
            pltpu.SemaphoreType.DMA((WEIGHT_SLOTS,)),
            pltpu.SemaphoreType.DMA((1 + DEPTH,)),
        ],
        compiler_params=pltpu.CompilerParams(
            dimension_semantics=("arbitrary", "arbitrary"),
            vmem_limit_bytes=VMEM_LIMIT_BYTES,
        ),
        name="trunk",
    )(*args)


def kernel(x_prompt, x_sample, state_pool, norm_g, w_in, w_pool, pool_scale, v_norm_g, w_s, b_s, w_out, final_norm_g):
    assert x_sample.shape[1] == 1 and state_pool.shape[2] == POOL_BUF
    sp_t = jnp.transpose(state_pool, (0, 2, 1, 3))
    y_prompt, pool_prompt, chunk_v_prompt, y_sample, a_in_s, chunk_v_sample = _trunk_call(
        x_prompt, x_sample, sp_t, norm_g, w_in, w_pool, pool_scale, v_norm_g, w_s, b_s, w_out, final_norm_g)
    is_new_row = lax.broadcasted_iota(jnp.int32, (1, 1, POOL_BUF, 1), 2) == POOL_BUF - 1
    pool_sample = jnp.where(is_new_row, a_in_s[:, :, None, :], jnp.roll(state_pool, -1, axis=2))
    return (y_prompt, y_sample, pool_prompt, pool_sample, chunk_v_prompt, chunk_v_sample)
```

```python
import jax
import jax.numpy as jnp
from jax import lax
from jax.experimental import pallas as pl
from jax.experimental.pallas import tpu as pltpu

D_MODEL = 1024
DEPTH = 2
W_A = 512
W_B = 512
D_IN = 2 * W_A + 3 * W_B
POOL_WINDOWS = (2, 4, 8, 16)
N_POOL_GROUPS = len(POOL_WINDOWS)
POOL_GROUP_DIM = W_A // N_POOL_GROUPS
POOL_BUF = max(POOL_WINDOWS) - 1
CHUNK = 128
N_HEADS_B = 4
HEAD_DIM_B = W_B // N_HEADS_B
EPS = 1e-6
SAMPLE_PAST_LEN = 16384

COL_A_IN = 0
COL_A_GATE = W_A
COL_U = 2 * W_A
COL_V = 2 * W_A + W_B
COL_B_GATE = 2 * W_A + 2 * W_B

HIST = 16
ROW_TILE = 512
ROW_BLOCKS = 2
WEIGHT_CHUNK_ROWS = 256
WEIGHT_SLOTS = 4
VMEM_LIMIT_BYTES = 56 * 1024 * 1024

BF16 = jnp.bfloat16
F32 = jnp.float32


def _dot(a, b):
    return jnp.dot(a.astype(BF16), b.astype(BF16), preferred_element_type=F32)


def _rmsnorm(x, g):
    ms = jnp.mean(x * x, axis=-1, keepdims=True)
    return (x * lax.rsqrt(ms + EPS)) * g


def _silu(x):
    hx = 0.5 * x
    return hx * jnp.tanh(hx) + hx


def _head_rmsnorm(v, g):
    parts = []
    for h in range(N_HEADS_B):
        sl = slice(h * HEAD_DIM_B, (h + 1) * HEAD_DIM_B)
        parts.append(_rmsnorm(v[:, sl], g[:, sl]))
    return jnp.concatenate(parts, axis=-1)


def _masked_ws(ws_ref, l):
    row = lax.broadcasted_iota(jnp.int32, (CHUNK, CHUNK), 0)
    col = lax.broadcasted_iota(jnp.int32, (CHUNK, CHUNK), 1)
    keep = col <= row
    return [jnp.where(keep, ws_ref[l, h], 0.0).astype(BF16) for h in range(N_HEADS_B)]


class _Refs:
    def __init__(self, **kw):
        self.__dict__.update(kw)


def _row(ref, l):
    return ref[l:l + 1, :]


class _WeightStream:
    def __init__(self, w_in_hbm, w_out_hbm, w_in_bf, w_out_bf, stage, sems):
        self.chunks = []
        for l in range(DEPTH):
            for src, dst in ((w_in_hbm, w_in_bf), (w_out_hbm, w_out_bf)):
                _, k, n = src.shape
                self.chunks += [(src, dst, l, r, n) for r in range(0, k, WEIGHT_CHUNK_ROWS)]
        self.per_layer = len(self.chunks) // DEPTH
        self.stage, self.sems = stage, sems
        self.done = 0

    def _copy(self, i):
        src, _, l, r, n = self.chunks[i]
        slot = i % WEIGHT_SLOTS
        return pltpu.make_async_copy(
            src.at[l, pl.ds(r, WEIGHT_CHUNK_ROWS), :], self.stage.at[slot, :, pl.ds(0, n)], self.sems.at[slot])

    def start(self):
        for i in range(WEIGHT_SLOTS - 1):
            self._copy(i).start()

    def service(self):
        i = self.done
        if i == len(self.chunks):
            return
        if i + WEIGHT_SLOTS - 1 < len(self.chunks):
            self._copy(i + WEIGHT_SLOTS - 1).start()
        self._copy(i).wait()
        _, dst, l, r, n = self.chunks[i]
        dst[l, r:r + WEIGHT_CHUNK_ROWS, :] = self.stage[i % WEIGHT_SLOTS, :, 0:n].astype(BF16)
        self.done += 1

    def finish_layer(self, l):
        while self.done < self.per_layer * (l + 1):
            self.service()


class _Chain:
    def __init__(self, l, r0, rows, t, tm, R):
        self.l, self.r0, self.rows, self.t, self.tm, self.R = l, r0, rows, t, tm, R

    def _in(self, col, width):
        R = self.R
        return jnp.dot(R.h_buf[self.r0:self.r0 + self.rows, :], R.w_in[self.l, :, col:col + width],
                       preferred_element_type=F32)

    def p0(self):
        self.v = self._in(COL_V, W_B)
        self.a_in = self._in(COL_A_IN, W_A)

    def p1(self):
        self.a_gate = self._in(COL_A_GATE, W_A)
        self.u = self._in(COL_U, W_B)

    def p2(self):
        l, r0, rows, R = self.l, self.r0, self.rows, self.R
        n_chunks = rows // CHUNK
        a_in = self.a_in
        base = HIST + r0

        vn = _head_rmsnorm(self.v, _row(R.v_norm_g, l))
        self.cv_rows = vn[rows - CHUNK:, :]
        vn_b = vn.astype(BF16)

        a_hist, s2_hist, s4_hist, s8_hist = R.hists
        a_hist[base:base + rows, :] = a_in
        s2 = a_in + a_hist[base - 1:base - 1 + rows, :]
        s2_hist[base:base + rows, :] = s2[:, POOL_GROUP_DIM:]
        s4 = s2[:, POOL_GROUP_DIM:] + s2_hist[base - 2:base - 2 + rows, :]
        s4_hist[base:base + rows, :] = s4[:, POOL_GROUP_DIM:]
        s8 = s4[:, POOL_GROUP_DIM:] + s4_hist[base - 4:base - 4 + rows, :]
        s8_hist[base:base + rows, :] = s8[:, POOL_GROUP_DIM:]
        s16 = s8[:, POOL_GROUP_DIM:] + s8_hist[base - 8:base - 8 + rows, :]
        win_sums = (s2[:, :POOL_GROUP_DIM], s4[:, :POOL_GROUP_DIM], s8[:, :POOL_GROUP_DIM], s16)
        self.pool_rows = a_in[rows - POOL_BUF:, :]

        d_parts = []
        for g, w in enumerate(POOL_WINDOWS):
            sl = slice(g * POOL_GROUP_DIM, (g + 1) * POOL_GROUP_DIM)
            ssum = win_sums[g]
            if r0 == 0:
                pos1 = lax.broadcasted_iota(jnp.int32, (HIST, POOL_GROUP_DIM), 0) + self.t * self.tm + 1
                inv_head = 1.0 / jnp.minimum(pos1, w).astype(F32)
                mean = jnp.concatenate([ssum[:HIST] * inv_head, ssum[HIST:] * (1.0 / w)], axis=0)
            else:
                mean = ssum * (1.0 / w)
            d_parts.append((mean - a_in[:, sl]).astype(BF16))

        ws_b = _masked_ws(R.ws, l)
        s_heads = []
        for hh in range(N_HEADS_B):
            sl = slice(hh * HEAD_DIM_B, (hh + 1) * HEAD_DIM_B)
            rhs = jnp.concatenate([vn_b[c * CHUNK:(c + 1) * CHUNK, sl] for c in range(n_chunks)], axis=1)
            res = jnp.dot(ws_b[hh], rhs, preferred_element_type=F32)
            s_heads.append(jnp.concatenate(
                [res[:, c * HEAD_DIM_B:(c + 1) * HEAD_DIM_B] for c in range(n_chunks)], axis=0))
        self.s_heads = s_heads

        self.y_parts = [_dot(d_parts[g], R.w_pool[l, g]) for g in range(N_POOL_GROUPS)]
        self.b_gate = self._in(COL_B_GATE, W_B)

    def p3(self):
        l, rows, R = self.l, self.rows, self.R
        n_chunks = rows // CHUNK
        a_out = ((jnp.concatenate(self.y_parts, axis=-1) * _row(R.pool_scale, l)) * _silu(self.a_gate)).astype(BF16)
        out_a = _dot(a_out, R.w_out[l, 0:W_A, :])
        s = jnp.concatenate(self.s_heads, axis=-1) + jnp.concatenate([R.bias[l]] * n_chunks, axis=0)
        b_out = ((self.u * s) * _silu(self.b_gate)).astype(BF16)
        self.out = out_a + _dot(b_out, R.w_out[l, W_A:, :])


def _per_head_lanes(vals):
    return jnp.concatenate([jnp.broadcast_to(v, (1, HEAD_DIM_B)) for v in vals], axis=-1)


def _sample_group(xs_ref, sp_ref, norm_g_ref, b_s_ref, fin_g, R, ys_ref, a_in_s_ref, vn_s_ref):
    x = xs_ref[...]
    for l in range(DEPTH):
        h = _rmsnorm(x, _row(norm_g_ref, l)).astype(BF16)
        a_in = _dot(h, R.w_in[l, :, COL_A_IN:COL_A_IN + W_A])
        a_in_s_ref[l] = a_in
        a_gate = _dot(h, R.w_in[l, :, COL_A_GATE:COL_A_GATE + W_A])
        a_parts = []
        for g, w in enumerate(POOL_WINDOWS):
            sl = slice(g * POOL_GROUP_DIM, (g + 1) * POOL_GROUP_DIM)
            ssum = a_in[:, sl]
            for k in range(1, w):
                ssum = ssum + sp_ref[l, POOL_BUF - k, :, sl]
            cnt = float(min(SAMPLE_PAST_LEN + 1, w))
            d = ssum / cnt - a_in[:, sl]
            yg = _dot(d, R.w_pool[l, g]) * _row(R.pool_scale, l)[:, sl]
            a_parts.append(yg * _silu(a_gate[:, sl]))
        a_out = jnp.concatenate(a_parts, axis=-1)

        v = _dot(h, R.w_in[l, :, COL_V:COL_V + W_B])
        vn = _head_rmsnorm(v, _row(R.v_norm_g, l))
        vn_s_ref[l] = vn
        ws00 = _per_head_lanes([R.ws[l, hh, 0:1, 0:1] for hh in range(N_HEADS_B)])
        b0 = _per_head_lanes([b_s_ref[l, hh:hh + 1, 0:1] for hh in range(N_HEADS_B)])
        s = ws00 * vn + b0
        u = _dot(h, R.w_in[l, :, COL_U:COL_U + W_B])
        b_gate = _dot(h, R.w_in[l, :, COL_B_GATE:COL_B_GATE + W_B])
        b_out = (u * s) * _silu(b_gate)

        x = x + (_dot(a_out, R.w_out[l, 0:W_A, :]) + _dot(b_out, R.w_out[l, W_A:, :]))
    ys_ref[...] = _rmsnorm(x, fin_g)


def _trunk_kernel(x_ref, xs_hbm, sp_hbm, norm_g_ref, w_in_hbm, w_pool_ref, pool_scale_ref, v_norm_g_ref,
                  ws_ref, b_s_ref, w_out_hbm, fin_g_ref,
                  y_ref, pool_ref, cv_ref, ys_hbm, a_in_s_ref, vn_s_hbm,
                  a_hist, s2_hist, s4_hist, s8_hist, a_carry, s2_carry, s4_carry, s8_carry,
                  h_buf, x_buf, w_in_bf, w_out_bf, bias_buf, sp_buf,
                  xs_buf, ys_buf, vn_s_buf, stage, sems, io_sems):
    b = pl.program_id(0)
    t = pl.program_id(1)
    tm = x_ref.shape[1]
    rows = tm // ROW_BLOCKS
    R = _Refs(w_in=w_in_bf, w_pool=w_pool_ref, pool_scale=pool_scale_ref, v_norm_g=v_norm_g_ref, ws=ws_ref,
              bias=bias_buf, w_out=w_out_bf, hists=(a_hist, s2_hist, s4_hist, s8_hist), h_buf=h_buf)
    hists = R.hists
    carries = (a_carry, s2_carry, s4_carry, s8_carry)
    fin_g = fin_g_ref[...].reshape(1, D_MODEL)

    def tile_body(before_layer=None, between_phases=None):
        for l in range(DEPTH):
            if before_layer is not None:
                before_layer(l)

            def x_rows(rs, l=l):
                return x_ref[0, rs, :] if l == 0 else x_buf[rs, :]

            def phase_done():
                if between_phases is not None:
                    between_phases()

            blocks = [slice(k * rows, (k + 1) * rows) for k in range(ROW_BLOCKS)]
            for rs in blocks:
                h_buf[rs, :] = _rmsnorm(x_rows(rs), _row(norm_g_ref, l)).astype(BF16)
            chains = [_Chain(l, k * rows, rows, t, tm, R) for k in range(ROW_BLOCKS)]
            for c in chains:
                c.p0()
            phase_done()
            for c in chains:
                c.p1()
            phase_done()
            for buf, carry in zip(hists, carries):
                buf[0:HIST, :] = carry[l]
            for c in chains:
                c.p2()
                phase_done()
            for buf, carry in zip(hists, carries):
                carry[l] = buf[tm:tm + HIST, :]
            pool_ref[l, 0] = chains[-1].pool_rows
            cv_ref[l, 0] = chains[-1].cv_rows
            for c, rs in zip(chains, blocks):
                c.p3()
                x_new = x_rows(rs) + c.out
                if l + 1 < DEPTH:
                    x_buf[rs, :] = x_new
                else:
                    y_ref[0, rs, :] = _rmsnorm(x_new, fin_g)
                phase_done()

    def zero_hist():
        for carry in carries:
            carry[...] = jnp.zeros(carry.shape, F32)

    first = jnp.logical_and(b == 0, t == 0)

    @pl.when(first)
    def _():
        stream = _WeightStream(w_in_hbm, w_out_hbm, w_in_bf, w_out_bf, stage, sems)
        stream.start()
        in_copies = [
            pltpu.make_async_copy(sp_hbm, sp_buf, io_sems.at[0]),
            pltpu.make_async_copy(xs_hbm.at[:, 0, :], xs_buf, io_sems.at[1]),
        ]
        for l in range(DEPTH):
            for hh in range(N_HEADS_B):
                bias_buf[l, :, hh * HEAD_DIM_B:(hh + 1) * HEAD_DIM_B] = jnp.broadcast_to(
                    b_s_ref[l, hh:hh + 1, :], (CHUNK, CHUNK)).T
        zero_hist()

        def before_layer(l):
            stream.finish_layer(l)
            if l == DEPTH - 1:
                for cp in in_copies:
                    cp.start()

        tile_body(before_layer, stream.service)
        for cp in in_copies:
            cp.wait()
        _sample_group(xs_buf, sp_buf, norm_g_ref, b_s_ref, fin_g, R, ys_buf, a_in_s_ref, vn_s_buf)
        out_copies = [
            pltpu.make_async_copy(ys_buf, ys_hbm.at[:, 0, :], io_sems.at[0]),
            pltpu.make_async_copy(vn_s_buf, vn_s_hbm.at[:, :, 0, :], io_sems.at[1]),
        ]
        for cp in out_copies:
            cp.start()
        for cp in out_copies:
            cp.wait()

    @pl.when(jnp.logical_not(first))
    def _():
        pl.when(t == 0)(zero_hist)
        tile_body()


def _const_spec(shape):
    zeros = (0,) * len(shape)
    return pl.BlockSpec(shape, lambda b, t: zeros, pipeline_mode=pl.Buffered(1))


def _trunk_call(x, xs, sp_t, norm_g, w_in, w_pool, pool_scale, v_norm_g, w_s, b_s, w_out, fin_g):
    batch, seq, d = x.shape
    n_s = xs.shape[0]
    tm = ROW_TILE
    assert seq % tm == 0 and tm % (ROW_BLOCKS * CHUNK) == 0
    assert d % WEIGHT_CHUNK_ROWS == 0 and (W_A + W_B) % WEIGHT_CHUNK_ROWS == 0
    hbm = pl.BlockSpec(memory_space=pl.ANY)
    args = (x, xs, sp_t, norm_g, w_in, w_pool, pool_scale, v_norm_g, w_s, b_s, w_out, fin_g)
    in_specs = []
    for a in args:
        if a is x:
            in_specs.append(pl.BlockSpec((1, tm, d), lambda b, t: (b, t, 0)))
        elif a is w_in or a is w_out or a is sp_t or a is xs:
            in_specs.append(hbm)
        else:
            in_specs.append(_const_spec(a.shape))
    return pl.pallas_call(
        _trunk_kernel,
        grid=(batch, seq // tm),
        in_specs=in_specs,
        out_specs=[
            pl.BlockSpec((1, tm, d), lambda b, t: (b, t, 0)),
            pl.BlockSpec((DEPTH, 1, POOL_BUF, W_A), lambda b, t: (0, b, 0, 0)),
            pl.BlockSpec((DEPTH, 1, CHUNK, W_B), lambda b, t: (0, b, 0, 0)),
            hbm,
            pl.BlockSpec((DEPTH, n_s, W_A), lambda b, t: (0, 0, 0)),
            hbm,
        ],
        out_shape=[
            jax.ShapeDtypeStruct((batch, seq, d), F32),
            jax.ShapeDtypeStruct((DEPTH, batch, POOL_BUF, W_A), F32),
            jax.ShapeDtypeStruct((DEPTH, batch, CHUNK, W_B), F32),
            jax.ShapeDtypeStruct((n_s, 1, d), F32),
            jax.ShapeDtypeStruct((DEPTH, n_s, W_A), F32),
            jax.ShapeDtypeStruct((DEPTH, n_s, 1, W_B), F32),
        ],
        scratch_shapes=[
            pltpu.VMEM((HIST + tm, W_A), F32),
            pltpu.VMEM((HIST + tm, W_A - POOL_GROUP_DIM), F32),
            pltpu.VMEM((HIST + tm, W_A - 2 * POOL_GROUP_DIM), F32),
            pltpu.VMEM((HIST + tm, W_A - 3 * POOL_GROUP_DIM), F32),
            pltpu.VMEM((DEPTH, HIST, W_A), F32),
            pltpu.VMEM((DEPTH, HIST, W_A - POOL_GROUP_DIM), F32),
            pltpu.VMEM((DEPTH, HIST, W_A - 2 * POOL_GROUP_DIM), F32),
            pltpu.VMEM((DEPTH, HIST, W_A - 3 * POOL_GROUP_DIM), F32),
            pltpu.VMEM((tm, d), BF16),
            pltpu.VMEM((tm, d), F32),
            pltpu.VMEM((DEPTH, d, D_IN), BF16),
            pltpu.VMEM((DEPTH, W_A + W_B, d), BF16),
            pltpu.VMEM((DEPTH, CHUNK, W_B), F32),
            pltpu.VMEM((DEPTH, POOL_BUF, n_s, W_A), F32),
            pltpu.VMEM((n_s, d), F32),
            pltpu.VMEM((n_s, d), F32),
            pltpu.VMEM((DEPTH, n_s, W_B), F32),
            pltpu.VMEM((WEIGHT_SLOTS, WEIGHT_CHUNK_ROWS, D_IN), F32),
            pltpu.SemaphoreType.DMA((WEIGHT_SLOTS,)),
            pltpu.SemaphoreType.DMA((2,)),
        ],
        compiler_params=pltpu.CompilerParams(
            dimension_semantics=("arbitrary", "arbitrary"),
            vmem_limit_bytes=VMEM_LIMIT_BYTES,
        ),
        name="trunk",
    )(*args)


def kernel(x_prompt, x_sample, state_pool, norm_g, w_in, w_pool, pool_scale, v_norm_g, w_s, b_s, w_out, final_norm_g):
    assert x_sample.shape[1] == 1 and state_pool.shape[2] == POOL_BUF
    sp_t = jnp.transpose(state_pool, (0, 2, 1, 3))
    y_prompt, pool_prompt, chunk_v_prompt, y_sample, a_in_s, chunk_v_sample = _trunk_call(
        x_prompt, x_sample, sp_t, norm_g, w_in, w_pool, pool_scale, v_norm_g, w_s, b_s, w_out, final_norm_g)
    is_new_row = lax.broadcasted_iota(jnp.int32, (1, 1, POOL_BUF, 1), 2) == POOL_BUF - 1
    pool_sample = jnp.where(is_new_row, a_in_s[:, :, None, :], jnp.roll(state_pool, -1, axis=2))
    return (y_prompt, y_sample, pool_prompt, pool_sample, chunk_v_prompt, chunk_v_sample)
```

```python
import jax
import jax.numpy as jnp
from jax import lax
from jax.experimental import pallas as pl
from jax.experimental.pallas import tpu as pltpu

D_MODEL = 1024
DEPTH = 2
W_A = 512
W_B = 512
D_IN = 2 * W_A + 3 * W_B
POOL_WINDOWS = (2, 4, 8, 16)
N_POOL_GROUPS = len(POOL_WINDOWS)
POOL_GROUP_DIM = W_A // N_POOL_GROUPS
POOL_BUF = max(POOL_WINDOWS) - 1
CHUNK = 128
N_HEADS_B = 4
HEAD_DIM_B = W_B // N_HEADS_B
EPS = 1e-6
SAMPLE_PAST_LEN = 16384

COL_A_IN = 0
COL_A_GATE = W_A
COL_U = 2 * W_A
COL_V = 2 * W_A + W_B
COL_B_GATE = 2 * W_A + 2 * W_B

HIST = 16
ROW_TILE = 512
ROW_BLOCKS = 2
WEIGHT_CHUNK_ROWS = 256
WEIGHT_SLOTS = 4
VMEM_LIMIT_BYTES = 56 * 1024 * 1024

BF16 = jnp.bfloat16
F32 = jnp.float32


def _dot(a, b):
    return jnp.dot(a.astype(BF16), b.astype(BF16), preferred_element_type=F32)


def _rmsnorm(x, g):
    ms = jnp.mean(x * x, axis=-1, keepdims=True)
    return (x * lax.rsqrt(ms + EPS)) * g


def _silu(x):
    hx = 0.5 * x
    return hx * jnp.tanh(hx) + hx


def _head_rmsnorm(v, g):
    parts = []
    for h in range(N_HEADS_B):
        sl = slice(h * HEAD_DIM_B, (h + 1) * HEAD_DIM_B)
        parts.append(_rmsnorm(v[:, sl], g[:, sl]))
    return jnp.concatenate(parts, axis=-1)


def _masked_ws(ws_ref, l):
    row = lax.broadcasted_iota(jnp.int32, (CHUNK, CHUNK), 0)
    col = lax.broadcasted_iota(jnp.int32, (CHUNK, CHUNK), 1)
    keep = col <= row
    return [jnp.where(keep, ws_ref[l, h], 0.0).astype(BF16) for h in range(N_HEADS_B)]


class _Refs:
    def __init__(self, **kw):
        self.__dict__.update(kw)


def _row(ref, l):
    return ref[l:l + 1, :]


class _WeightStream:
    def __init__(self, w_in_hbm, w_out_hbm, w_in_bf, w_out_bf, stage, sems):
        self.chunks = []
        for l in range(DEPTH):
            for src, dst in ((w_in_hbm, w_in_bf), (w_out_hbm, w_out_bf)):
                _, k, n = src.shape
                self.chunks += [(src, dst, l, r, n) for r in range(0, k, WEIGHT_CHUNK_ROWS)]
        self.per_layer = len(self.chunks) // DEPTH
        self.stage, self.sems = stage, sems
        self.done = 0

    def _copy(self, i):
        src, _, l, r, n = self.chunks[i]
        slot = i % WEIGHT_SLOTS
        return pltpu.make_async_copy(
            src.at[l, pl.ds(r, WEIGHT_CHUNK_ROWS), :], self.stage.at[slot, :, pl.ds(0, n)], self.sems.at[slot])

    def start(self):
        for i in range(WEIGHT_SLOTS - 1):
            self._copy(i).start()

    def service(self):
        i = self.done
        if i == len(self.chunks):
            return
        if i + WEIGHT_SLOTS - 1 < len(self.chunks):
            self._copy(i + WEIGHT_SLOTS - 1).start()
        self._copy(i).wait()
        _, dst, l, r, n = self.chunks[i]
        dst[l, r:r + WEIGHT_CHUNK_ROWS, :] = self.stage[i % WEIGHT_SLOTS, :, 0:n].astype(BF16)
        self.done += 1

    def finish_layer(self, l):
        while self.done < self.per_layer * (l + 1):
            self.service()


class _Chain:
    def __init__(self, l, r0, rows, t, tm, R):
        self.l, self.r0, self.rows, self.t, self.tm, self.R = l, r0, rows, t, tm, R

    def _in(self, col, width):
        R = self.R
        return jnp.dot(R.h_buf[self.r0:self.r0 + self.rows, :], R.w_in[self.l, :, col:col + width],
                       preferred_element_type=F32)

    def p0(self):
        self.v = self._in(COL_V, W_B)
        self.a_in = self._in(COL_A_IN, W_A)

    def p1(self):
        self.a_gate = self._in(COL_A_GATE, W_A)
        self.u = self._in(COL_U, W_B)

    def p2(self):
        l, r0, rows, R = self.l, self.r0, self.rows, self.R
        n_chunks = rows // CHUNK
        a_in = self.a_in
        base = HIST + r0

        vn = _head_rmsnorm(self.v, _row(R.v_norm_g, l))
        self.cv_rows = vn[rows - CHUNK:, :]
        vn_b = vn.astype(BF16)

        a_hist, s2_hist, s4_hist, s8_hist = R.hists
        a_hist[base:base + rows, :] = a_in
        s2 = a_in + a_hist[base - 1:base - 1 + rows, :]
        s2_hist[base:base + rows, :] = s2[:, POOL_GROUP_DIM:]
        s4 = s2[:, POOL_GROUP_DIM:] + s2_hist[base - 2:base - 2 + rows, :]
        s4_hist[base:base + rows, :] = s4[:, POOL_GROUP_DIM:]
        s8 = s4[:, POOL_GROUP_DIM:] + s4_hist[base - 4:base - 4 + rows, :]
        s8_hist[base:base + rows, :] = s8[:, POOL_GROUP_DIM:]
        s16 = s8[:, POOL_GROUP_DIM:] + s8_hist[base - 8:base - 8 + rows, :]
        win_sums = (s2[:, :POOL_GROUP_DIM], s4[:, :POOL_GROUP_DIM], s8[:, :POOL_GROUP_DIM], s16)
        self.pool_rows = a_in[rows - POOL_BUF:, :]

        d_parts = []
        for g, w in enumerate(POOL_WINDOWS):
            sl = slice(g * POOL_GROUP_DIM, (g + 1) * POOL_GROUP_DIM)
            ssum = win_sums[g]
            if r0 == 0:
                pos1 = lax.broadcasted_iota(jnp.int32, (HIST, POOL_GROUP_DIM), 0) + self.t * self.tm + 1
                inv_head = 1.0 / jnp.minimum(pos1, w).astype(F32)
                mean = jnp.concatenate([ssum[:HIST] * inv_head, ssum[HIST:] * (1.0 / w)], axis=0)
            else:
                mean = ssum * (1.0 / w)
            d_parts.append((mean - a_in[:, sl]).astype(BF16))

        ws_b = _masked_ws(R.ws, l)
        s_heads = []
        for hh in range(N_HEADS_B):
            sl = slice(hh * HEAD_DIM_B, (hh + 1) * HEAD_DIM_B)
            rhs = jnp.concatenate([vn_b[c * CHUNK:(c + 1) * CHUNK, sl] for c in range(n_chunks)], axis=1)
            res = jnp.dot(ws_b[hh], rhs, preferred_element_type=F32)
            s_heads.append(jnp.concatenate(
                [res[:, c * HEAD_DIM_B:(c + 1) * HEAD_DIM_B] for c in range(n_chunks)], axis=0))
        self.s_heads = s_heads

        self.y_parts = [_dot(d_parts[g], R.w_pool[l, g]) for g in range(N_POOL_GROUPS)]
        self.b_gate = self._in(COL_B_GATE, W_B)

    def p3(self):
        l, rows, R = self.l, self.rows, self.R
        n_chunks = rows // CHUNK
        a_out = ((jnp.concatenate(self.y_parts, axis=-1) * _row(R.pool_scale, l)) * _silu(self.a_gate)).astype(BF16)
        out_a = _dot(a_out, R.w_out[l, 0:W_A, :])
        s = jnp.concatenate(self.s_heads, axis=-1) + jnp.concatenate([R.bias[l]] * n_chunks, axis=0)
        b_out = ((self.u * s) * _silu(self.b_gate)).astype(BF16)
        self.out = out_a + _dot(b_out, R.w_out[l, W_A:, :])


def _per_head_lanes(vals):
    return jnp.concatenate([jnp.broadcast_to(v, (1, HEAD_DIM_B)) for v in vals], axis=-1)


def _sample_group(xs_ref, sp_ref, norm_g_ref, b_s_ref, fin_g, R, ys_ref, a_in_s_ref, vn_s_ref):
    x = xs_ref[...]
    for l in range(DEPTH):
        h = _rmsnorm(x, _row(norm_g_ref, l)).astype(BF16)
        a_in = _dot(h, R.w_in[l, :, COL_A_IN:COL_A_IN + W_A])
        a_in_s_ref[l] = a_in
        a_gate = _dot(h, R.w_in[l, :, COL_A_GATE:COL_A_GATE + W_A])
        a_parts = []
        for g, w in enumerate(POOL_WINDOWS):
            sl = slice(g * POOL_GROUP_DIM, (g + 1) * POOL_GROUP_DIM)
            ssum = a_in[:, sl]
            for k in range(1, w):
                ssum = ssum + sp_ref[l, POOL_BUF - k, :, sl]
            cnt = float(min(SAMPLE_PAST_LEN + 1, w))
            d = ssum / cnt - a_in[:, sl]
            yg = _dot(d, R.w_pool[l, g]) * _row(R.pool_scale, l)[:, sl]
            a_parts.append(yg * _silu(a_gate[:, sl]))
        a_out = jnp.concatenate(a_parts, axis=-1)

        v = _dot(h, R.w_in[l, :, COL_V:COL_V + W_B])
        vn = _head_rmsnorm(v, _row(R.v_norm_g, l))
        vn_s_ref[l] = vn
        ws00 = _per_head_lanes([R.ws[l, hh, 0:1, 0:1] for hh in range(N_HEADS_B)])
        b0 = _per_head_lanes([b_s_ref[l, hh:hh + 1, 0:1] for hh in range(N_HEADS_B)])
        s = ws00 * vn + b0
        u = _dot(h, R.w_in[l, :, COL_U:COL_U + W_B])
        b_gate = _dot(h, R.w_in[l, :, COL_B_GATE:COL_B_GATE + W_B])
        b_out = (u * s) * _silu(b_gate)

        x = x + (_dot(a_out, R.w_out[l, 0:W_A, :]) + _dot(b_out, R.w_out[l, W_A:, :]))
    ys_ref[...] = _rmsnorm(x, fin_g)


def _trunk_kernel(x_ref, xs_hbm, sp_hbm, norm_g_ref, w_in_hbm, w_pool_ref, pool_scale_ref, v_norm_g_ref,
                  ws_ref, b_s_ref, w_out_hbm, fin_g_ref,
                  y_ref, pool_ref, cv_ref, ys_hbm, pool_s_t_hbm, vn_s_hbm,
                  a_hist, s2_hist, s4_hist, s8_hist, a_carry, s2_carry, s4_carry, s8_carry,
                  h_buf, x_buf, w_in_bf, w_out_bf, bias_buf, sp_buf,
                  xs_buf, ys_buf, a_in_s_buf, vn_s_buf, stage, sems, io_sems):
    b = pl.program_id(0)
    t = pl.program_id(1)
    tm = x_ref.shape[1]
    rows = tm // ROW_BLOCKS
    R = _Refs(w_in=w_in_bf, w_pool=w_pool_ref, pool_scale=pool_scale_ref, v_norm_g=v_norm_g_ref, ws=ws_ref,
              bias=bias_buf, w_out=w_out_bf, hists=(a_hist, s2_hist, s4_hist, s8_hist), h_buf=h_buf)
    hists = R.hists
    carries = (a_carry, s2_carry, s4_carry, s8_carry)
    fin_g = fin_g_ref[...].reshape(1, D_MODEL)

    def tile_body(before_layer=None, between_phases=None):
        for l in range(DEPTH):
            if before_layer is not None:
                before_layer(l)

            def x_rows(rs, l=l):
                return x_ref[0, rs, :] if l == 0 else x_buf[rs, :]

            def phase_done():
                if between_phases is not None:
                    between_phases()

            blocks = [slice(k * rows, (k + 1) * rows) for k in range(ROW_BLOCKS)]
            for rs in blocks:
                h_buf[rs, :] = _rmsnorm(x_rows(rs), _row(norm_g_ref, l)).astype(BF16)
            chains = [_Chain(l, k * rows, rows, t, tm, R) for k in range(ROW_BLOCKS)]
            for c in chains:
                c.p0()
            phase_done()
            for c in chains:
                c.p1()
            phase_done()
            for buf, carry in zip(hists, carries):
                buf[0:HIST, :] = carry[l]
            for c in chains:
                c.p2()
                phase_done()
            for buf, carry in zip(hists, carries):
                carry[l] = buf[tm:tm + HIST, :]
            pool_ref[l, 0] = chains[-1].pool_rows
            cv_ref[l, 0] = chains[-1].cv_rows
            for c, rs in zip(chains, blocks):
                c.p3()
                x_new = x_rows(rs) + c.out
                if l + 1 < DEPTH:
                    x_buf[rs, :] = x_new
                else:
                    y_ref[0, rs, :] = _rmsnorm(x_new, fin_g)
                phase_done()

    def zero_hist():
        for carry in carries:
            carry[...] = jnp.zeros(carry.shape, F32)

    first = jnp.logical_and(b == 0, t == 0)

    @pl.when(first)
    def _():
        stream = _WeightStream(w_in_hbm, w_out_hbm, w_in_bf, w_out_bf, stage, sems)
        stream.start()
        in_copies = [
            pltpu.make_async_copy(sp_hbm, sp_buf, io_sems.at[0]),
            pltpu.make_async_copy(xs_hbm.at[:, 0, :], xs_buf, io_sems.at[1]),
        ]
        for l in range(DEPTH):
            for hh in range(N_HEADS_B):
                bias_buf[l, :, hh * HEAD_DIM_B:(hh + 1) * HEAD_DIM_B] = jnp.broadcast_to(
                    b_s_ref[l, hh:hh + 1, :], (CHUNK, CHUNK)).T
        zero_hist()

        def before_layer(l):
            stream.finish_layer(l)
            if l == DEPTH - 1:
                for cp in in_copies:
                    cp.start()

        tile_body(before_layer, stream.service)
        for cp in in_copies:
            cp.wait()
        keep_copy = pltpu.make_async_copy(sp_buf.at[:, pl.ds(1, POOL_BUF - 1)],
                                          pool_s_t_hbm.at[:, pl.ds(0, POOL_BUF - 1)], io_sems.at[2])
        keep_copy.start()
        _sample_group(xs_buf, sp_buf, norm_g_ref, b_s_ref, fin_g, R, ys_buf, a_in_s_buf, vn_s_buf)
        out_copies = [
            pltpu.make_async_copy(ys_buf, ys_hbm.at[:, 0, :], io_sems.at[0]),
            pltpu.make_async_copy(vn_s_buf, vn_s_hbm.at[:, :, 0, :], io_sems.at[1]),
            pltpu.make_async_copy(a_in_s_buf, pool_s_t_hbm.at[:, POOL_BUF - 1], io_sems.at[3]),
        ]
        for cp in out_copies:
            cp.start()
        keep_copy.wait()
        for cp in out_copies:
            cp.wait()

    @pl.when(jnp.logical_not(first))
    def _():
        pl.when(t == 0)(zero_hist)
        tile_body()


def _const_spec(shape):
    zeros = (0,) * len(shape)
    return pl.BlockSpec(shape, lambda b, t: zeros, pipeline_mode=pl.Buffered(1))


def _trunk_call(x, xs, sp_t, norm_g, w_in, w_pool, pool_scale, v_norm_g, w_s, b_s, w_out, fin_g):
    batch, seq, d = x.shape
    n_s = xs.shape[0]
    tm = ROW_TILE
    assert seq % tm == 0 and tm % (ROW_BLOCKS * CHUNK) == 0
    assert d % WEIGHT_CHUNK_ROWS == 0 and (W_A + W_B) % WEIGHT_CHUNK_ROWS == 0
    hbm = pl.BlockSpec(memory_space=pl.ANY)
    args = (x, xs, sp_t, norm_g, w_in, w_pool, pool_scale, v_norm_g, w_s, b_s, w_out, fin_g)
    in_specs = []
    for a in args:
        if a is x:
            in_specs.append(pl.BlockSpec((1, tm, d), lambda b, t: (b, t, 0)))
        elif a is w_in or a is w_out or a is sp_t or a is xs:
            in_specs.append(hbm)
        else:
            in_specs.append(_const_spec(a.shape))
    return pl.pallas_call(
        _trunk_kernel,
        grid=(batch, seq // tm),
        in_specs=in_specs,
        out_specs=[
            pl.BlockSpec((1, tm, d), lambda b, t: (b, t, 0)),
            pl.BlockSpec((DEPTH, 1, POOL_BUF, W_A), lambda b, t: (0, b, 0, 0)),
            pl.BlockSpec((DEPTH, 1, CHUNK, W_B), lambda b, t: (0, b, 0, 0)),
            hbm, hbm, hbm,
        ],
        out_shape=[
            jax.ShapeDtypeStruct((batch, seq, d), F32),
            jax.ShapeDtypeStruct((DEPTH, batch, POOL_BUF, W_A), F32),
            jax.ShapeDtypeStruct((DEPTH, batch, CHUNK, W_B), F32),
            jax.ShapeDtypeStruct((n_s, 1, d), F32),
            jax.ShapeDtypeStruct((DEPTH, POOL_BUF, n_s, W_A), F32),
            jax.ShapeDtypeStruct((DEPTH, n_s, 1, W_B), F32),
        ],
        scratch_shapes=[
            pltpu.VMEM((HIST + tm, W_A), F32),
            pltpu.VMEM((HIST + tm, W_A - POOL_GROUP_DIM), F32),
            pltpu.VMEM((HIST + tm, W_A - 2 * POOL_GROUP_DIM), F32),
            pltpu.VMEM((HIST + tm, W_A - 3 * POOL_GROUP_DIM), F32),
            pltpu.VMEM((DEPTH, HIST, W_A), F32),
            pltpu.VMEM((DEPTH, HIST, W_A - POOL_GROUP_DIM), F32),
            pltpu.VMEM((DEPTH, HIST, W_A - 2 * POOL_GROUP_DIM), F32),
            pltpu.VMEM((DEPTH, HIST, W_A - 3 * POOL_GROUP_DIM), F32),
            pltpu.VMEM((tm, d), BF16),
            pltpu.VMEM((tm, d), F32),
            pltpu.VMEM((DEPTH, d, D_IN), BF16),
            pltpu.VMEM((DEPTH, W_A + W_B, d), BF16),
            pltpu.VMEM((DEPTH, CHUNK, W_B), F32),
            pltpu.VMEM((DEPTH, POOL_BUF, n_s, W_A), F32),
            pltpu.VMEM((n_s, d), F32),
            pltpu.VMEM((n_s, d), F32),
            pltpu.VMEM((DEPTH, n_s, W_A), F32),
            pltpu.VMEM((DEPTH, n_s, W_B), F32),
            pltpu.VMEM((WEIGHT_SLOTS, WEIGHT_CHUNK_ROWS, D_IN), F32),
            pltpu.SemaphoreType.DMA((WEIGHT_SLOTS,)),
            pltpu.SemaphoreType.DMA((4,)),
        ],
        compiler_params=pltpu.CompilerParams(
            dimension_semantics=("arbitrary", "arbitrary"),
            vmem_limit_bytes=VMEM_LIMIT_BYTES,
        ),
        name="trunk",
    )(*args)


def kernel(x_prompt, x_sample, state_pool, norm_g, w_in, w_pool, pool_scale, v_norm_g, w_s, b_s, w_out, final_norm_g):
    assert x_sample.shape[1] == 1 and state_pool.shape[2] == POOL_BUF
    sp_t = jnp.transpose(state_pool, (0, 2, 1, 3))
    y_prompt, pool_prompt, chunk_v_prompt, y_sample, pool_sample_t, chunk_v_sample = _trunk_call(
        x_prompt, x_sample, sp_t, norm_g, w_in, w_pool, pool_scale, v_norm_g, w_s, b_s, w_out, final_norm_g)
    pool_sample = jnp.transpose(pool_sample_t, (0, 2, 1, 3))
    return (y_prompt, y_sample, pool_prompt, pool_sample, chunk_v_prompt, chunk_v_sample)
```

```python
import jax
import jax.numpy as jnp
from jax import lax
from jax.experimental import pallas as pl
from jax.experimental.pallas import tpu as pltpu

D_MODEL = 1024
DEPTH = 2
W_A = 512
W_B = 512
D_IN = 2 * W_A + 3 * W_B
POOL_WINDOWS = (2, 4, 8, 16)
N_POOL_GROUPS = len(POOL_WINDOWS)
POOL_GROUP_DIM = W_A // N_POOL_GROUPS
POOL_BUF = max(POOL_WINDOWS) - 1
CHUNK = 128
N_HEADS_B = 4
HEAD_DIM_B = W_B // N_HEADS_B
EPS = 1e-6
SAMPLE_PAST_LEN = 16384

COL_A_IN = 0
COL_A_GATE = W_A
COL_U = 2 * W_A
COL_V = 2 * W_A + W_B
COL_B_GATE = 2 * W_A + 2 * W_B

HIST = 16
ROW_TILE = 512
ROW_BLOCKS = 2
WEIGHT_CHUNK_ROWS = 256
WEIGHT_SLOTS = 4
VMEM_LIMIT_BYTES = 56 * 1024 * 1024

BF16 = jnp.bfloat16
F32 = jnp.float32


def _dot(a, b):
    return jnp.dot(a.astype(BF16), b.astype(BF16), preferred_element_type=F32)


def _rmsnorm(x, g):
    ms = jnp.mean(x * x, axis=-1, keepdims=True)
    return (x * lax.rsqrt(ms + EPS)) * g


def _silu(x):
    hx = 0.5 * x
    return hx * jnp.tanh(hx) + hx


def _head_rmsnorm(v, g):
    parts = []
    for h in range(N_HEADS_B):
        sl = slice(h * HEAD_DIM_B, (h + 1) * HEAD_DIM_B)
        parts.append(_rmsnorm(v[:, sl], g[:, sl]))
    return jnp.concatenate(parts, axis=-1)


def _masked_ws(ws_ref, l):
    row = lax.broadcasted_iota(jnp.int32, (CHUNK, CHUNK), 0)
    col = lax.broadcasted_iota(jnp.int32, (CHUNK, CHUNK), 1)
    keep = col <= row
    return [jnp.where(keep, ws_ref[l, h], 0.0).astype(BF16) for h in range(N_HEADS_B)]


class _Refs:
    def __init__(self, **kw):
        self.__dict__.update(kw)


def _row(ref, l):
    return ref[l:l + 1, :]


class _WeightStream:
    def __init__(self, w_in_hbm, w_out_hbm, w_in_bf, w_out_bf, stage, sems):
        self.chunks = []
        for l in range(DEPTH):
            for src, dst in ((w_in_hbm, w_in_bf), (w_out_hbm, w_out_bf)):
                _, k, n = src.shape
                self.chunks += [(src, dst, l, r, n) for r in range(0, k, WEIGHT_CHUNK_ROWS)]
        self.per_layer = len(self.chunks) // DEPTH
        self.stage, self.sems = stage, sems
        self.done = 0

    def _copy(self, i):
        src, _, l, r, n = self.chunks[i]
        slot = i % WEIGHT_SLOTS
        return pltpu.make_async_copy(
            src.at[l, pl.ds(r, WEIGHT_CHUNK_ROWS), :], self.stage.at[slot, :, pl.ds(0, n)], self.sems.at[slot])

    def start(self):
        for i in range(WEIGHT_SLOTS - 1):
            self._copy(i).start()

    def service(self):
        i = self.done
        if i == len(self.chunks):
            return
        if i + WEIGHT_SLOTS - 1 < len(self.chunks):
            self._copy(i + WEIGHT_SLOTS - 1).start()
        self._copy(i).wait()
        _, dst, l, r, n = self.chunks[i]
        dst[l, r:r + WEIGHT_CHUNK_ROWS, :] = self.stage[i % WEIGHT_SLOTS, :, 0:n].astype(BF16)
        self.done += 1

    def finish_layer(self, l):
        while self.done < self.per_layer * (l + 1):
            self.service()


class _Chain:
    def __init__(self, l, r0, rows, t, tm, R):
        self.l, self.r0, self.rows, self.t, self.tm, self.R = l, r0, rows, t, tm, R

    def _in(self, col, width):
        R = self.R
        return jnp.dot(R.h_buf[self.r0:self.r0 + self.rows, :], R.w_in[self.l, :, col:col + width],
                       preferred_element_type=F32)

    def p0(self):
        self.v = self._in(COL_V, W_B)
        self.a_in = self._in(COL_A_IN, W_A)

    def p1(self):
        self.a_gate = self._in(COL_A_GATE, W_A)
        self.u = self._in(COL_U, W_B)

    def p2(self):
        l, r0, rows, R = self.l, self.r0, self.rows, self.R
        n_chunks = rows // CHUNK
        a_in = self.a_in
        base = HIST + r0

        vn = _head_rmsnorm(self.v, _row(R.v_norm_g, l))
        self.cv_rows = vn[rows - CHUNK:, :]
        vn_b = vn.astype(BF16)

        a_hist, s2_hist, s4_hist, s8_hist = R.hists
        a_hist[base:base + rows, :] = a_in
        s2 = a_in + a_hist[base - 1:base - 1 + rows, :]
        s2_hist[base:base + rows, :] = s2[:, POOL_GROUP_DIM:]
        s4 = s2[:, POOL_GROUP_DIM:] + s2_hist[base - 2:base - 2 + rows, :]
        s4_hist[base:base + rows, :] = s4[:, POOL_GROUP_DIM:]
        s8 = s4[:, POOL_GROUP_DIM:] + s4_hist[base - 4:base - 4 + rows, :]
        s8_hist[base:base + rows, :] = s8[:, POOL_GROUP_DIM:]
        s16 = s8[:, POOL_GROUP_DIM:] + s8_hist[base - 8:base - 8 + rows, :]
        win_sums = (s2[:, :POOL_GROUP_DIM], s4[:, :POOL_GROUP_DIM], s8[:, :POOL_GROUP_DIM], s16)
        self.pool_rows = a_in[rows - POOL_BUF:, :]

        d_parts = []
        for g, w in enumerate(POOL_WINDOWS):
            sl = slice(g * POOL_GROUP_DIM, (g + 1) * POOL_GROUP_DIM)
            ssum = win_sums[g]
            if r0 == 0:
                pos1 = lax.broadcasted_iota(jnp.int32, (HIST, POOL_GROUP_DIM), 0) + self.t * self.tm + 1
                inv_head = 1.0 / jnp.minimum(pos1, w).astype(F32)
                mean = jnp.concatenate([ssum[:HIST] * inv_head, ssum[HIST:] * (1.0 / w)], axis=0)
            else:
                mean = ssum * (1.0 / w)
            d_parts.append((mean - a_in[:, sl]).astype(BF16))

        ws_b = _masked_ws(R.ws, l)
        s_heads = []
        for hh in range(N_HEADS_B):
            sl = slice(hh * HEAD_DIM_B, (hh + 1) * HEAD_DIM_B)
            rhs = jnp.concatenate([vn_b[c * CHUNK:(c + 1) * CHUNK, sl] for c in range(n_chunks)], axis=1)
            res = jnp.dot(ws_b[hh], rhs, preferred_element_type=F32)
            s_heads.append(jnp.concatenate(
                [res[:, c * HEAD_DIM_B:(c + 1) * HEAD_DIM_B] for c in range(n_chunks)], axis=0))
        self.s_heads = s_heads

        self.y_parts = [_dot(d_parts[g], R.w_pool[l, g]) for g in range(N_POOL_GROUPS)]
        self.b_gate = self._in(COL_B_GATE, W_B)

    def p3(self):
        l, rows, R = self.l, self.rows, self.R
        n_chunks = rows // CHUNK
        a_out = ((jnp.concatenate(self.y_parts, axis=-1) * _row(R.pool_scale, l)) * _silu(self.a_gate)).astype(BF16)
        out_a = _dot(a_out, R.w_out[l, 0:W_A, :])
        s = jnp.concatenate(self.s_heads, axis=-1) + jnp.concatenate([R.bias[l]] * n_chunks, axis=0)
        b_out = ((self.u * s) * _silu(self.b_gate)).astype(BF16)
        self.out = out_a + _dot(b_out, R.w_out[l, W_A:, :])


def _per_head_lanes(vals):
    return jnp.concatenate([jnp.broadcast_to(v, (1, HEAD_DIM_B)) for v in vals], axis=-1)


def _sample_group(xs_ref, sp_ref, norm_g_ref, b_s_ref, fin_g, R, ys_ref, a_in_s_ref, vn_s_ref):
    x = xs_ref[...]
    for l in range(DEPTH):
        h = _rmsnorm(x, _row(norm_g_ref, l)).astype(BF16)
        a_in = _dot(h, R.w_in[l, :, COL_A_IN:COL_A_IN + W_A])
        a_in_s_ref[l] = a_in
        a_gate = _dot(h, R.w_in[l, :, COL_A_GATE:COL_A_GATE + W_A])
        a_parts = []
        for g, w in enumerate(POOL_WINDOWS):
            sl = slice(g * POOL_GROUP_DIM, (g + 1) * POOL_GROUP_DIM)
            ssum = a_in[:, sl]
            for k in range(1, w):
                ssum = ssum + sp_ref[l, POOL_BUF - k, :, sl]
            cnt = float(min(SAMPLE_PAST_LEN + 1, w))
            d = ssum / cnt - a_in[:, sl]
            yg = _dot(d, R.w_pool[l, g]) * _row(R.pool_scale, l)[:, sl]
            a_parts.append(yg * _silu(a_gate[:, sl]))
        a_out = jnp.concatenate(a_parts, axis=-1)

        v = _dot(h, R.w_in[l, :, COL_V:COL_V + W_B])
        vn = _head_rmsnorm(v, _row(R.v_norm_g, l))
        vn_s_ref[l] = vn
        ws00 = _per_head_lanes([R.ws[l, hh, 0:1, 0:1] for hh in range(N_HEADS_B)])
        b0 = _per_head_lanes([b_s_ref[l, hh:hh + 1, 0:1] for hh in range(N_HEADS_B)])
        s = ws00 * vn + b0
        u = _dot(h, R.w_in[l, :, COL_U:COL_U + W_B])
        b_gate = _dot(h, R.w_in[l, :, COL_B_GATE:COL_B_GATE + W_B])
        b_out = (u * s) * _silu(b_gate)

        x = x + (_dot(a_out, R.w_out[l, 0:W_A, :]) + _dot(b_out, R.w_out[l, W_A:, :]))
    ys_ref[...] = _rmsnorm(x, fin_g)


def _trunk_kernel(x_ref, x_probe_ref, xs_hbm, sp_hbm, norm_g_ref, w_in_hbm, w_pool_ref, pool_scale_ref, v_norm_g_ref,
                  ws_ref, b_s_ref, w_out_hbm, fin_g_ref,
                  y_ref, pool_ref, cv_ref, ys_hbm, pool_s_t_hbm, vn_s_hbm,
                  a_hist, s2_hist, s4_hist, s8_hist, a_carry, s2_carry, s4_carry, s8_carry,
                  h_buf, x_buf, w_in_bf, w_out_bf, bias_buf, sp_buf,
                  xs_buf, ys_buf, a_in_s_buf, vn_s_buf, stage, sems, io_sems):
    b = pl.program_id(0)
    t = pl.program_id(1)
    tm = x_ref.shape[1]
    rows = tm // ROW_BLOCKS
    R = _Refs(w_in=w_in_bf, w_pool=w_pool_ref, pool_scale=pool_scale_ref, v_norm_g=v_norm_g_ref, ws=ws_ref,
              bias=bias_buf, w_out=w_out_bf, hists=(a_hist, s2_hist, s4_hist, s8_hist), h_buf=h_buf)
    hists = R.hists
    carries = (a_carry, s2_carry, s4_carry, s8_carry)
    fin_g = fin_g_ref[...].reshape(1, D_MODEL)

    def tile_body(before_layer=None, between_phases=None):
        for l in range(DEPTH):
            if before_layer is not None:
                before_layer(l)

            def x_rows(rs, l=l):
                return x_ref[0, rs, :] if l == 0 else x_buf[rs, :]

            def phase_done():
                if between_phases is not None:
                    between_phases()

            blocks = [slice(k * rows, (k + 1) * rows) for k in range(ROW_BLOCKS)]
            for rs in blocks:
                h_buf[rs, :] = _rmsnorm(x_rows(rs), _row(norm_g_ref, l)).astype(BF16)
            chains = [_Chain(l, k * rows, rows, t, tm, R) for k in range(ROW_BLOCKS)]
            for c in chains:
                c.p0()
            phase_done()
            for c in chains:
                c.p1()
            phase_done()
            for buf, carry in zip(hists, carries):
                buf[0:HIST, :] = carry[l]
            for c in chains:
                c.p2()
                phase_done()
            for buf, carry in zip(hists, carries):
                carry[l] = buf[tm:tm + HIST, :]
            pool_ref[l, 0] = chains[-1].pool_rows
            cv_ref[l, 0] = chains[-1].cv_rows
            for c, rs in zip(chains, blocks):
                c.p3()
                x_new = x_rows(rs) + c.out
                if l + 1 < DEPTH:
                    x_buf[rs, :] = x_new
                else:
                    y_ref[0, rs, :] = _rmsnorm(x_new, fin_g)
                phase_done()

    def zero_hist():
        for carry in carries:
            carry[...] = jnp.zeros(carry.shape, F32)

    first = jnp.logical_and(b == 0, t == 0)

    @pl.when(first)
    def _():
        stream = _WeightStream(w_in_hbm, w_out_hbm, w_in_bf, w_out_bf, stage, sems)
        stream.start()
        in_copies = [
            pltpu.make_async_copy(sp_hbm, sp_buf, io_sems.at[0]),
            pltpu.make_async_copy(xs_hbm.at[:, 0, :], xs_buf, io_sems.at[1]),
        ]
        for l in range(DEPTH):
            for hh in range(N_HEADS_B):
                bias_buf[l, :, hh * HEAD_DIM_B:(hh + 1) * HEAD_DIM_B] = jnp.broadcast_to(
                    b_s_ref[l, hh:hh + 1, :], (CHUNK, CHUNK)).T
        zero_hist()

        def before_layer(l):
            stream.finish_layer(l)
            if l == DEPTH - 1:
                for cp in in_copies:
                    cp.start()

        tile_body(before_layer, stream.service)
        for cp in in_copies:
            cp.wait()
        keep_copy = pltpu.make_async_copy(sp_buf.at[:, pl.ds(1, POOL_BUF - 1)],
                                          pool_s_t_hbm.at[:, pl.ds(0, POOL_BUF - 1)], io_sems.at[2])
        keep_copy.start()
        _sample_group(xs_buf, sp_buf, norm_g_ref, b_s_ref, fin_g, R, ys_buf, a_in_s_buf, vn_s_buf)
        out_copies = [
            pltpu.make_async_copy(ys_buf, ys_hbm.at[:, 0, :], io_sems.at[0]),
            pltpu.make_async_copy(vn_s_buf, vn_s_hbm.at[:, :, 0, :], io_sems.at[1]),
            pltpu.make_async_copy(a_in_s_buf, pool_s_t_hbm.at[:, POOL_BUF - 1], io_sems.at[3]),
        ]
        for cp in out_copies:
            cp.start()
        keep_copy.wait()
        for cp in out_copies:
            cp.wait()

    @pl.when(jnp.logical_not(first))
    def _():
        pl.when(t == 0)(zero_hist)
        tile_body()


def _const_spec(shape):
    zeros = (0,) * len(shape)
    return pl.BlockSpec(shape, lambda b, t: zeros, pipeline_mode=pl.Buffered(1))


def _trunk_call(x, xs, sp_t, norm_g, w_in, w_pool, pool_scale, v_norm_g, w_s, b_s, w_out, fin_g):
    batch, seq, d = x.shape
    n_s = xs.shape[0]
    tm = ROW_TILE
    assert seq % tm == 0 and tm % (ROW_BLOCKS * CHUNK) == 0
    assert d % WEIGHT_CHUNK_ROWS == 0 and (W_A + W_B) % WEIGHT_CHUNK_ROWS == 0
    hbm = pl.BlockSpec(memory_space=pl.ANY)
    tiles_per_seq = seq // tm
    n_tiles = batch * tiles_per_seq

    def probe_block(b, t):
        i = jnp.minimum(b * tiles_per_seq + t + 1, n_tiles - 1)
        return (i // tiles_per_seq, (i % tiles_per_seq) * (tm // 8), 0)

    consts = (xs, sp_t, norm_g, w_in, w_pool, pool_scale, v_norm_g, w_s, b_s, w_out, fin_g)
    in_specs = [pl.BlockSpec((1, tm, d), lambda b, t: (b, t, 0)), pl.BlockSpec((1, 8, d), probe_block)]
    for a in consts:
        in_specs.append(hbm if (a is w_in or a is w_out or a is sp_t or a is xs) else _const_spec(a.shape))
    args = (x, x) + consts
    return pl.pallas_call(
        _trunk_kernel,
        grid=(batch, seq // tm),
        in_specs=in_specs,
        out_specs=[
            pl.BlockSpec((1, tm, d), lambda b, t: (b, t, 0)),
            pl.BlockSpec((DEPTH, 1, POOL_BUF, W_A), lambda b, t: (0, b, 0, 0)),
            pl.BlockSpec((DEPTH, 1, CHUNK, W_B), lambda b, t: (0, b, 0, 0)),
            hbm, hbm, hbm,
        ],
        out_shape=[
            jax.ShapeDtypeStruct((batch, seq, d), F32),
            jax.ShapeDtypeStruct((DEPTH, batch, POOL_BUF, W_A), F32),
            jax.ShapeDtypeStruct((DEPTH, batch, CHUNK, W_B), F32),
            jax.ShapeDtypeStruct((n_s, 1, d), F32),
            jax.ShapeDtypeStruct((DEPTH, POOL_BUF, n_s, W_A), F32),
            jax.ShapeDtypeStruct((DEPTH, n_s, 1, W_B), F32),
        ],
        scratch_shapes=[
            pltpu.VMEM((HIST + tm, W_A), F32),
            pltpu.VMEM((HIST + tm, W_A - POOL_GROUP_DIM), F32),
            pltpu.VMEM((HIST + tm, W_A - 2 * POOL_GROUP_DIM), F32),
            pltpu.VMEM((HIST + tm, W_A - 3 * POOL_GROUP_DIM), F32),
            pltpu.VMEM((DEPTH, HIST, W_A), F32),
            pltpu.VMEM((DEPTH, HIST, W_A - POOL_GROUP_DIM), F32),
            pltpu.VMEM((DEPTH, HIST, W_A - 2 * POOL_GROUP_DIM), F32),
            pltpu.VMEM((DEPTH, HIST, W_A - 3 * POOL_GROUP_DIM), F32),
            pltpu.VMEM((tm, d), BF16),
            pltpu.VMEM((tm, d), F32),
            pltpu.VMEM((DEPTH, d, D_IN), BF16),
            pltpu.VMEM((DEPTH, W_A + W_B, d), BF16),
            pltpu.VMEM((DEPTH, CHUNK, W_B), F32),
            pltpu.VMEM((DEPTH, POOL_BUF, n_s, W_A), F32),
            pltpu.VMEM((n_s, d), F32),
            pltpu.VMEM((n_s, d), F32),
            pltpu.VMEM((DEPTH, n_s, W_A), F32),
            pltpu.VMEM((DEPTH, n_s, W_B), F32),
            pltpu.VMEM((WEIGHT_SLOTS, WEIGHT_CHUNK_ROWS, D_IN), F32),
            pltpu.SemaphoreType.DMA((WEIGHT_SLOTS,)),
            pltpu.SemaphoreType.DMA((4,)),
        ],
        compiler_params=pltpu.CompilerParams(
            dimension_semantics=("arbitrary", "arbitrary"),
            vmem_limit_bytes=VMEM_LIMIT_BYTES,
        ),
        name="trunk",
    )(*args)


def kernel(x_prompt, x_sample, state_pool, norm_g, w_in, w_pool, pool_scale, v_norm_g, w_s, b_s, w_out, final_norm_g):
    assert x_sample.shape[1] == 1 and state_pool.shape[2] == POOL_BUF
    sp_t = jnp.transpose(state_pool, (0, 2, 1, 3))
    y_prompt, pool_prompt, chunk_v_prompt, y_sample, pool_sample_t, chunk_v_sample = _trunk_call(
        x_prompt, x_sample, sp_t, norm_g, w_in, w_pool, pool_scale, v_norm_g, w_s, b_s, w_out, final_norm_g)
    pool_sample = jnp.transpose(pool_sample_t, (0, 2, 1, 3))
    return (y_prompt, y_sample, pool_prompt, pool_sample, chunk_v_prompt, chunk_v_sample)
```

```python
import jax
import jax.numpy as jnp
from jax import lax
from jax.experimental import pallas as pl
from jax.experimental.pallas import tpu as pltpu

D_MODEL = 1024
DEPTH = 2
W_A = 512
W_B = 512
D_IN = 2 * W_A + 3 * W_B
POOL_WINDOWS = (2, 4, 8, 16)
N_POOL_GROUPS = len(POOL_WINDOWS)
POOL_GROUP_DIM = W_A // N_POOL_GROUPS
POOL_BUF = max(POOL_WINDOWS) - 1
CHUNK = 128
N_HEADS_B = 4
HEAD_DIM_B = W_B // N_HEADS_B
EPS = 1e-6
SAMPLE_PAST_LEN = 16384

COL_A_IN = 0
COL_A_GATE = W_A
COL_U = 2 * W_A
COL_V = 2 * W_A + W_B
COL_B_GATE = 2 * W_A + 2 * W_B

HIST = 16
ROW_TILE = 512
ROW_BLOCKS = 2
WEIGHT_CHUNK_ROWS = 256
WEIGHT_SLOTS = 4
VMEM_LIMIT_BYTES = 60 * 1024 * 1024

BF16 = jnp.bfloat16
F32 = jnp.float32


def _dot(a, b):
    return jnp.dot(a.astype(BF16), b.astype(BF16), preferred_element_type=F32)


def _rmsnorm(x, g):
    ms = jnp.mean(x * x, axis=-1, keepdims=True)
    return (x * lax.rsqrt(ms + EPS)) * g


def _silu(x):
    hx = 0.5 * x
    return hx * jnp.tanh(hx) + hx


def _head_rmsnorm(v, g):
    parts = []
    for h in range(N_HEADS_B):
        sl = slice(h * HEAD_DIM_B, (h + 1) * HEAD_DIM_B)
        parts.append(_rmsnorm(v[:, sl], g[:, sl]))
    return jnp.concatenate(parts, axis=-1)


def _masked_ws(ws_ref, l):
    row = lax.broadcasted_iota(jnp.int32, (CHUNK, CHUNK), 0)
    col = lax.broadcasted_iota(jnp.int32, (CHUNK, CHUNK), 1)
    keep = col <= row
    return [jnp.where(keep, ws_ref[l, h], 0.0).astype(BF16) for h in range(N_HEADS_B)]


class _Refs:
    def __init__(self, **kw):
        self.__dict__.update(kw)


def _row(ref, l):
    return ref[l:l + 1, :]


class _WeightStream:
    def __init__(self, w_in_hbm, w_out_hbm, w_in_bf, w_out_bf, stage, sems):
        self.chunks = []
        for l in range(DEPTH):
            for src, dst in ((w_in_hbm, w_in_bf), (w_out_hbm, w_out_bf)):
                _, k, n = src.shape
                self.chunks += [(src, dst, l, r, n) for r in range(0, k, WEIGHT_CHUNK_ROWS)]
        self.per_layer = len(self.chunks) // DEPTH
        self.stage, self.sems = stage, sems
        self.done = 0

    def _copy(self, i):
        src, _, l, r, n = self.chunks[i]
        slot = i % WEIGHT_SLOTS
        return pltpu.make_async_copy(
            src.at[l, pl.ds(r, WEIGHT_CHUNK_ROWS), :], self.stage.at[slot, :, pl.ds(0, n)], self.sems.at[slot])

    def start(self):
        for i in range(WEIGHT_SLOTS - 1):
            self._copy(i).start()

    def service(self):
        i = self.done
        if i == len(self.chunks):
            return
        if i + WEIGHT_SLOTS - 1 < len(self.chunks):
            self._copy(i + WEIGHT_SLOTS - 1).start()
        self._copy(i).wait()
        _, dst, l, r, n = self.chunks[i]
        dst[l, r:r + WEIGHT_CHUNK_ROWS, :] = self.stage[i % WEIGHT_SLOTS, :, 0:n].astype(BF16)
        self.done += 1

    def finish_layer(self, l):
        while self.done < self.per_layer * (l + 1):
            self.service()


class _Chain:
    def __init__(self, l, r0, rows, t, tm, R, lead=False, gates_done=False):
        self.l, self.r0, self.rows, self.t, self.tm, self.R = l, r0, rows, t, tm, R
        self.lead, self.gates_done = lead, gates_done

    def _in(self, col, width):
        R = self.R
        h = R.hn_buf[...] if self.lead else R.h_buf[self.r0:self.r0 + self.rows, :]
        return jnp.dot(h, R.w_in[self.l, :, col:col + width], preferred_element_type=F32)

    def p0(self):
        self.v = self._in(COL_V, W_B)
        self.a_in = self._in(COL_A_IN, W_A)

    def p1(self):
        self.a_gate = self.R.gate_buf[0] if self.gates_done else self._in(COL_A_GATE, W_A)
        self.u = self._in(COL_U, W_B)

    def p2(self):
        l, r0, rows, R = self.l, self.r0, self.rows, self.R
        n_chunks = rows // CHUNK
        a_in = self.a_in
        base = HIST + r0

        vn = _head_rmsnorm(self.v, _row(R.v_norm_g, l))
        self.cv_rows = vn[rows - CHUNK:, :]
        vn_b = vn.astype(BF16)

        a_hist, s2_hist, s4_hist, s8_hist = R.hists
        a_hist[base:base + rows, :] = a_in
        s2 = a_in + a_hist[base - 1:base - 1 + rows, :]
        s2_hist[base:base + rows, :] = s2[:, POOL_GROUP_DIM:]
        s4 = s2[:, POOL_GROUP_DIM:] + s2_hist[base - 2:base - 2 + rows, :]
        s4_hist[base:base + rows, :] = s4[:, POOL_GROUP_DIM:]
        s8 = s4[:, POOL_GROUP_DIM:] + s4_hist[base - 4:base - 4 + rows, :]
        s8_hist[base:base + rows, :] = s8[:, POOL_GROUP_DIM:]
        s16 = s8[:, POOL_GROUP_DIM:] + s8_hist[base - 8:base - 8 + rows, :]
        win_sums = (s2[:, :POOL_GROUP_DIM], s4[:, :POOL_GROUP_DIM], s8[:, :POOL_GROUP_DIM], s16)
        self.pool_rows = a_in[rows - POOL_BUF:, :]

        d_parts = []
        for g, w in enumerate(POOL_WINDOWS):
            sl = slice(g * POOL_GROUP_DIM, (g + 1) * POOL_GROUP_DIM)
            ssum = win_sums[g]
            if r0 == 0:
                pos1 = lax.broadcasted_iota(jnp.int32, (HIST, POOL_GROUP_DIM), 0) + self.t * self.tm + 1
                inv_head = 1.0 / jnp.minimum(pos1, w).astype(F32)
                mean = jnp.concatenate([ssum[:HIST] * inv_head, ssum[HIST:] * (1.0 / w)], axis=0)
            else:
                mean = ssum * (1.0 / w)
            d_parts.append((mean - a_in[:, sl]).astype(BF16))

        ws_b = _masked_ws(R.ws, l)
        s_heads = []
        for hh in range(N_HEADS_B):
            sl = slice(hh * HEAD_DIM_B, (hh + 1) * HEAD_DIM_B)
            rhs = jnp.concatenate([vn_b[c * CHUNK:(c + 1) * CHUNK, sl] for c in range(n_chunks)], axis=1)
            res = jnp.dot(ws_b[hh], rhs, preferred_element_type=F32)
            s_heads.append(jnp.concatenate(
                [res[:, c * HEAD_DIM_B:(c + 1) * HEAD_DIM_B] for c in range(n_chunks)], axis=0))
        self.s_heads = s_heads

        self.y_parts = [_dot(d_parts[g], R.w_pool[l, g]) for g in range(N_POOL_GROUPS)]
        self.b_gate = self.R.gate_buf[1] if self.gates_done else self._in(COL_B_GATE, W_B)

    def p3(self):
        l, rows, R = self.l, self.rows, self.R
        n_chunks = rows // CHUNK
        a_out = ((jnp.concatenate(self.y_parts, axis=-1) * _row(R.pool_scale, l)) * _silu(self.a_gate)).astype(BF16)
        out_a = _dot(a_out, R.w_out[l, 0:W_A, :])
        s = jnp.concatenate(self.s_heads, axis=-1) + jnp.concatenate([R.bias[l]] * n_chunks, axis=0)
        b_out = ((self.u * s) * _silu(self.b_gate)).astype(BF16)
        self.out = out_a + _dot(b_out, R.w_out[l, W_A:, :])


def _per_head_lanes(vals):
    return jnp.concatenate([jnp.broadcast_to(v, (1, HEAD_DIM_B)) for v in vals], axis=-1)


def _sample_group(xs_ref, sp_ref, norm_g_ref, b_s_ref, fin_g, R, ys_ref, a_in_s_ref, vn_s_ref):
    x = xs_ref[...]
    for l in range(DEPTH):
        h = _rmsnorm(x, _row(norm_g_ref, l)).astype(BF16)
        a_in = _dot(h, R.w_in[l, :, COL_A_IN:COL_A_IN + W_A])
        a_in_s_ref[l] = a_in
        a_gate = _dot(h, R.w_in[l, :, COL_A_GATE:COL_A_GATE + W_A])
        a_parts = []
        for g, w in enumerate(POOL_WINDOWS):
            sl = slice(g * POOL_GROUP_DIM, (g + 1) * POOL_GROUP_DIM)
            ssum = a_in[:, sl]
            for k in range(1, w):
                ssum = ssum + sp_ref[l, POOL_BUF - k, :, sl]
            cnt = float(min(SAMPLE_PAST_LEN + 1, w))
            d = ssum / cnt - a_in[:, sl]
            yg = _dot(d, R.w_pool[l, g]) * _row(R.pool_scale, l)[:, sl]
            a_parts.append(yg * _silu(a_gate[:, sl]))
        a_out = jnp.concatenate(a_parts, axis=-1)

        v = _dot(h, R.w_in[l, :, COL_V:COL_V + W_B])
        vn = _head_rmsnorm(v, _row(R.v_norm_g, l))
        vn_s_ref[l] = vn
        ws00 = _per_head_lanes([R.ws[l, hh, 0:1, 0:1] for hh in range(N_HEADS_B)])
        b0 = _per_head_lanes([b_s_ref[l, hh:hh + 1, 0:1] for hh in range(N_HEADS_B)])
        s = ws00 * vn + b0
        u = _dot(h, R.w_in[l, :, COL_U:COL_U + W_B])
        b_gate = _dot(h, R.w_in[l, :, COL_B_GATE:COL_B_GATE + W_B])
        b_out = (u * s) * _silu(b_gate)

        x = x + (_dot(a_out, R.w_out[l, 0:W_A, :]) + _dot(b_out, R.w_out[l, W_A:, :]))
    ys_ref[...] = _rmsnorm(x, fin_g)


def _trunk_kernel(x_ref, x_lead_next_ref, xs_hbm, sp_hbm, norm_g_ref, w_in_hbm, w_pool_ref, pool_scale_ref, v_norm_g_ref,
                  ws_ref, b_s_ref, w_out_hbm, fin_g_ref,
                  y_ref, pool_ref, cv_ref, ys_hbm, pool_s_t_hbm, vn_s_hbm,
                  a_hist, s2_hist, s4_hist, s8_hist, a_carry, s2_carry, s4_carry, s8_carry,
                  h_buf, x_buf, w_in_bf, w_out_bf, bias_buf, sp_buf,
                  xs_buf, ys_buf, a_in_s_buf, vn_s_buf, hn_buf, gate_buf, stage, sems, io_sems):
    b = pl.program_id(0)
    t = pl.program_id(1)
    tm = x_ref.shape[1]
    rows = tm // ROW_BLOCKS
    R = _Refs(w_in=w_in_bf, w_pool=w_pool_ref, pool_scale=pool_scale_ref, v_norm_g=v_norm_g_ref, ws=ws_ref,
              bias=bias_buf, w_out=w_out_bf, hists=(a_hist, s2_hist, s4_hist, s8_hist), h_buf=h_buf,
              hn_buf=hn_buf, gate_buf=gate_buf)
    hists = R.hists
    carries = (a_carry, s2_carry, s4_carry, s8_carry)
    fin_g = fin_g_ref[...].reshape(1, D_MODEL)

    def tile_body(first_tile, before_layer=None, between_phases=None):
        blocks = [slice(k * rows, (k + 1) * rows) for k in range(ROW_BLOCKS)]

        def next_gate(col, width):
            return jnp.dot(hn_buf[...], w_in_bf[0, :, col:col + width], preferred_element_type=F32)

        for l in range(DEPTH):
            if before_layer is not None:
                before_layer(l)

            def x_rows(rs, l=l):
                return x_ref[0, rs, :] if l == 0 else x_buf[rs, :]

            def phase_done():
                if between_phases is not None:
                    between_phases()

            if l == 0 and first_tile:
                hn_buf[...] = _rmsnorm(x_rows(blocks[0]), _row(norm_g_ref, 0)).astype(BF16)
            for rs in (blocks[1:] if l == 0 else blocks):
                h_buf[rs, :] = _rmsnorm(x_rows(rs), _row(norm_g_ref, l)).astype(BF16)
            chains = [_Chain(l, k * rows, rows, t, tm, R, lead=(l == 0 and k == 0),
                             gates_done=(l == 0 and k == 0 and not first_tile)) for k in range(ROW_BLOCKS)]
            for c in chains:
                c.p0()
            phase_done()
            for c in chains:
                c.p1()
            phase_done()
            if l == DEPTH - 1:
                hn_buf[...] = _rmsnorm(x_lead_next_ref[0], _row(norm_g_ref, 0)).astype(BF16)
                gate_buf[0] = next_gate(COL_A_GATE, W_A)
            for buf, carry in zip(hists, carries):
                buf[0:HIST, :] = carry[l]
            for c in chains:
                c.p2()
                phase_done()
            for buf, carry in zip(hists, carries):
                carry[l] = buf[tm:tm + HIST, :]
            pool_ref[l, 0] = chains[-1].pool_rows
            cv_ref[l, 0] = chains[-1].cv_rows
            for c, rs in zip(chains, blocks):
                c.p3()
                x_new = x_rows(rs) + c.out
                if l + 1 < DEPTH:
                    x_buf[rs, :] = x_new
                else:
                    y_ref[0, rs, :] = _rmsnorm(x_new, fin_g)
                phase_done()
        gate_buf[1] = next_gate(COL_B_GATE, W_B)

    def zero_hist():
        for carry in carries:
            carry[...] = jnp.zeros(carry.shape, F32)

    first = jnp.logical_and(b == 0, t == 0)

    @pl.when(first)
    def _():
        stream = _WeightStream(w_in_hbm, w_out_hbm, w_in_bf, w_out_bf, stage, sems)
        stream.start()
        in_copies = [
            pltpu.make_async_copy(sp_hbm, sp_buf, io_sems.at[0]),
            pltpu.make_async_copy(xs_hbm.at[:, 0, :], xs_buf, io_sems.at[1]),
        ]
        for l in range(DEPTH):
            for hh in range(N_HEADS_B):
                bias_buf[l, :, hh * HEAD_DIM_B:(hh + 1) * HEAD_DIM_B] = jnp.broadcast_to(
                    b_s_ref[l, hh:hh + 1, :], (CHUNK, CHUNK)).T
        zero_hist()

        def before_layer(l):
            stream.finish_layer(l)
            if l == DEPTH - 1:
                for cp in in_copies:
                    cp.start()

        tile_body(True, before_layer, stream.service)
        for cp in in_copies:
            cp.wait()
        keep_copy = pltpu.make_async_copy(sp_buf.at[:, pl.ds(1, POOL_BUF - 1)],
                                          pool_s_t_hbm.at[:, pl.ds(0, POOL_BUF - 1)], io_sems.at[2])
        keep_copy.start()
        _sample_group(xs_buf, sp_buf, norm_g_ref, b_s_ref, fin_g, R, ys_buf, a_in_s_buf, vn_s_buf)
        out_copies = [
            pltpu.make_async_copy(ys_buf, ys_hbm.at[:, 0, :], io_sems.at[0]),
            pltpu.make_async_copy(vn_s_buf, vn_s_hbm.at[:, :, 0, :], io_sems.at[1]),
            pltpu.make_async_copy(a_in_s_buf, pool_s_t_hbm.at[:, POOL_BUF - 1], io_sems.at[3]),
        ]
        for cp in out_copies:
            cp.start()
        keep_copy.wait()
        for cp in out_copies:
            cp.wait()

    @pl.when(jnp.logical_not(first))
    def _():
        pl.when(t == 0)(zero_hist)
        tile_body(False)


def _const_spec(shape):
    zeros = (0,) * len(shape)
    return pl.BlockSpec(shape, lambda b, t: zeros, pipeline_mode=pl.Buffered(1))


def _trunk_call(x, xs, sp_t, norm_g, w_in, w_pool, pool_scale, v_norm_g, w_s, b_s, w_out, fin_g):
    batch, seq, d = x.shape
    n_s = xs.shape[0]
    tm = ROW_TILE
    assert seq % tm == 0 and tm % (ROW_BLOCKS * CHUNK) == 0
    assert d % WEIGHT_CHUNK_ROWS == 0 and (W_A + W_B) % WEIGHT_CHUNK_ROWS == 0
    hbm = pl.BlockSpec(memory_space=pl.ANY)
    tiles_per_seq = seq // tm
    n_tiles = batch * tiles_per_seq
    rows = tm // ROW_BLOCKS

    def lead_block_of_next_tile(b, t):
        i = jnp.minimum(b * tiles_per_seq + t + 1, n_tiles - 1)
        return (i // tiles_per_seq, (i % tiles_per_seq) * ROW_BLOCKS, 0)

    consts = (xs, sp_t, norm_g, w_in, w_pool, pool_scale, v_norm_g, w_s, b_s, w_out, fin_g)
    in_specs = [pl.BlockSpec((1, tm, d), lambda b, t: (b, t, 0)),
                pl.BlockSpec((1, rows, d), lead_block_of_next_tile)]
    for a in consts:
        in_specs.append(hbm if (a is w_in or a is w_out or a is sp_t or a is xs) else _const_spec(a.shape))
    args = (x, x) + consts
    return pl.pallas_call(
        _trunk_kernel,
        grid=(batch, seq // tm),
        in_specs=in_specs,
        out_specs=[
            pl.BlockSpec((1, tm, d), lambda b, t: (b, t, 0)),
            pl.BlockSpec((DEPTH, 1, POOL_BUF, W_A), lambda b, t: (0, b, 0, 0)),
            pl.BlockSpec((DEPTH, 1, CHUNK, W_B), lambda b, t: (0, b, 0, 0)),
            hbm, hbm, hbm,
        ],
        out_shape=[
            jax.ShapeDtypeStruct((batch, seq, d), F32),
            jax.ShapeDtypeStruct((DEPTH, batch, POOL_BUF, W_A), F32),
            jax.ShapeDtypeStruct((DEPTH, batch, CHUNK, W_B), F32),
            jax.ShapeDtypeStruct((n_s, 1, d), F32),
            jax.ShapeDtypeStruct((DEPTH, POOL_BUF, n_s, W_A), F32),
            jax.ShapeDtypeStruct((DEPTH, n_s, 1, W_B), F32),
        ],
        scratch_shapes=[
            pltpu.VMEM((HIST + tm, W_A), F32),
            pltpu.VMEM((HIST + tm, W_A - POOL_GROUP_DIM), F32),
            pltpu.VMEM((HIST + tm, W_A - 2 * POOL_GROUP_DIM), F32),
            pltpu.VMEM((HIST + tm, W_A - 3 * POOL_GROUP_DIM), F32),
            pltpu.VMEM((DEPTH, HIST, W_A), F32),
            pltpu.VMEM((DEPTH, HIST, W_A - POOL_GROUP_DIM), F32),
            pltpu.VMEM((DEPTH, HIST, W_A - 2 * POOL_GROUP_DIM), F32),
            pltpu.VMEM((DEPTH, HIST, W_A - 3 * POOL_GROUP_DIM), F32),
            pltpu.VMEM((tm, d), BF16),
            pltpu.VMEM((tm, d), F32),
            pltpu.VMEM((DEPTH, d, D_IN), BF16),
            pltpu.VMEM((DEPTH, W_A + W_B, d), BF16),
            pltpu.VMEM((DEPTH, CHUNK, W_B), F32),
            pltpu.VMEM((DEPTH, POOL_BUF, n_s, W_A), F32),
            pltpu.VMEM((n_s, d), F32),
            pltpu.VMEM((n_s, d), F32),
            pltpu.VMEM((DEPTH, n_s, W_A), F32),
            pltpu.VMEM((DEPTH, n_s, W_B), F32),
            pltpu.VMEM((rows, d), BF16),
            pltpu.VMEM((2, rows, W_B), F32),
            pltpu.VMEM((WEIGHT_SLOTS, WEIGHT_CHUNK_ROWS, D_IN), F32),
            pltpu.SemaphoreType.DMA((WEIGHT_SLOTS,)),
            pltpu.SemaphoreType.DMA((4,)),
        ],
        compiler_params=pltpu.CompilerParams(
            dimension_semantics=("arbitrary", "arbitrary"),
            vmem_limit_bytes=VMEM_LIMIT_BYTES,
        ),
        name="trunk",
    )(*args)


def kernel(x_prompt, x_sample, state_pool, norm_g, w_in, w_pool, pool_scale, v_norm_g, w_s, b_s, w_out, final_norm_g):
    assert x_sample.shape[1] == 1 and state_pool.shape[2] == POOL_BUF
    sp_t = jnp.transpose(state_pool, (0, 2, 1, 3))
    y_prompt, pool_prompt, chunk_v_prompt, y_sample, pool_sample_t, chunk_v_sample = _trunk_call(
        x_prompt, x_sample, sp_t, norm_g, w_in, w_pool, pool_scale, v_norm_g, w_s, b_s, w_out, final_norm_g)
    pool_sample = jnp.transpose(pool_sample_t, (0, 2, 1, 3))
    return (y_prompt, y_sample, pool_prompt, pool_sample, chunk_v_prompt, chunk_v_sample)
```

```python
import jax
import jax.numpy as jnp
from jax import lax
from jax.experimental import pallas as pl
from jax.experimental.pallas import tpu as pltpu

D_MODEL = 1024
DEPTH = 2
W_A = 512
W_B = 512
D_IN = 2 * W_A + 3 * W_B
POOL_WINDOWS = (2, 4, 8, 16)
N_POOL_GROUPS = len(POOL_WINDOWS)
POOL_GROUP_DIM = W_A // N_POOL_GROUPS
POOL_BUF = max(POOL_WINDOWS) - 1
CHUNK = 128
N_HEADS_B = 4
HEAD_DIM_B = W_B // N_HEADS_B
EPS = 1e-6
SAMPLE_PAST_LEN = 16384

COL_A_IN = 0
COL_A_GATE = W_A
COL_U = 2 * W_A
COL_V = 2 * W_A + W_B
COL_B_GATE = 2 * W_A + 2 * W_B

HIST = 16
ROW_TILE = 512
ROW_BLOCKS = 2
WEIGHT_CHUNK_ROWS = 256
WEIGHT_SLOTS = 4
VMEM_LIMIT_BYTES = 56 * 1024 * 1024

BF16 = jnp.bfloat16
F32 = jnp.float32


def _dot(a, b):
    return jnp.dot(a.astype(BF16), b.astype(BF16), preferred_element_type=F32)


def _rmsnorm(x, g):
    ms = jnp.mean(x * x, axis=-1, keepdims=True)
    return (x * lax.rsqrt(ms + EPS)) * g


def _silu(x):
    hx = 0.5 * x
    return hx * jnp.tanh(hx) + hx


def _head_rmsnorm(v, g):
    parts = []
    for h in range(N_HEADS_B):
        sl = slice(h * HEAD_DIM_B, (h + 1) * HEAD_DIM_B)
        parts.append(_rmsnorm(v[:, sl], g[:, sl]))
    return jnp.concatenate(parts, axis=-1)


def _masked_ws(ws_ref, l):
    row = lax.broadcasted_iota(jnp.int32, (CHUNK, CHUNK), 0)
    col = lax.broadcasted_iota(jnp.int32, (CHUNK, CHUNK), 1)
    keep = col <= row
    return [jnp.where(keep, ws_ref[l, h], 0.0).astype(BF16) for h in range(N_HEADS_B)]


class _Refs:
    def __init__(self, **kw):
        self.__dict__.update(kw)


def _row(ref, l):
    return ref[l:l + 1, :]


class _WeightStream:
    def __init__(self, w_in_hbm, w_out_hbm, w_in_bf, w_out_bf, stage, sems):
        self.chunks = []
        for l in range(DEPTH):
            for src, dst in ((w_in_hbm, w_in_bf), (w_out_hbm, w_out_bf)):
                _, k, n = src.shape
                self.chunks += [(src, dst, l, r, n) for r in range(0, k, WEIGHT_CHUNK_ROWS)]
        self.per_layer = len(self.chunks) // DEPTH
        self.stage, self.sems = stage, sems
        self.done = 0

    def _copy(self, i):
        src, _, l, r, n = self.chunks[i]
        slot = i % WEIGHT_SLOTS
        return pltpu.make_async_copy(
            src.at[l, pl.ds(r, WEIGHT_CHUNK_ROWS), :], self.stage.at[slot, :, pl.ds(0, n)], self.sems.at[slot])

    def start(self):
        for i in range(WEIGHT_SLOTS - 1):
            self._copy(i).start()

    def service(self):
        i = self.done
        if i == len(self.chunks):
            return
        if i + WEIGHT_SLOTS - 1 < len(self.chunks):
            self._copy(i + WEIGHT_SLOTS - 1).start()
        self._copy(i).wait()
        _, dst, l, r, n = self.chunks[i]
        dst[l, r:r + WEIGHT_CHUNK_ROWS, :] = self.stage[i % WEIGHT_SLOTS, :, 0:n].astype(BF16)
        self.done += 1

    def finish_layer(self, l):
        while self.done < self.per_layer * (l + 1):
            self.service()


class _Chain:
    def __init__(self, l, r0, rows, t, tm, R):
        self.l, self.r0, self.rows, self.t, self.tm, self.R = l, r0, rows, t, tm, R

    def _in(self, col, width):
        R = self.R
        return jnp.dot(R.h_buf[self.r0:self.r0 + self.rows, :], R.w_in[self.l, :, col:col + width],
                       preferred_element_type=F32)

    def p0(self):
        self.v = self._in(COL_V, W_B)
        self.a_in = self._in(COL_A_IN, W_A)

    def p1(self):
        self.a_gate = self._in(COL_A_GATE, W_A)
        self.u = self._in(COL_U, W_B)

    def p2a(self):
        l, rows, R = self.l, self.rows, self.R
        n_chunks = rows // CHUNK
        vn = _head_rmsnorm(self.v, _row(R.v_norm_g, l))
        self.cv_rows = vn[rows - CHUNK:, :]
        vn_b = vn.astype(BF16)

        ws_b = _masked_ws(R.ws, l)
        s_heads = []
        for hh in range(N_HEADS_B):
            sl = slice(hh * HEAD_DIM_B, (hh + 1) * HEAD_DIM_B)
            rhs = jnp.concatenate([vn_b[c * CHUNK:(c + 1) * CHUNK, sl] for c in range(n_chunks)], axis=1)
            res = jnp.dot(ws_b[hh], rhs, preferred_element_type=F32)
            s_heads.append(jnp.concatenate(
                [res[:, c * HEAD_DIM_B:(c + 1) * HEAD_DIM_B] for c in range(n_chunks)], axis=0))
        self.s_heads = s_heads

    def p2b(self):
        l, r0, rows, R = self.l, self.r0, self.rows, self.R
        a_in = self.a_in
        base = HIST + r0

        a_hist, s2_hist, s4_hist, s8_hist = R.hists
        a_hist[base:base + rows, :] = a_in
        s2 = a_in + a_hist[base - 1:base - 1 + rows, :]
        s2_hist[base:base + rows, :] = s2[:, POOL_GROUP_DIM:]
        s4 = s2[:, POOL_GROUP_DIM:] + s2_hist[base - 2:base - 2 + rows, :]
        s4_hist[base:base + rows, :] = s4[:, POOL_GROUP_DIM:]
        s8 = s4[:, POOL_GROUP_DIM:] + s4_hist[base - 4:base - 4 + rows, :]
        s8_hist[base:base + rows, :] = s8[:, POOL_GROUP_DIM:]
        s16 = s8[:, POOL_GROUP_DIM:] + s8_hist[base - 8:base - 8 + rows, :]
        win_sums = (s2[:, :POOL_GROUP_DIM], s4[:, :POOL_GROUP_DIM], s8[:, :POOL_GROUP_DIM], s16)
        self.pool_rows = a_in[rows - POOL_BUF:, :]

        d_parts = []
        for g, w in enumerate(POOL_WINDOWS):
            sl = slice(g * POOL_GROUP_DIM, (g + 1) * POOL_GROUP_DIM)
            ssum = win_sums[g]
            if r0 == 0:
                pos1 = lax.broadcasted_iota(jnp.int32, (HIST, POOL_GROUP_DIM), 0) + self.t * self.tm + 1
                inv_head = 1.0 / jnp.minimum(pos1, w).astype(F32)
                mean = jnp.concatenate([ssum[:HIST] * inv_head, ssum[HIST:] * (1.0 / w)], axis=0)
            else:
                mean = ssum * (1.0 / w)
            d_parts.append((mean - a_in[:, sl]).astype(BF16))

        self.y_parts = [_dot(d_parts[g], R.w_pool[l, g]) for g in range(N_POOL_GROUPS)]
        self.b_gate = self._in(COL_B_GATE, W_B)

    def p3(self):
        l, rows, R = self.l, self.rows, self.R
        n_chunks = rows // CHUNK
        a_out = ((jnp.concatenate(self.y_parts, axis=-1) * _row(R.pool_scale, l)) * _silu(self.a_gate)).astype(BF16)
        out_a = _dot(a_out, R.w_out[l, 0:W_A, :])
        s = jnp.concatenate(self.s_heads, axis=-1) + jnp.concatenate([R.bias[l]] * n_chunks, axis=0)
        b_out = ((self.u * s) * _silu(self.b_gate)).astype(BF16)
        self.out = out_a + _dot(b_out, R.w_out[l, W_A:, :])


def _per_head_lanes(vals):
    return jnp.concatenate([jnp.broadcast_to(v, (1, HEAD_DIM_B)) for v in vals], axis=-1)


def _sample_group(xs_ref, sp_ref, norm_g_ref, b_s_ref, fin_g, R, ys_ref, a_in_s_ref, vn_s_ref):
    x = xs_ref[...]
    for l in range(DEPTH):
        h = _rmsnorm(x, _row(norm_g_ref, l)).astype(BF16)
        a_in = _dot(h, R.w_in[l, :, COL_A_IN:COL_A_IN + W_A])
        a_in_s_ref[l] = a_in
        a_gate = _dot(h, R.w_in[l, :, COL_A_GATE:COL_A_GATE + W_A])
        a_parts = []
        for g, w in enumerate(POOL_WINDOWS):
            sl = slice(g * POOL_GROUP_DIM, (g + 1) * POOL_GROUP_DIM)
            ssum = a_in[:, sl]
            for k in range(1, w):
                ssum = ssum + sp_ref[l, POOL_BUF - k, :, sl]
            cnt = float(min(SAMPLE_PAST_LEN + 1, w))
            d = ssum / cnt - a_in[:, sl]
            yg = _dot(d, R.w_pool[l, g]) * _row(R.pool_scale, l)[:, sl]
            a_parts.append(yg * _silu(a_gate[:, sl]))
        a_out = jnp.concatenate(a_parts, axis=-1)

        v = _dot(h, R.w_in[l, :, COL_V:COL_V + W_B])
        vn = _head_rmsnorm(v, _row(R.v_norm_g, l))
        vn_s_ref[l] = vn
        ws00 = _per_head_lanes([R.ws[l, hh, 0:1, 0:1] for hh in range(N_HEADS_B)])
        b0 = _per_head_lanes([b_s_ref[l, hh:hh + 1, 0:1] for hh in range(N_HEADS_B)])
        s = ws00 * vn + b0
        u = _dot(h, R.w_in[l, :, COL_U:COL_U + W_B])
        b_gate = _dot(h, R.w_in[l, :, COL_B_GATE:COL_B_GATE + W_B])
        b_out = (u * s) * _silu(b_gate)

        x = x + (_dot(a_out, R.w_out[l, 0:W_A, :]) + _dot(b_out, R.w_out[l, W_A:, :]))
    ys_ref[...] = _rmsnorm(x, fin_g)


def _trunk_kernel(x_ref, xs_hbm, sp_hbm, norm_g_ref, w_in_hbm, w_pool_ref, pool_scale_ref, v_norm_g_ref,
                  ws_ref, b_s_ref, w_out_hbm, fin_g_ref,
                  y_ref, pool_ref, cv_ref, ys_hbm, pool_s_t_hbm, vn_s_hbm,
                  a_hist, s2_hist, s4_hist, s8_hist, a_carry, s2_carry, s4_carry, s8_carry,
                  h_buf, x_buf, w_in_bf, w_out_bf, bias_buf, sp_buf,
                  xs_buf, ys_buf, a_in_s_buf, vn_s_buf, stage, sems, io_sems):
    b = pl.program_id(0)
    t = pl.program_id(1)
    tm = x_ref.shape[1]
    rows = tm // ROW_BLOCKS
    R = _Refs(w_in=w_in_bf, w_pool=w_pool_ref, pool_scale=pool_scale_ref, v_norm_g=v_norm_g_ref, ws=ws_ref,
              bias=bias_buf, w_out=w_out_bf, hists=(a_hist, s2_hist, s4_hist, s8_hist), h_buf=h_buf)
    hists = R.hists
    carries = (a_carry, s2_carry, s4_carry, s8_carry)
    fin_g = fin_g_ref[...].reshape(1, D_MODEL)

    def tile_body(before_layer=None, between_phases=None):
        for l in range(DEPTH):
            if before_layer is not None:
                before_layer(l)

            def x_rows(rs, l=l):
                return x_ref[0, rs, :] if l == 0 else x_buf[rs, :]

            def phase_done():
                if between_phases is not None:
                    between_phases()

            blocks = [slice(k * rows, (k + 1) * rows) for k in range(ROW_BLOCKS)]
            for rs in blocks:
                h_buf[rs, :] = _rmsnorm(x_rows(rs), _row(norm_g_ref, l)).astype(BF16)
            chains = [_Chain(l, k * rows, rows, t, tm, R) for k in range(ROW_BLOCKS)]
            for c in chains:
                c.p0()
            phase_done()
            for c in chains:
                c.p1()
            phase_done()
            for buf, carry in zip(hists, carries):
                buf[0:HIST, :] = carry[l]
            for c in chains:
                c.p2a()
            for c in chains:
                c.p2b()
                phase_done()
            for buf, carry in zip(hists, carries):
                carry[l] = buf[tm:tm + HIST, :]
            pool_ref[l, 0] = chains[-1].pool_rows
            cv_ref[l, 0] = chains[-1].cv_rows
            for c, rs in zip(chains, blocks):
                c.p3()
                x_new = x_rows(rs) + c.out
                if l + 1 < DEPTH:
                    x_buf[rs, :] = x_new
                else:
                    y_ref[0, rs, :] = _rmsnorm(x_new, fin_g)
                phase_done()

    def zero_hist():
        for carry in carries:
            carry[...] = jnp.zeros(carry.shape, F32)

    first = jnp.logical_and(b == 0, t == 0)

    @pl.when(first)
    def _():
        stream = _WeightStream(w_in_hbm, w_out_hbm, w_in_bf, w_out_bf, stage, sems)
        stream.start()
        in_copies = [
            pltpu.make_async_copy(sp_hbm, sp_buf, io_sems.at[0]),
            pltpu.make_async_copy(xs_hbm.at[:, 0, :], xs_buf, io_sems.at[1]),
        ]
        for l in range(DEPTH):
            for hh in range(N_HEADS_B):
                bias_buf[l, :, hh * HEAD_DIM_B:(hh + 1) * HEAD_DIM_B] = jnp.broadcast_to(
                    b_s_ref[l, hh:hh + 1, :], (CHUNK, CHUNK)).T
        zero_hist()

        def before_layer(l):
            stream.finish_layer(l)
            if l == DEPTH - 1:
                for cp in in_copies:
                    cp.start()

        tile_body(before_layer, stream.service)
        for cp in in_copies:
            cp.wait()
        keep_copy = pltpu.make_async_copy(sp_buf.at[:, pl.ds(1, POOL_BUF - 1)],
                                          pool_s_t_hbm.at[:, pl.ds(0, POOL_BUF - 1)], io_sems.at[2])
        keep_copy.start()
        _sample_group(xs_buf, sp_buf, norm_g_ref, b_s_ref, fin_g, R, ys_buf, a_in_s_buf, vn_s_buf)
        out_copies = [
            pltpu.make_async_copy(ys_buf, ys_hbm.at[:, 0, :], io_sems.at[0]),
            pltpu.make_async_copy(vn_s_buf, vn_s_hbm.at[:, :, 0, :], io_sems.at[1]),
            pltpu.make_async_copy(a_in_s_buf, pool_s_t_hbm.at[:, POOL_BUF - 1], io_sems.at[3]),
        ]
        for cp in out_copies:
            cp.start()
        keep_copy.wait()
        for cp in out_copies:
            cp.wait()

    @pl.when(jnp.logical_not(first))
    def _():
        pl.when(t == 0)(zero_hist)
        tile_body()


def _const_spec(shape):
    zeros = (0,) * len(shape)
    return pl.BlockSpec(shape, lambda b, t: zeros, pipeline_mode=pl.Buffered(1))


def _trunk_call(x, xs, sp_t, norm_g, w_in, w_pool, pool_scale, v_norm_g, w_s, b_s, w_out, fin_g):
    batch, seq, d = x.shape
    n_s = xs.shape[0]
    tm = ROW_TILE
    assert seq % tm == 0 and tm % (ROW_BLOCKS * CHUNK) == 0
    assert d % WEIGHT_CHUNK_ROWS == 0 and (W_A + W_B) % WEIGHT_CHUNK_ROWS == 0
    hbm = pl.BlockSpec(memory_space=pl.ANY)
    args = (x, xs, sp_t, norm_g, w_in, w_pool, pool_scale, v_norm_g, w_s, b_s, w_out, fin_g)
    in_specs = []
    for a in args:
        if a is x:
            in_specs.append(pl.BlockSpec((1, tm, d), lambda b, t: (b, t, 0)))
        elif a is w_in or a is w_out or a is sp_t or a is xs:
            in_specs.append(hbm)
        else:
            in_specs.append(_const_spec(a.shape))
    return pl.pallas_call(
        _trunk_kernel,
        grid=(batch, seq // tm),
        in_specs=in_specs,
        out_specs=[
            pl.BlockSpec((1, tm, d), lambda b, t: (b, t, 0)),
            pl.BlockSpec((DEPTH, 1, POOL_BUF, W_A), lambda b, t: (0, b, 0, 0)),
            pl.BlockSpec((DEPTH, 1, CHUNK, W_B), lambda b, t: (0, b, 0, 0)),
            hbm, hbm, hbm,
        ],
        out_shape=[
            jax.ShapeDtypeStruct((batch, seq, d), F32),
            jax.ShapeDtypeStruct((DEPTH, batch, POOL_BUF, W_A), F32),
            jax.ShapeDtypeStruct((DEPTH, batch, CHUNK, W_B), F32),
            jax.ShapeDtypeStruct((n_s, 1, d), F32),
            jax.ShapeDtypeStruct((DEPTH, POOL_BUF, n_s, W_A), F32),
            jax.ShapeDtypeStruct((DEPTH, n_s, 1, W_B), F32),
        ],
        scratch_shapes=[
            pltpu.VMEM((HIST + tm, W_A), F32),
            pltpu.VMEM((HIST + tm, W_A - POOL_GROUP_DIM), F32),
            pltpu.VMEM((HIST + tm, W_A - 2 * POOL_GROUP_DIM), F32),
            pltpu.VMEM((HIST + tm, W_A - 3 * POOL_GROUP_DIM), F32),
            pltpu.VMEM((DEPTH, HIST, W_A), F32),
            pltpu.VMEM((DEPTH, HIST, W_A - POOL_GROUP_DIM), F32),
            pltpu.VMEM((DEPTH, HIST, W_A - 2 * POOL_GROUP_DIM), F32),
            pltpu.VMEM((DEPTH, HIST, W_A - 3 * POOL_GROUP_DIM), F32),
            pltpu.VMEM((tm, d), BF16),
            pltpu.VMEM((tm, d), F32),
            pltpu.VMEM((DEPTH, d, D_IN), BF16),
            pltpu.VMEM((DEPTH, W_A + W_B, d), BF16),
            pltpu.VMEM((DEPTH, CHUNK, W_B), F32),
            pltpu.VMEM((DEPTH, POOL_BUF, n_s, W_A), F32),
            pltpu.VMEM((n_s, d), F32),
            pltpu.VMEM((n_s, d), F32),
            pltpu.VMEM((DEPTH, n_s, W_A), F32),
            pltpu.VMEM((DEPTH, n_s, W_B), F32),
            pltpu.VMEM((WEIGHT_SLOTS, WEIGHT_CHUNK_ROWS, D_IN), F32),
            pltpu.SemaphoreType.DMA((WEIGHT_SLOTS,)),
            pltpu.SemaphoreType.DMA((4,)),
        ],
        compiler_params=pltpu.CompilerParams(
            dimension_semantics=("arbitrary", "arbitrary"),
            vmem_limit_bytes=VMEM_LIMIT_BYTES,
        ),
        name="trunk",
    )(*args)


def kernel(x_prompt, x_sample, state_pool, norm_g, w_in, w_pool, pool_scale, v_norm_g, w_s, b_s, w_out, final_norm_g):
    assert x_sample.shape[1] == 1 and state_pool.shape[2] == POOL_BUF
    sp_t = jnp.transpose(state_pool, (0, 2, 1, 3))
    y_prompt, pool_prompt, chunk_v_prompt, y_sample, pool_sample_t, chunk_v_sample = _trunk_call(
        x_prompt, x_sample, sp_t, norm_g, w_in, w_pool, pool_scale, v_norm_g, w_s, b_s, w_out, final_norm_g)
    pool_sample = jnp.transpose(pool_sample_t, (0, 2, 1, 3))
    return (y_prompt, y_sample, pool_prompt, pool_sample, chunk_v_prompt, chunk_v_sample)
```

```python
import functools

import jax
import jax.numpy as jnp
from jax import lax
from jax.experimental import pallas as pl
from jax.experimental.pallas import tpu as pltpu

D_MODEL = 1024
DEPTH = 2
W_A = 512
W_B = 512
D_IN = 2 * W_A + 3 * W_B
POOL_WINDOWS = (2, 4, 8, 16)
N_POOL_GROUPS = len(POOL_WINDOWS)
POOL_GROUP_DIM = W_A // N_POOL_GROUPS
POOL_BUF = max(POOL_WINDOWS) - 1
CHUNK = 128
N_HEADS_B = 4
HEAD_DIM_B = W_B // N_HEADS_B
EPS = 1e-6
SAMPLE_PAST_LEN = 16384

COL_A_IN = 0
COL_A_GATE = W_A
COL_U = 2 * W_A
COL_V = 2 * W_A + W_B
COL_B_GATE = 2 * W_A + 2 * W_B

HIST = 16
ROW_TILE = 512
ROW_BLOCKS = 2
WEIGHT_CHUNK_COLS = 512
WEIGHT_SLOTS = 5
VMEM_LIMIT_BYTES = 56 * 1024 * 1024

BF16 = jnp.bfloat16
F32 = jnp.float32


def _dot(a, b):
    return jnp.dot(a.astype(BF16), b.astype(BF16), preferred_element_type=F32)


def _rmsnorm(x, g):
    ms = jnp.mean(x * x, axis=-1, keepdims=True)
    return (x * lax.rsqrt(ms + EPS)) * g


def _silu(x):
    hx = 0.5 * x
    return hx * jnp.tanh(hx) + hx


def _head_rmsnorm(v, g):
    parts = []
    for h in range(N_HEADS_B):
        sl = slice(h * HEAD_DIM_B, (h + 1) * HEAD_DIM_B)
        parts.append(_rmsnorm(v[:, sl], g[:, sl]))
    return jnp.concatenate(parts, axis=-1)


def _masked_ws(ws_ref, l):
    row = lax.broadcasted_iota(jnp.int32, (CHUNK, CHUNK), 0)
    col = lax.broadcasted_iota(jnp.int32, (CHUNK, CHUNK), 1)
    keep = col <= row
    return [jnp.where(keep, ws_ref[l, h], 0.0).astype(BF16) for h in range(N_HEADS_B)]


class _Refs:
    def __init__(self, **kw):
        self.__dict__.update(kw)


def _row(ref, l):
    return ref[l:l + 1, :]


class _WeightStream:
    def __init__(self, w_in_hbm, w_out_hbm, w_in_bf, w_out_bf, stage, sems, after_last_start):
        self.chunks = []
        for l in range(DEPTH):
            for col in (COL_V, COL_A_IN, COL_A_GATE, COL_U, COL_B_GATE):
                self.chunks.append((w_in_hbm, w_in_bf, l, col))
            for col in range(0, w_out_hbm.shape[2], WEIGHT_CHUNK_COLS):
                self.chunks.append((w_out_hbm, w_out_bf, l, col))
        self.index = {(id(dst), l, col): i for i, (_, dst, l, col) in enumerate(self.chunks)}
        self.stage, self.sems = stage, sems
        self.after_last_start = after_last_start
        self.done = 0

    def _copy(self, i):
        src, _, l, col = self.chunks[i]
        slot = i % WEIGHT_SLOTS
        return pltpu.make_async_copy(
            src.at[l, :, pl.ds(col, WEIGHT_CHUNK_COLS)], self.stage.at[slot], self.sems.at[slot])

    def _start(self, i):
        self._copy(i).start()
        if i == len(self.chunks) - 1:
            self.after_last_start()

    def start(self):
        for i in range(WEIGHT_SLOTS - 1):
            self._start(i)

    def service(self):
        i = self.done
        if i == len(self.chunks):
            return
        if i + WEIGHT_SLOTS - 1 < len(self.chunks):
            self._start(i + WEIGHT_SLOTS - 1)
        self._copy(i).wait()
        _, dst, l, col = self.chunks[i]
        dst[l, :, col:col + WEIGHT_CHUNK_COLS] = self.stage[i % WEIGHT_SLOTS].astype(BF16)
        self.done += 1

    def need(self, dst, l, col):
        while self.done <= self.index[(id(dst), l, col)]:
            self.service()


class _Chain:
    def __init__(self, l, r0, rows, t, tm, R, need=None):
        self.l, self.r0, self.rows, self.t, self.tm, self.R = l, r0, rows, t, tm, R
        self.need = need if need is not None else (lambda dst, l, col: None)

    def _in(self, col, width):
        R = self.R
        self.need(R.w_in, self.l, col)
        return jnp.dot(R.h_buf[self.r0:self.r0 + self.rows, :], R.w_in[self.l, :, col:col + width],
                       preferred_element_type=F32)

    def p0(self):
        self.v = self._in(COL_V, W_B)
        self.a_in = self._in(COL_A_IN, W_A)

    def p1(self):
        self.a_gate = self._in(COL_A_GATE, W_A)
        self.u = self._in(COL_U, W_B)

    def p2(self):
        l, r0, rows, R = self.l, self.r0, self.rows, self.R
        n_chunks = rows // CHUNK
        a_in = self.a_in
        base = HIST + r0

        vn = _head_rmsnorm(self.v, _row(R.v_norm_g, l))
        self.cv_rows = vn[rows - CHUNK:, :]
        vn_b = vn.astype(BF16)

        a_hist, s2_hist, s4_hist, s8_hist = R.hists
        a_hist[base:base + rows, :] = a_in
        s2 = a_in + a_hist[base - 1:base - 1 + rows, :]
        s2_hist[base:base + rows, :] = s2[:, POOL_GROUP_DIM:]
        s4 = s2[:, POOL_GROUP_DIM:] + s2_hist[base - 2:base - 2 + rows, :]
        s4_hist[base:base + rows, :] = s4[:, POOL_GROUP_DIM:]
        s8 = s4[:, POOL_GROUP_DIM:] + s4_hist[base - 4:base - 4 + rows, :]
        s8_hist[base:base + rows, :] = s8[:, POOL_GROUP_DIM:]
        s16 = s8[:, POOL_GROUP_DIM:] + s8_hist[base - 8:base - 8 + rows, :]
        win_sums = (s2[:, :POOL_GROUP_DIM], s4[:, :POOL_GROUP_DIM], s8[:, :POOL_GROUP_DIM], s16)
        self.pool_rows = a_in[rows - POOL_BUF:, :]

        d_parts = []
        for g, w in enumerate(POOL_WINDOWS):
            sl = slice(g * POOL_GROUP_DIM, (g + 1) * POOL_GROUP_DIM)
            ssum = win_sums[g]
            if r0 == 0:
                pos1 = lax.broadcasted_iota(jnp.int32, (HIST, POOL_GROUP_DIM), 0) + self.t * self.tm + 1
                inv_head = 1.0 / jnp.minimum(pos1, w).astype(F32)
                mean = jnp.concatenate([ssum[:HIST] * inv_head, ssum[HIST:] * (1.0 / w)], axis=0)
            else:
                mean = ssum * (1.0 / w)
            d_parts.append((mean - a_in[:, sl]).astype(BF16))

        ws_b = _masked_ws(R.ws, l)
        s_heads = []
        for hh in range(N_HEADS_B):
            sl = slice(hh * HEAD_DIM_B, (hh + 1) * HEAD_DIM_B)
            rhs = jnp.concatenate([vn_b[c * CHUNK:(c + 1) * CHUNK, sl] for c in range(n_chunks)], axis=1)
            res = jnp.dot(ws_b[hh], rhs, preferred_element_type=F32)
            s_heads.append(jnp.concatenate(
                [res[:, c * HEAD_DIM_B:(c + 1) * HEAD_DIM_B] for c in range(n_chunks)], axis=0))
        self.s_heads = s_heads

        self.y_parts = [_dot(d_parts[g], R.w_pool[l, g]) for g in range(N_POOL_GROUPS)]
        self.b_gate = self._in(COL_B_GATE, W_B)

    def p3(self):
        l, rows, R = self.l, self.rows, self.R
        n_chunks = rows // CHUNK
        self.need(R.w_out, l, R.w_out.shape[2] - WEIGHT_CHUNK_COLS)
        a_out = ((jnp.concatenate(self.y_parts, axis=-1) * _row(R.pool_scale, l)) * _silu(self.a_gate)).astype(BF16)
        out_a = _dot(a_out, R.w_out[l, 0:W_A, :])
        s = jnp.concatenate(self.s_heads, axis=-1) + jnp.concatenate([R.bias[l]] * n_chunks, axis=0)
        b_out = ((self.u * s) * _silu(self.b_gate)).astype(BF16)
        self.out = out_a + _dot(b_out, R.w_out[l, W_A:, :])


def _per_head_lanes(vals):
    return jnp.concatenate([jnp.broadcast_to(v, (1, HEAD_DIM_B)) for v in vals], axis=-1)


def _sample_group(xs_ref, sp_ref, norm_g_ref, b_s_ref, fin_g, R, ys_ref, a_in_s_ref, vn_s_ref, before_pool):
    x = xs_ref[...]
    for l in range(DEPTH):
        h = _rmsnorm(x, _row(norm_g_ref, l)).astype(BF16)
        a_in = _dot(h, R.w_in[l, :, COL_A_IN:COL_A_IN + W_A])
        a_in_s_ref[l] = a_in
        a_gate = _dot(h, R.w_in[l, :, COL_A_GATE:COL_A_GATE + W_A])

        v = _dot(h, R.w_in[l, :, COL_V:COL_V + W_B])
        vn = _head_rmsnorm(v, _row(R.v_norm_g, l))
        vn_s_ref[l] = vn
        ws00 = _per_head_lanes([R.ws[l, hh, 0:1, 0:1] for hh in range(N_HEADS_B)])
        b0 = _per_head_lanes([b_s_ref[l, hh:hh + 1, 0:1] for hh in range(N_HEADS_B)])
        s = ws00 * vn + b0
        u = _dot(h, R.w_in[l, :, COL_U:COL_U + W_B])
        b_gate = _dot(h, R.w_in[l, :, COL_B_GATE:COL_B_GATE + W_B])
        b_out = (u * s) * _silu(b_gate)

        before_pool(l)
        a_parts = []
        for g, w in enumerate(POOL_WINDOWS):
            sl = slice(g * POOL_GROUP_DIM, (g + 1) * POOL_GROUP_DIM)
            ssum = a_in[:, sl]
            for k in range(1, w):
                ssum = ssum + sp_ref[l, POOL_BUF - k, :, sl]
            cnt = float(min(SAMPLE_PAST_LEN + 1, w))
            d = ssum / cnt - a_in[:, sl]
            yg = _dot(d, R.w_pool[l, g]) * _row(R.pool_scale, l)[:, sl]
            a_parts.append(yg * _silu(a_gate[:, sl]))
        a_out = jnp.concatenate(a_parts, axis=-1)

        x = x + (_dot(a_out, R.w_out[l, 0:W_A, :]) + _dot(b_out, R.w_out[l, W_A:, :]))
    ys_ref[...] = _rmsnorm(x, fin_g)


def _trunk_kernel(x_ref, xs_hbm, sp_hbm, norm_g_ref, w_in_hbm, w_pool_ref, pool_scale_ref, v_norm_g_ref,
                  ws_ref, b_s_ref, w_out_hbm, fin_g_ref,
                  y_ref, pool_ref, cv_ref, ys_hbm, pool_s_t_hbm, vn_s_hbm,
                  a_hist, s2_hist, s4_hist, s8_hist, a_carry, s2_carry, s4_carry, s8_carry,
                  h_buf, x_buf, w_in_bf, w_out_bf, bias_buf, sp_buf,
                  xs_buf, ys_buf, a_in_s_buf, vn_s_buf, stage, sems, io_sems, *, single_step):
    b = pl.program_id(0)
    t = pl.program_id(1)
    tm = x_ref.shape[1]
    rows = tm // ROW_BLOCKS
    R = _Refs(w_in=w_in_bf, w_pool=w_pool_ref, pool_scale=pool_scale_ref, v_norm_g=v_norm_g_ref, ws=ws_ref,
              bias=bias_buf, w_out=w_out_bf, hists=(a_hist, s2_hist, s4_hist, s8_hist), h_buf=h_buf)
    hists = R.hists
    carries = (a_carry, s2_carry, s4_carry, s8_carry)
    fin_g = fin_g_ref[...].reshape(1, D_MODEL)

    def tile_body(need=None, between_phases=None):
        for l in range(DEPTH):

            def x_rows(rs, l=l):
                return x_ref[0, rs, :] if l == 0 else x_buf[rs, :]

            def phase_done():
                if between_phases is not None:
                    between_phases()

            blocks = [slice(k * rows, (k + 1) * rows) for k in range(ROW_BLOCKS)]
            for rs in blocks:
                h_buf[rs, :] = _rmsnorm(x_rows(rs), _row(norm_g_ref, l)).astype(BF16)
            chains = [_Chain(l, k * rows, rows, t, tm, R, need) for k in range(ROW_BLOCKS)]
            for c in chains:
                c.p0()
            phase_done()
            for c in chains:
                c.p1()
            phase_done()
            for buf, carry in zip(hists, carries):
                buf[0:HIST, :] = carry[l]
            for c in chains:
                c.p2()
                phase_done()
            for buf, carry in zip(hists, carries):
                carry[l] = buf[tm:tm + HIST, :]
            pool_ref[l, 0] = chains[-1].pool_rows
            cv_ref[l, 0] = chains[-1].cv_rows
            for c, rs in zip(chains, blocks):
                c.p3()
                x_new = x_rows(rs) + c.out
                if l + 1 < DEPTH:
                    x_buf[rs, :] = x_new
                else:
                    y_ref[0, rs, :] = _rmsnorm(x_new, fin_g)
                phase_done()

    def zero_hist():
        for carry in carries:
            carry[...] = jnp.zeros(carry.shape, F32)

    def sample_out_copies():
        return {
            "keep": pltpu.make_async_copy(sp_buf.at[:, pl.ds(1, POOL_BUF - 1)],
                                          pool_s_t_hbm.at[:, pl.ds(0, POOL_BUF - 1)], io_sems.at[2]),
            "new_row": pltpu.make_async_copy(a_in_s_buf, pool_s_t_hbm.at[:, POOL_BUF - 1], io_sems.at[3]),
            "ys": pltpu.make_async_copy(ys_buf, ys_hbm.at[:, 0, :], io_sems.at[4]),
            "vn": pltpu.make_async_copy(vn_s_buf, vn_s_hbm.at[:, :, 0, :], io_sems.at[5]),
        }

    def wait_sample_outs():
        for cp in sample_out_copies().values():
            cp.wait()

    first = jnp.logical_and(b == 0, t == 0)

    @pl.when(first)
    def _():
        xs_copy = pltpu.make_async_copy(xs_hbm.at[:, 0, :], xs_buf, io_sems.at[0])
        sp_copy = pltpu.make_async_copy(sp_hbm, sp_buf, io_sems.at[1])
        outs = sample_out_copies()

        def start_in_copies():
            xs_copy.start()
            sp_copy.start()

        def before_pool(l):
            if l == 0:
                sp_copy.wait()
                outs["keep"].start()

        stream = _WeightStream(w_in_hbm, w_out_hbm, w_in_bf, w_out_bf, stage, sems, start_in_copies)
        stream.start()
        for l in range(DEPTH):
            for hh in range(N_HEADS_B):
                bias_buf[l, :, hh * HEAD_DIM_B:(hh + 1) * HEAD_DIM_B] = jnp.broadcast_to(
                    b_s_ref[l, hh:hh + 1, :], (CHUNK, CHUNK)).T
        zero_hist()

        tile_body(stream.need, stream.service)
        assert stream.done == len(stream.chunks)
        xs_copy.wait()
        _sample_group(xs_buf, sp_buf, norm_g_ref, b_s_ref, fin_g, R, ys_buf, a_in_s_buf, vn_s_buf, before_pool)
        for name in ("ys", "vn", "new_row"):
            outs[name].start()
        if single_step:
            wait_sample_outs()

    @pl.when(jnp.logical_not(first))
    def _():
        pl.when(b * pl.num_programs(1) + t == 1)(wait_sample_outs)
        pl.when(t == 0)(zero_hist)
        tile_body()


def _const_spec(shape):
    zeros = (0,) * len(shape)
    return pl.BlockSpec(shape, lambda b, t: zeros, pipeline_mode=pl.Buffered(1))


def _trunk_call(x, xs, sp_t, norm_g, w_in, w_pool, pool_scale, v_norm_g, w_s, b_s, w_out, fin_g):
    batch, seq, d = x.shape
    n_s = xs.shape[0]
    tm = ROW_TILE
    assert seq % tm == 0 and tm % (ROW_BLOCKS * CHUNK) == 0
    assert W_A == W_B == WEIGHT_CHUNK_COLS and d % WEIGHT_CHUNK_COLS == 0 and d == W_A + W_B
    hbm = pl.BlockSpec(memory_space=pl.ANY)
    args = (x, xs, sp_t, norm_g, w_in, w_pool, pool_scale, v_norm_g, w_s, b_s, w_out, fin_g)
    in_specs = []
    for a in args:
        if a is x:
            in_specs.append(pl.BlockSpec((1, tm, d), lambda b, t: (b, t, 0)))
        elif a is w_in or a is w_out or a is sp_t or a is xs:
            in_specs.append(hbm)
        else:
            in_specs.append(_const_spec(a.shape))
    return pl.pallas_call(
        functools.partial(_trunk_kernel, single_step=(batch * (seq // tm) == 1)),
        grid=(batch, seq // tm),
        in_specs=in_specs,
        out_specs=[
            pl.BlockSpec((1, tm, d), lambda b, t: (b, t, 0)),
            pl.BlockSpec((DEPTH, 1, POOL_BUF, W_A), lambda b, t: (0, b, 0, 0)),
            pl.BlockSpec((DEPTH, 1, CHUNK, W_B), lambda b, t: (0, b, 0, 0)),
            hbm, hbm, hbm,
        ],
        out_shape=[
            jax.ShapeDtypeStruct((batch, seq, d), F32),
            jax.ShapeDtypeStruct((DEPTH, batch, POOL_BUF, W_A), F32),
            jax.ShapeDtypeStruct((DEPTH, batch, CHUNK, W_B), F32),
            jax.ShapeDtypeStruct((n_s, 1, d), F32),
            jax.ShapeDtypeStruct((DEPTH, POOL_BUF, n_s, W_A), F32),
            jax.ShapeDtypeStruct((DEPTH, n_s, 1, W_B), F32),
        ],
        scratch_shapes=[
            pltpu.VMEM((HIST + tm, W_A), F32),
            pltpu.VMEM((HIST + tm, W_A - POOL_GROUP_DIM), F32),
            pltpu.VMEM((HIST + tm, W_A - 2 * POOL_GROUP_DIM), F32),
            pltpu.VMEM((HIST + tm, W_A - 3 * POOL_GROUP_DIM), F32),
            pltpu.VMEM((DEPTH, HIST, W_A), F32),
            pltpu.VMEM((DEPTH, HIST, W_A - POOL_GROUP_DIM), F32),
            pltpu.VMEM((DEPTH, HIST, W_A - 2 * POOL_GROUP_DIM), F32),
            pltpu.VMEM((DEPTH, HIST, W_A - 3 * POOL_GROUP_DIM), F32),
            pltpu.VMEM((tm, d), BF16),
            pltpu.VMEM((tm, d), F32),
            pltpu.VMEM((DEPTH, d, D_IN), BF16),
            pltpu.VMEM((DEPTH, W_A + W_B, d), BF16),
            pltpu.VMEM((DEPTH, CHUNK, W_B), F32),
            pltpu.VMEM((DEPTH, POOL_BUF, n_s, W_A), F32),
            pltpu.VMEM((n_s, d), F32),
            pltpu.VMEM((n_s, d), F32),
            pltpu.VMEM((DEPTH, n_s, W_A), F32),
            pltpu.VMEM((DEPTH, n_s, W_B), F32),
            pltpu.VMEM((WEIGHT_SLOTS, d, WEIGHT_CHUNK_COLS), F32),
            pltpu.SemaphoreType.DMA((WEIGHT_SLOTS,)),
            pltpu.SemaphoreType.DMA((6,)),
        ],
        compiler_params=pltpu.CompilerParams(
            dimension_semantics=("arbitrary", "arbitrary"),
            vmem_limit_bytes=VMEM_LIMIT_BYTES,
        ),
        name="trunk",
    )(*args)


def kernel(x_prompt, x_sample, state_pool, norm_g, w_in, w_pool, pool_scale, v_norm_g, w_s, b_s, w_out, final_norm_g):
    assert x_sample.shape[1] == 1 and state_pool.shape[2] == POOL_BUF
    sp_t = jnp.transpose(state_pool, (0, 2, 1, 3))
    y_prompt, pool_prompt, chunk_v_prompt, y_sample, pool_sample_t, chunk_v_sample = _trunk_call(
        x_prompt, x_sample, sp_t, norm_g, w_in, w_pool, pool_scale, v_norm_g, w_s, b_s, w_out, final_norm_g)
    pool_sample = jnp.transpose(pool_sample_t, (0, 2, 1, 3))
    return (y_prompt, y_sample, pool_prompt, pool_sample, chunk_v_prompt, chunk_v_sample)
```

```python
import functools

import jax
import jax.numpy as jnp
from jax import lax
from jax.experimental import pallas as pl
from jax.experimental.pallas import tpu as pltpu

D_MODEL = 1024
DEPTH = 2
W_A = 512
W_B = 512
D_IN = 2 * W_A + 3 * W_B
POOL_WINDOWS = (2, 4, 8, 16)
N_POOL_GROUPS = len(POOL_WINDOWS)
POOL_GROUP_DIM = W_A // N_POOL_GROUPS
POOL_BUF = max(POOL_WINDOWS) - 1
CHUNK = 128
N_HEADS_B = 4
HEAD_DIM_B = W_B // N_HEADS_B
EPS = 1e-6
SAMPLE_PAST_LEN = 16384

COL_A_IN = 0
COL_A_GATE = W_A
COL_U = 2 * W_A
COL_V = 2 * W_A + W_B
COL_B_GATE = 2 * W_A + 2 * W_B

HIST = 16
ROW_TILE = 512
ROW_BLOCKS = 2
WEIGHT_CHUNK_COLS = 512
WEIGHT_SLOTS = 5
VMEM_LIMIT_BYTES = 56 * 1024 * 1024

BF16 = jnp.bfloat16
F32 = jnp.float32


def _dot(a, b):
    return jnp.dot(a.astype(BF16), b.astype(BF16), preferred_element_type=F32)


def _rmsnorm(x, g):
    ms = jnp.mean(x * x, axis=-1, keepdims=True)
    return (x * lax.rsqrt(ms + EPS)) * g


def _silu(x):
    hx = 0.5 * x
    return hx * jnp.tanh(hx) + hx


def _head_rmsnorm(v, g):
    parts = []
    for h in range(N_HEADS_B):
        sl = slice(h * HEAD_DIM_B, (h + 1) * HEAD_DIM_B)
        parts.append(_rmsnorm(v[:, sl], g[:, sl]))
    return jnp.concatenate(parts, axis=-1)


def _masked_ws(ws_ref, l):
    row = lax.broadcasted_iota(jnp.int32, (CHUNK, CHUNK), 0)
    col = lax.broadcasted_iota(jnp.int32, (CHUNK, CHUNK), 1)
    keep = col <= row
    return [jnp.where(keep, ws_ref[l, h], 0.0).astype(BF16) for h in range(N_HEADS_B)]


class _Refs:
    def __init__(self, **kw):
        self.__dict__.update(kw)


def _row(ref, l):
    return ref[l:l + 1, :]


class _WeightStream:
    def __init__(self, w_in_hbm, w_out_hbm, w_in_bf, w_out_bf, stage, sems, after_last_start):
        self.chunks = []
        for l in range(DEPTH):
            for col in (COL_V, COL_A_IN, COL_A_GATE, COL_U, COL_B_GATE):
                self.chunks.append((w_in_hbm, w_in_bf, l, col))
            for col in range(0, w_out_hbm.shape[2], WEIGHT_CHUNK_COLS):
                self.chunks.append((w_out_hbm, w_out_bf, l, col))
        self.index = {(id(dst), l, col): i for i, (_, dst, l, col) in enumerate(self.chunks)}
        self.stage, self.sems = stage, sems
        self.after_last_start = after_last_start
        self.done = 0

    def _copy(self, i):
        src, _, l, col = self.chunks[i]
        slot = i % WEIGHT_SLOTS
        return pltpu.make_async_copy(
            src.at[l, :, pl.ds(col, WEIGHT_CHUNK_COLS)], self.stage.at[slot], self.sems.at[slot])

    def _start(self, i):
        self._copy(i).start()
        if i == len(self.chunks) - 1:
            self.after_last_start()

    def start(self):
        for i in range(WEIGHT_SLOTS - 1):
            self._start(i)

    def service(self):
        i = self.done
        if i == len(self.chunks):
            return
        if i + WEIGHT_SLOTS - 1 < len(self.chunks):
            self._start(i + WEIGHT_SLOTS - 1)
        self._copy(i).wait()
        _, dst, l, col = self.chunks[i]
        dst[l, :, col:col + WEIGHT_CHUNK_COLS] = self.stage[i % WEIGHT_SLOTS].astype(BF16)
        self.done += 1

    def need(self, dst, l, col):
        while self.done <= self.index[(id(dst), l, col)]:
            self.service()


class _Chain:
    def __init__(self, l, r0, rows, t, tm, R, need=None):
        self.l, self.r0, self.rows, self.t, self.tm, self.R = l, r0, rows, t, tm, R
        self.need = need if need is not None else (lambda dst, l, col: None)

    def _in(self, col, width):
        R = self.R
        self.need(R.w_in, self.l, col)
        return jnp.dot(R.h_buf[self.r0:self.r0 + self.rows, :], R.w_in[self.l, :, col:col + width],
                       preferred_element_type=F32)

    def p0(self):
        self.v = self._in(COL_V, W_B)
        self.a_in = self._in(COL_A_IN, W_A)

    def p1(self):
        self.a_gate = self._in(COL_A_GATE, W_A)
        self.u = self._in(COL_U, W_B)

    def p2(self):
        l, r0, rows, R = self.l, self.r0, self.rows, self.R
        n_chunks = rows // CHUNK
        a_in = self.a_in
        base = HIST + r0

        vn = _head_rmsnorm(self.v, _row(R.v_norm_g, l))
        self.cv_rows = vn[rows - CHUNK:, :]
        vn_b = vn.astype(BF16)

        a_hist, s2_hist, s4_hist, s8_hist = R.hists
        a_hist[base:base + rows, :] = a_in
        s2 = a_in + a_hist[base - 1:base - 1 + rows, :]
        s2_hist[base:base + rows, :] = s2[:, POOL_GROUP_DIM:]
        s4 = s2[:, POOL_GROUP_DIM:] + s2_hist[base - 2:base - 2 + rows, :]
        s4_hist[base:base + rows, :] = s4[:, POOL_GROUP_DIM:]
        s8 = s4[:, POOL_GROUP_DIM:] + s4_hist[base - 4:base - 4 + rows, :]
        s8_hist[base:base + rows, :] = s8[:, POOL_GROUP_DIM:]
        s16 = s8[:, POOL_GROUP_DIM:] + s8_hist[base - 8:base - 8 + rows, :]
        win_sums = (s2[:, :POOL_GROUP_DIM], s4[:, :POOL_GROUP_DIM], s8[:, :POOL_GROUP_DIM], s16)
        self.pool_rows = a_in[rows - POOL_BUF:, :]

        d_parts = []
        for g, w in enumerate(POOL_WINDOWS):
            sl = slice(g * POOL_GROUP_DIM, (g + 1) * POOL_GROUP_DIM)
            ssum = win_sums[g]
            if r0 == 0:
                pos1 = lax.broadcasted_iota(jnp.int32, (HIST, POOL_GROUP_DIM), 0) + self.t * self.tm + 1
                inv_head = 1.0 / jnp.minimum(pos1, w).astype(F32)
                mean = jnp.concatenate([ssum[:HIST] * inv_head, ssum[HIST:] * (1.0 / w)], axis=0)
            else:
                mean = ssum * (1.0 / w)
            d_parts.append((mean - a_in[:, sl]).astype(BF16))

        ws_b = _masked_ws(R.ws, l)
        s_heads = []
        for hh in range(N_HEADS_B):
            sl = slice(hh * HEAD_DIM_B, (hh + 1) * HEAD_DIM_B)
            rhs = jnp.concatenate([vn_b[c * CHUNK:(c + 1) * CHUNK, sl] for c in range(n_chunks)], axis=1)
            res = jnp.dot(ws_b[hh], rhs, preferred_element_type=F32)
            s_heads.append(jnp.concatenate(
                [res[:, c * HEAD_DIM_B:(c + 1) * HEAD_DIM_B] for c in range(n_chunks)], axis=0))
        self.s_heads = s_heads

        self.y_parts = [_dot(d_parts[g], R.w_pool[l, g]) for g in range(N_POOL_GROUPS)]
        self.b_gate = self._in(COL_B_GATE, W_B)

    def p3(self):
        l, rows, R = self.l, self.rows, self.R
        n_chunks = rows // CHUNK
        self.need(R.w_out, l, R.w_out.shape[2] - WEIGHT_CHUNK_COLS)
        a_out = ((jnp.concatenate(self.y_parts, axis=-1) * _row(R.pool_scale, l)) * _silu(self.a_gate)).astype(BF16)
        out_a = _dot(a_out, R.w_out[l, 0:W_A, :])
        s = jnp.concatenate(self.s_heads, axis=-1) + jnp.concatenate([R.bias[l]] * n_chunks, axis=0)
        b_out = ((self.u * s) * _silu(self.b_gate)).astype(BF16)
        self.out = out_a + _dot(b_out, R.w_out[l, W_A:, :])


def _per_head_lanes(vals):
    return jnp.concatenate([jnp.broadcast_to(v, (1, HEAD_DIM_B)) for v in vals], axis=-1)


def _sample_group(xs_ref, sp_ref, norm_g_ref, b_s_ref, fin_g, R, ys_ref, a_in_s_ref, vn_s_ref, before_pool):
    x = xs_ref[...]
    for l in range(DEPTH):
        h = _rmsnorm(x, _row(norm_g_ref, l)).astype(BF16)
        a_in = _dot(h, R.w_in[l, :, COL_A_IN:COL_A_IN + W_A])
        a_in_s_ref[l] = a_in
        a_gate = _dot(h, R.w_in[l, :, COL_A_GATE:COL_A_GATE + W_A])

        v = _dot(h, R.w_in[l, :, COL_V:COL_V + W_B])
        vn = _head_rmsnorm(v, _row(R.v_norm_g, l))
        vn_s_ref[l] = vn
        ws00 = _per_head_lanes([R.ws[l, hh, 0:1, 0:1] for hh in range(N_HEADS_B)])
        b0 = _per_head_lanes([b_s_ref[l, hh:hh + 1, 0:1] for hh in range(N_HEADS_B)])
        s = ws00 * vn + b0
        u = _dot(h, R.w_in[l, :, COL_U:COL_U + W_B])
        b_gate = _dot(h, R.w_in[l, :, COL_B_GATE:COL_B_GATE + W_B])
        b_out = (u * s) * _silu(b_gate)

        before_pool(l)
        a_parts = []
        for g, w in enumerate(POOL_WINDOWS):
            sl = slice(g * POOL_GROUP_DIM, (g + 1) * POOL_GROUP_DIM)
            ssum = a_in[:, sl]
            for k in range(1, w):
                ssum = ssum + sp_ref[l, POOL_BUF - k, :, sl]
            cnt = float(min(SAMPLE_PAST_LEN + 1, w))
            d = ssum / cnt - a_in[:, sl]
            yg = _dot(d, R.w_pool[l, g]) * _row(R.pool_scale, l)[:, sl]
            a_parts.append(yg * _silu(a_gate[:, sl]))
        a_out = jnp.concatenate(a_parts, axis=-1)

        x = x + (_dot(a_out, R.w_out[l, 0:W_A, :]) + _dot(b_out, R.w_out[l, W_A:, :]))
    ys_ref[...] = _rmsnorm(x, fin_g)


def _trunk_kernel(x_ref, xs_hbm, sp_hbm, norm_g_ref, w_in_hbm, w_pool_ref, pool_scale_ref, v_norm_g_ref,
                  ws_ref, b_s_ref, w_out_hbm, fin_g_ref,
                  y_ref, pool_ref, cv_ref, ys_hbm, pool_s_t_hbm, vn_s_hbm,
                  a_hist, s2_hist, s4_hist, s8_hist, a_carry, s2_carry, s4_carry, s8_carry,
                  h_buf, x_buf, w_in_bf, w_out_bf, bias_buf, sp_buf,
                  xs_buf, ys_buf, a_in_s_buf, vn_s_buf, stage, sems, io_sems, sp_sems, keep_sems,
                  *, single_step):
    b = pl.program_id(0)
    t = pl.program_id(1)
    tm = x_ref.shape[1]
    rows = tm // ROW_BLOCKS
    R = _Refs(w_in=w_in_bf, w_pool=w_pool_ref, pool_scale=pool_scale_ref, v_norm_g=v_norm_g_ref, ws=ws_ref,
              bias=bias_buf, w_out=w_out_bf, hists=(a_hist, s2_hist, s4_hist, s8_hist), h_buf=h_buf)
    hists = R.hists
    carries = (a_carry, s2_carry, s4_carry, s8_carry)
    fin_g = fin_g_ref[...].reshape(1, D_MODEL)

    def tile_body(need=None, between_phases=None):
        for l in range(DEPTH):

            def x_rows(rs, l=l):
                return x_ref[0, rs, :] if l == 0 else x_buf[rs, :]

            def phase_done():
                if between_phases is not None:
                    between_phases()

            blocks = [slice(k * rows, (k + 1) * rows) for k in range(ROW_BLOCKS)]
            for rs in blocks:
                h_buf[rs, :] = _rmsnorm(x_rows(rs), _row(norm_g_ref, l)).astype(BF16)
            chains = [_Chain(l, k * rows, rows, t, tm, R, need) for k in range(ROW_BLOCKS)]
            for c in chains:
                c.p0()
            phase_done()
            for c in chains:
                c.p1()
            phase_done()
            for buf, carry in zip(hists, carries):
                buf[0:HIST, :] = carry[l]
            for c in chains:
                c.p2()
                phase_done()
            for buf, carry in zip(hists, carries):
                carry[l] = buf[tm:tm + HIST, :]
            pool_ref[l, 0] = chains[-1].pool_rows
            cv_ref[l, 0] = chains[-1].cv_rows
            for c, rs in zip(chains, blocks):
                c.p3()
                x_new = x_rows(rs) + c.out
                if l + 1 < DEPTH:
                    x_buf[rs, :] = x_new
                else:
                    y_ref[0, rs, :] = _rmsnorm(x_new, fin_g)
                phase_done()

    def zero_hist():
        for carry in carries:
            carry[...] = jnp.zeros(carry.shape, F32)

    def sample_out_copies():
        return {
            "keep": [pltpu.make_async_copy(sp_buf.at[l, pl.ds(1, POOL_BUF - 1)],
                                           pool_s_t_hbm.at[l, pl.ds(0, POOL_BUF - 1)], keep_sems.at[l])
                     for l in range(DEPTH)],
            "new_row": pltpu.make_async_copy(a_in_s_buf, pool_s_t_hbm.at[:, POOL_BUF - 1], io_sems.at[1]),
            "ys": pltpu.make_async_copy(ys_buf, ys_hbm.at[:, 0, :], io_sems.at[2]),
            "vn": pltpu.make_async_copy(vn_s_buf, vn_s_hbm.at[:, :, 0, :], io_sems.at[3]),
        }

    def wait_sample_outs():
        outs = sample_out_copies()
        for cp in outs["keep"] + [outs["new_row"], outs["ys"], outs["vn"]]:
            cp.wait()

    first = jnp.logical_and(b == 0, t == 0)

    @pl.when(first)
    def _():
        xs_copy = pltpu.make_async_copy(xs_hbm.at[:, 0, :], xs_buf, io_sems.at[0])
        sp_copies = [pltpu.make_async_copy(sp_hbm.at[l], sp_buf.at[l], sp_sems.at[l]) for l in range(DEPTH)]
        outs = sample_out_copies()

        def start_in_copies():
            xs_copy.start()
            for cp in sp_copies:
                cp.start()

        def before_pool(l):
            sp_copies[l].wait()
            outs["keep"][l].start()

        stream = _WeightStream(w_in_hbm, w_out_hbm, w_in_bf, w_out_bf, stage, sems, start_in_copies)
        stream.start()
        for l in range(DEPTH):
            for hh in range(N_HEADS_B):
                bias_buf[l, :, hh * HEAD_DIM_B:(hh + 1) * HEAD_DIM_B] = jnp.broadcast_to(
                    b_s_ref[l, hh:hh + 1, :], (CHUNK, CHUNK)).T
        zero_hist()

        tile_body(stream.need, stream.service)
        assert stream.done == len(stream.chunks)
        xs_copy.wait()
        _sample_group(xs_buf, sp_buf, norm_g_ref, b_s_ref, fin_g, R, ys_buf, a_in_s_buf, vn_s_buf, before_pool)
        for name in ("ys", "vn", "new_row"):
            outs[name].start()
        if single_step:
            wait_sample_outs()

    @pl.when(jnp.logical_not(first))
    def _():
        pl.when(b * pl.num_programs(1) + t == 1)(wait_sample_outs)
        pl.when(t == 0)(zero_hist)
        tile_body()


def _const_spec(shape):
    zeros = (0,) * len(shape)
    return pl.BlockSpec(shape, lambda b, t: zeros, pipeline_mode=pl.Buffered(1))


def _trunk_call(x, xs, sp_t, norm_g, w_in, w_pool, pool_scale, v_norm_g, w_s, b_s, w_out, fin_g):
    batch, seq, d = x.shape
    n_s = xs.shape[0]
    tm = ROW_TILE
    assert seq % tm == 0 and tm % (ROW_BLOCKS * CHUNK) == 0
    assert W_A == W_B == WEIGHT_CHUNK_COLS and d % WEIGHT_CHUNK_COLS == 0 and d == W_A + W_B
    hbm = pl.BlockSpec(memory_space=pl.ANY)
    args = (x, xs, sp_t, norm_g, w_in, w_pool, pool_scale, v_norm_g, w_s, b_s, w_out, fin_g)
    in_specs = []
    for a in args:
        if a is x:
            in_specs.append(pl.BlockSpec((1, tm, d), lambda b, t: (b, t, 0)))
        elif a is w_in or a is w_out or a is sp_t or a is xs:
            in_specs.append(hbm)
        else:
            in_specs.append(_const_spec(a.shape))
    return pl.pallas_call(
        functools.partial(_trunk_kernel, single_step=(batch * (seq // tm) == 1)),
        grid=(batch, seq // tm),
        in_specs=in_specs,
        out_specs=[
            pl.BlockSpec((1, tm, d), lambda b, t: (b, t, 0)),
            pl.BlockSpec((DEPTH, 1, POOL_BUF, W_A), lambda b, t: (0, b, 0, 0)),
            pl.BlockSpec((DEPTH, 1, CHUNK, W_B), lambda b, t: (0, b, 0, 0)),
            hbm, hbm, hbm,
        ],
        out_shape=[
            jax.ShapeDtypeStruct((batch, seq, d), F32),
            jax.ShapeDtypeStruct((DEPTH, batch, POOL_BUF, W_A), F32),
            jax.ShapeDtypeStruct((DEPTH, batch, CHUNK, W_B), F32),
            jax.ShapeDtypeStruct((n_s, 1, d), F32),
            jax.ShapeDtypeStruct((DEPTH, POOL_BUF, n_s, W_A), F32),
            jax.ShapeDtypeStruct((DEPTH, n_s, 1, W_B), F32),
        ],
        scratch_shapes=[
            pltpu.VMEM((HIST + tm, W_A), F32),
            pltpu.VMEM((HIST + tm, W_A - POOL_GROUP_DIM), F32),
            pltpu.VMEM((HIST + tm, W_A - 2 * POOL_GROUP_DIM), F32),
            pltpu.VMEM((HIST + tm, W_A - 3 * POOL_GROUP_DIM), F32),
            pltpu.VMEM((DEPTH, HIST, W_A), F32),
            pltpu.VMEM((DEPTH, HIST, W_A - POOL_GROUP_DIM), F32),
            pltpu.VMEM((DEPTH, HIST, W_A - 2 * POOL_GROUP_DIM), F32),
            pltpu.VMEM((DEPTH, HIST, W_A - 3 * POOL_GROUP_DIM), F32),
            pltpu.VMEM((tm, d), BF16),
            pltpu.VMEM((tm, d), F32),
            pltpu.VMEM((DEPTH, d, D_IN), BF16),
            pltpu.VMEM((DEPTH, W_A + W_B, d), BF16),
            pltpu.VMEM((DEPTH, CHUNK, W_B), F32),
            pltpu.VMEM((DEPTH, POOL_BUF, n_s, W_A), F32),
            pltpu.VMEM((n_s, d), F32),
            pltpu.VMEM((n_s, d), F32),
            pltpu.VMEM((DEPTH, n_s, W_A), F32),
            pltpu.VMEM((DEPTH, n_s, W_B), F32),
            pltpu.VMEM((WEIGHT_SLOTS, d, WEIGHT_CHUNK_COLS), F32),
            pltpu.SemaphoreType.DMA((WEIGHT_SLOTS,)),
            pltpu.SemaphoreType.DMA((4,)),
            pltpu.SemaphoreType.DMA((DEPTH,)),
            pltpu.SemaphoreType.DMA((DEPTH,)),
        ],
        compiler_params=pltpu.CompilerParams(
            dimension_semantics=("arbitrary", "arbitrary"),
            vmem_limit_bytes=VMEM_LIMIT_BYTES,
        ),
        name="trunk",
    )(*args)


def kernel(x_prompt, x_sample, state_pool, norm_g, w_in, w_pool, pool_scale, v_norm_g, w_s, b_s, w_out, final_norm_g):
    assert x_sample.shape[1] == 1 and state_pool.shape[2] == POOL_BUF
    sp_t = jnp.transpose(state_pool, (0, 2, 1, 3))
    y_prompt, pool_prompt, chunk_v_prompt, y_sample, pool_sample_t, chunk_v_sample = _trunk_call(
        x_prompt, x_sample, sp_t, norm_g, w_in, w_pool, pool_scale, v_norm_g, w_s, b_s, w_out, final_norm_g)
    pool_sample = jnp.transpose(pool_sample_t, (0, 2, 1, 3))
    return (y_prompt, y_sample, pool_prompt, pool_sample, chunk_v_prompt, chunk_v_sample)
```

```python
import functools

import jax
import jax.numpy as jnp
from jax import lax
from jax.experimental import pallas as pl
from jax.experimental.pallas import tpu as pltpu

D_MODEL = 1024
DEPTH = 2
W_A = 512
W_B = 512
D_IN = 2 * W_A + 3 * W_B
POOL_WINDOWS = (2, 4, 8, 16)
N_POOL_GROUPS = len(POOL_WINDOWS)
POOL_GROUP_DIM = W_A // N_POOL_GROUPS
POOL_BUF = max(POOL_WINDOWS) - 1
CHUNK = 128
N_HEADS_B = 4
HEAD_DIM_B = W_B // N_HEADS_B
EPS = 1e-6
SAMPLE_PAST_LEN = 16384

COL_A_IN = 0
COL_A_GATE = W_A
COL_U = 2 * W_A
COL_V = 2 * W_A + W_B
COL_B_GATE = 2 * W_A + 2 * W_B

HIST = 16
ROW_TILE = 512
ROW_BLOCKS = 2
WEIGHT_CHUNK_COLS = 512
WEIGHT_SLOTS = 5
VMEM_LIMIT_BYTES = 56 * 1024 * 1024

BF16 = jnp.bfloat16
F32 = jnp.float32


def _dot(a, b):
    return jnp.dot(a.astype(BF16), b.astype(BF16), preferred_element_type=F32)


def _rmsnorm(x, g):
    ms = jnp.mean(x * x, axis=-1, keepdims=True)
    return (x * lax.rsqrt(ms + EPS)) * g


def _silu(x):
    hx = 0.5 * x
    return hx * jnp.tanh(hx) + hx


def _head_rmsnorm(v, g):
    parts = []
    for h in range(N_HEADS_B):
        sl = slice(h * HEAD_DIM_B, (h + 1) * HEAD_DIM_B)
        parts.append(_rmsnorm(v[:, sl], g[:, sl]))
    return jnp.concatenate(parts, axis=-1)


def _masked_ws(ws_ref, l):
    row = lax.broadcasted_iota(jnp.int32, (CHUNK, CHUNK), 0)
    col = lax.broadcasted_iota(jnp.int32, (CHUNK, CHUNK), 1)
    keep = col <= row
    return [jnp.where(keep, ws_ref[l, h], 0.0).astype(BF16) for h in range(N_HEADS_B)]


class _Refs:
    def __init__(self, **kw):
        self.__dict__.update(kw)


def _row(ref, l):
    return ref[l:l + 1, :]


class _WeightStream:
    def __init__(self, w_in_hbm, w_out_hbm, w_in_bf, w_out_bf, stage, sems, after_last_start):
        self.chunks = []
        for l in range(DEPTH):
            for col in (COL_V, COL_A_IN, COL_A_GATE, COL_U, COL_B_GATE):
                self.chunks.append((w_in_hbm, w_in_bf, l, col))
            for col in range(0, w_out_hbm.shape[2], WEIGHT_CHUNK_COLS):
                self.chunks.append((w_out_hbm, w_out_bf, l, col))
        self.index = {(id(dst), l, col): i for i, (_, dst, l, col) in enumerate(self.chunks)}
        self.stage, self.sems = stage, sems
        self.after_last_start = after_last_start
        self.done = 0

    def _copy(self, i):
        src, _, l, col = self.chunks[i]
        slot = i % WEIGHT_SLOTS
        return pltpu.make_async_copy(
            src.at[l, :, pl.ds(col, WEIGHT_CHUNK_COLS)], self.stage.at[slot], self.sems.at[slot])

    def _start(self, i):
        self._copy(i).start()
        if i == len(self.chunks) - 1:
            self.after_last_start()

    def start(self):
        for i in range(WEIGHT_SLOTS - 1):
            self._start(i)

    def service(self):
        i = self.done
        if i == len(self.chunks):
            return
        if i + WEIGHT_SLOTS - 1 < len(self.chunks):
            self._start(i + WEIGHT_SLOTS - 1)
        self._copy(i).wait()
        _, dst, l, col = self.chunks[i]
        dst[l, :, col:col + WEIGHT_CHUNK_COLS] = self.stage[i % WEIGHT_SLOTS].astype(BF16)
        self.done += 1

    def need(self, dst, l, col):
        while self.done <= self.index[(id(dst), l, col)]:
            self.service()


class _Chain:
    def __init__(self, l, r0, rows, t, tm, R, need=None):
        self.l, self.r0, self.rows, self.t, self.tm, self.R = l, r0, rows, t, tm, R
        self.need = need if need is not None else (lambda dst, l, col: None)

    def _in(self, col, width):
        R = self.R
        self.need(R.w_in, self.l, col)
        return jnp.dot(R.h_buf[self.r0:self.r0 + self.rows, :], R.w_in[self.l, :, col:col + width],
                       preferred_element_type=F32)

    def p0(self):
        self.v = self._in(COL_V, W_B)
        self.a_in = self._in(COL_A_IN, W_A)

    def p1(self):
        self.a_gate = self._in(COL_A_GATE, W_A)
        self.u = self._in(COL_U, W_B)

    def p2(self):
        l, r0, rows, R = self.l, self.r0, self.rows, self.R
        n_chunks = rows // CHUNK
        a_in = self.a_in
        base = HIST + r0

        vn = _head_rmsnorm(self.v, _row(R.v_norm_g, l))
        self.cv_rows = vn[rows - CHUNK:, :]
        vn_b = vn.astype(BF16)

        a_hist, s2_hist, s4_hist, s8_hist = R.hists
        a_hist[base:base + rows, :] = a_in
        s2 = a_in + a_hist[base - 1:base - 1 + rows, :]
        s2_hist[base:base + rows, :] = s2[:, POOL_GROUP_DIM:]
        s4 = s2[:, POOL_GROUP_DIM:] + s2_hist[base - 2:base - 2 + rows, :]
        s4_hist[base:base + rows, :] = s4[:, POOL_GROUP_DIM:]
        s8 = s4[:, POOL_GROUP_DIM:] + s4_hist[base - 4:base - 4 + rows, :]
        s8_hist[base:base + rows, :] = s8[:, POOL_GROUP_DIM:]
        s16 = s8[:, POOL_GROUP_DIM:] + s8_hist[base - 8:base - 8 + rows, :]
        win_sums = (s2[:, :POOL_GROUP_DIM], s4[:, :POOL_GROUP_DIM], s8[:, :POOL_GROUP_DIM], s16)
        self.pool_rows = a_in[rows - POOL_BUF:, :]

        d_parts = []
        for g, w in enumerate(POOL_WINDOWS):
            sl = slice(g * POOL_GROUP_DIM, (g + 1) * POOL_GROUP_DIM)
            ssum = win_sums[g]
            if r0 == 0:
                pos1 = lax.broadcasted_iota(jnp.int32, (HIST, POOL_GROUP_DIM), 0) + self.t * self.tm + 1
                inv_head = 1.0 / jnp.minimum(pos1, w).astype(F32)
                mean = jnp.concatenate([ssum[:HIST] * inv_head, ssum[HIST:] * (1.0 / w)], axis=0)
            else:
                mean = ssum * (1.0 / w)
            d_parts.append((mean - a_in[:, sl]).astype(BF16))

        ws_b = _masked_ws(R.ws, l)
        s_heads = []
        for hh in range(N_HEADS_B):
            sl = slice(hh * HEAD_DIM_B, (hh + 1) * HEAD_DIM_B)
            rhs = jnp.concatenate([vn_b[c * CHUNK:(c + 1) * CHUNK, sl] for c in range(n_chunks)], axis=1)
            res = jnp.dot(ws_b[hh], rhs, preferred_element_type=F32)
            s_heads.append(jnp.concatenate(
                [res[:, c * HEAD_DIM_B:(c + 1) * HEAD_DIM_B] for c in range(n_chunks)], axis=0))
        self.s_heads = s_heads

        self.y_parts = [_dot(d_parts[g], R.w_pool[l, g]) for g in range(N_POOL_GROUPS)]
        self.b_gate = self._in(COL_B_GATE, W_B)

    def p3(self):
        l, rows, R = self.l, self.rows, self.R
        n_chunks = rows // CHUNK
        self.need(R.w_out, l, R.w_out.shape[2] - WEIGHT_CHUNK_COLS)
        a_out = ((jnp.concatenate(self.y_parts, axis=-1) * _row(R.pool_scale, l)) * _silu(self.a_gate)).astype(BF16)
        out_a = _dot(a_out, R.w_out[l, 0:W_A, :])
        s = jnp.concatenate(self.s_heads, axis=-1) + jnp.concatenate([R.bias[l]] * n_chunks, axis=0)
        b_out = ((self.u * s) * _silu(self.b_gate)).astype(BF16)
        self.out = out_a + _dot(b_out, R.w_out[l, W_A:, :])


def _per_head_lanes(vals):
    return jnp.concatenate([jnp.broadcast_to(v, (1, HEAD_DIM_B)) for v in vals], axis=-1)


def _sample_group(xs_ref, sp_ref, norm_g_ref, b_s_ref, fin_g, R, ys_ref, a_in_s_ref, vn_s_ref, before_pool):
    x = xs_ref[...]
    for l in range(DEPTH):
        h = _rmsnorm(x, _row(norm_g_ref, l)).astype(BF16)
        a_in = _dot(h, R.w_in[l, :, COL_A_IN:COL_A_IN + W_A])
        a_in_s_ref[l] = a_in
        a_gate = _dot(h, R.w_in[l, :, COL_A_GATE:COL_A_GATE + W_A])

        v = _dot(h, R.w_in[l, :, COL_V:COL_V + W_B])
        vn = _head_rmsnorm(v, _row(R.v_norm_g, l))
        vn_s_ref[l] = vn
        ws00 = _per_head_lanes([R.ws[l, hh, 0:1, 0:1] for hh in range(N_HEADS_B)])
        b0 = _per_head_lanes([b_s_ref[l, hh:hh + 1, 0:1] for hh in range(N_HEADS_B)])
        s = ws00 * vn + b0
        u = _dot(h, R.w_in[l, :, COL_U:COL_U + W_B])
        b_gate = _dot(h, R.w_in[l, :, COL_B_GATE:COL_B_GATE + W_B])
        b_out = (u * s) * _silu(b_gate)

        before_pool(l)
        a_parts = []
        for g, w in enumerate(POOL_WINDOWS):
            sl = slice(g * POOL_GROUP_DIM, (g + 1) * POOL_GROUP_DIM)
            ssum = a_in[:, sl]
            for k in range(1, w):
                ssum = ssum + sp_ref[l, POOL_BUF - k, :, sl]
            cnt = float(min(SAMPLE_PAST_LEN + 1, w))
            d = ssum / cnt - a_in[:, sl]
            yg = _dot(d, R.w_pool[l, g]) * _row(R.pool_scale, l)[:, sl]
            a_parts.append(yg * _silu(a_gate[:, sl]))
        a_out = jnp.concatenate(a_parts, axis=-1)

        x = x + (_dot(a_out, R.w_out[l, 0:W_A, :]) + _dot(b_out, R.w_out[l, W_A:, :]))
    ys_ref[...] = _rmsnorm(x, fin_g)


def _trunk_kernel(x_ref, xs_hbm, sp_hbm, norm_g_ref, w_in_hbm, w_pool_ref, pool_scale_ref, v_norm_g_ref,
                  ws_ref, b_s_ref, w_out_hbm, fin_g_ref,
                  y_ref, pool_ref, cv_ref, ys_hbm, pool_s_t_hbm, vn_s_hbm,
                  a_hist, s2_hist, s4_hist, s8_hist, a_carry, s2_carry, s4_carry, s8_carry,
                  h_buf, x_buf, w_in_bf, w_out_bf, bias_buf, sp_buf,
                  xs_buf, ys_buf, a_in_s_buf, vn_s_buf, stage, sems, io_sems, sp_sems, keep_sems,
                  *, single_step):
    b = pl.program_id(0)
    t = pl.program_id(1)
    tm = x_ref.shape[1]
    rows = tm // ROW_BLOCKS
    R = _Refs(w_in=w_in_bf, w_pool=w_pool_ref, pool_scale=pool_scale_ref, v_norm_g=v_norm_g_ref, ws=ws_ref,
              bias=bias_buf, w_out=w_out_bf, hists=(a_hist, s2_hist, s4_hist, s8_hist), h_buf=h_buf)
    hists = R.hists
    carries = (a_carry, s2_carry, s4_carry, s8_carry)
    fin_g = fin_g_ref[...].reshape(1, D_MODEL)

    def tile_body(need=None, between_phases=None):
        for l in range(DEPTH):

            def x_rows(rs, l=l):
                return x_ref[0, rs, :] if l == 0 else x_buf[rs, :]

            def phase_done():
                if between_phases is not None:
                    between_phases()

            blocks = [slice(k * rows, (k + 1) * rows) for k in range(ROW_BLOCKS)]
            for rs in blocks:
                h_buf[rs, :] = _rmsnorm(x_rows(rs), _row(norm_g_ref, l)).astype(BF16)
            chains = [_Chain(l, k * rows, rows, t, tm, R, need) for k in range(ROW_BLOCKS)]
            for c in chains:
                c.p0()
            phase_done()
            for c in chains:
                c.p1()
            phase_done()
            for buf, carry in zip(hists, carries):
                buf[0:HIST, :] = carry[l]
            for c in chains:
                c.p2()
                phase_done()
            for buf, carry in zip(hists, carries):
                carry[l] = buf[tm:tm + HIST, :]
            pool_ref[l, 0] = chains[-1].pool_rows
            cv_ref[l, 0] = chains[-1].cv_rows
            for c, rs in zip(chains, blocks):
                c.p3()
                x_new = x_rows(rs) + c.out
                if l + 1 < DEPTH:
                    x_buf[rs, :] = x_new
                else:
                    y_ref[0, rs, :] = _rmsnorm(x_new, fin_g)
                phase_done()

    def zero_hist():
        for carry in carries:
            carry[...] = jnp.zeros(carry.shape, F32)

    def sample_out_copies():
        return {
            "keep": [pltpu.make_async_copy(sp_buf.at[l, pl.ds(1, POOL_BUF - 1)],
                                           pool_s_t_hbm.at[l, pl.ds(0, POOL_BUF - 1)], keep_sems.at[l])
                     for l in range(DEPTH)],
            "new_row": pltpu.make_async_copy(a_in_s_buf, pool_s_t_hbm.at[:, POOL_BUF - 1], io_sems.at[1]),
            "ys": pltpu.make_async_copy(ys_buf, ys_hbm.at[:, 0, :], io_sems.at[2]),
            "vn": pltpu.make_async_copy(vn_s_buf, vn_s_hbm.at[:, :, 0, :], io_sems.at[3]),
        }

    def wait_sample_outs():
        outs = sample_out_copies()
        for cp in outs["keep"] + [outs["new_row"], outs["ys"], outs["vn"]]:
            cp.wait()

    first = jnp.logical_and(b == 0, t == 0)

    @pl.when(first)
    def _():
        xs_copy = pltpu.make_async_copy(xs_hbm.at[:, 0, :], xs_buf, io_sems.at[0])
        sp_copies = [pltpu.make_async_copy(sp_hbm.at[l], sp_buf.at[l], sp_sems.at[l]) for l in range(DEPTH)]
        outs = sample_out_copies()

        def start_in_copies():
            xs_copy.start()
            sp_copies[0].start()

        def before_pool(l):
            sp_copies[l].wait()
            if l + 1 < DEPTH:
                sp_copies[l + 1].start()

        stream = _WeightStream(w_in_hbm, w_out_hbm, w_in_bf, w_out_bf, stage, sems, start_in_copies)
        stream.start()
        for l in range(DEPTH):
            for hh in range(N_HEADS_B):
                bias_buf[l, :, hh * HEAD_DIM_B:(hh + 1) * HEAD_DIM_B] = jnp.broadcast_to(
                    b_s_ref[l, hh:hh + 1, :], (CHUNK, CHUNK)).T
        zero_hist()

        tile_body(stream.need, stream.service)
        assert stream.done == len(stream.chunks)
        xs_copy.wait()
        _sample_group(xs_buf, sp_buf, norm_g_ref, b_s_ref, fin_g, R, ys_buf, a_in_s_buf, vn_s_buf, before_pool)
        for cp in outs["keep"] + [outs["new_row"], outs["ys"], outs["vn"]]:
            cp.start()
        if single_step:
            wait_sample_outs()

    @pl.when(jnp.logical_not(first))
    def _():
        pl.when(b * pl.num_programs(1) + t == 1)(wait_sample_outs)
        pl.when(t == 0)(zero_hist)
        tile_body()


def _const_spec(shape):
    zeros = (0,) * len(shape)
    return pl.BlockSpec(shape, lambda b, t: zeros, pipeline_mode=pl.Buffered(1))


def _trunk_call(x, xs, sp_t, norm_g, w_in, w_pool, pool_scale, v_norm_g, w_s, b_s, w_out, fin_g):
    batch, seq, d = x.shape
    n_s = xs.shape[0]
    tm = ROW_TILE
    assert seq % tm == 0 and tm % (ROW_BLOCKS * CHUNK) == 0
    assert W_A == W_B == WEIGHT_CHUNK_COLS and d % WEIGHT_CHUNK_COLS == 0 and d == W_A + W_B
    hbm = pl.BlockSpec(memory_space=pl.ANY)
    args = (x, xs, sp_t, norm_g, w_in, w_pool, pool_scale, v_norm_g, w_s, b_s, w_out, fin_g)
    in_specs = []
    for a in args:
        if a is x:
            in_specs.append(pl.BlockSpec((1, tm, d), lambda b, t: (b, t, 0)))
        elif a is w_in or a is w_out or a is sp_t or a is xs:
            in_specs.append(hbm)
        else:
            in_specs.append(_const_spec(a.shape))
    return pl.pallas_call(
        functools.partial(_trunk_kernel, single_step=(batch * (seq // tm) == 1)),
        grid=(batch, seq // tm),
        in_specs=in_specs,
        out_specs=[
            pl.BlockSpec((1, tm, d), lambda b, t: (b, t, 0)),
            pl.BlockSpec((DEPTH, 1, POOL_BUF, W_A), lambda b, t: (0, b, 0, 0)),
            pl.BlockSpec((DEPTH, 1, CHUNK, W_B), lambda b, t: (0, b, 0, 0)),
            hbm, hbm, hbm,
        ],
        out_shape=[
            jax.ShapeDtypeStruct((batch, seq, d), F32),
            jax.ShapeDtypeStruct((DEPTH, batch, POOL_BUF, W_A), F32),
            jax.ShapeDtypeStruct((DEPTH, batch, CHUNK, W_B), F32),
            jax.ShapeDtypeStruct((n_s, 1, d), F32),
            jax.ShapeDtypeStruct((DEPTH, POOL_BUF, n_s, W_A), F32),
            jax.ShapeDtypeStruct((DEPTH, n_s, 1, W_B), F32),
        ],
        scratch_shapes=[
            pltpu.VMEM((HIST + tm, W_A), F32),
            pltpu.VMEM((HIST + tm, W_A - POOL_GROUP_DIM), F32),
            pltpu.VMEM((HIST + tm, W_A - 2 * POOL_GROUP_DIM), F32),
            pltpu.VMEM((HIST + tm, W_A - 3 * POOL_GROUP_DIM), F32),
            pltpu.VMEM((DEPTH, HIST, W_A), F32),
            pltpu.VMEM((DEPTH, HIST, W_A - POOL_GROUP_DIM), F32),
            pltpu.VMEM((DEPTH, HIST, W_A - 2 * POOL_GROUP_DIM), F32),
            pltpu.VMEM((DEPTH, HIST, W_A - 3 * POOL_GROUP_DIM), F32),
            pltpu.VMEM((tm, d), BF16),
            pltpu.VMEM((tm, d), F32),
            pltpu.VMEM((DEPTH, d, D_IN), BF16),
            pltpu.VMEM((DEPTH, W_A + W_B, d), BF16),
            pltpu.VMEM((DEPTH, CHUNK, W_B), F32),
            pltpu.VMEM((DEPTH, POOL_BUF, n_s, W_A), F32),
            pltpu.VMEM((n_s, d), F32),
            pltpu.VMEM((n_s, d), F32),
            pltpu.VMEM((DEPTH, n_s, W_A), F32),
            pltpu.VMEM((DEPTH, n_s, W_B), F32),
            pltpu.VMEM((WEIGHT_SLOTS, d, WEIGHT_CHUNK_COLS), F32),
            pltpu.SemaphoreType.DMA((WEIGHT_SLOTS,)),
            pltpu.SemaphoreType.DMA((4,)),
            pltpu.SemaphoreType.DMA((DEPTH,)),
            pltpu.SemaphoreType.DMA((DEPTH,)),
        ],
        compiler_params=pltpu.CompilerParams(
            dimension_semantics=("arbitrary", "arbitrary"),
            vmem_limit_bytes=VMEM_LIMIT_BYTES,
        ),
        name="trunk",
    )(*args)


def kernel(x_prompt, x_sample, state_pool, norm_g, w_in, w_pool, pool_scale, v_norm_g, w_s, b_s, w_out, final_norm_g):
    assert x_sample.shape[1] == 1 and state_pool.shape[2] == POOL_BUF
    sp_t = jnp.transpose(state_pool, (0, 2, 1, 3))
    y_prompt, pool_prompt, chunk_v_prompt, y_sample, pool_sample_t, chunk_v_sample = _trunk_call(
        x_prompt, x_sample, sp_t, norm_g, w_in, w_pool, pool_scale, v_norm_g, w_s, b_s, w_out, final_norm_g)
    pool_sample = jnp.transpose(pool_sample_t, (0, 2, 1, 3))
    return (y_prompt, y_sample, pool_prompt, pool_sample, chunk_v_prompt, chunk_v_sample)
```

```python
import functools

import jax
import jax.numpy as jnp
from jax import lax
from jax.experimental import pallas as pl
from jax.experimental.pallas import tpu as pltpu

D_MODEL = 1024
DEPTH = 2
W_A = 512
W_B = 512
D_IN = 2 * W_A + 3 * W_B
POOL_WINDOWS = (2, 4, 8, 16)
N_POOL_GROUPS = len(POOL_WINDOWS)
POOL_GROUP_DIM = W_A // N_POOL_GROUPS
POOL_BUF = max(POOL_WINDOWS) - 1
CHUNK = 128
N_HEADS_B = 4
HEAD_DIM_B = W_B // N_HEADS_B
EPS = 1e-6
SAMPLE_PAST_LEN = 16384

COL_A_IN = 0
COL_A_GATE = W_A
COL_U = 2 * W_A
COL_V = 2 * W_A + W_B
COL_B_GATE = 2 * W_A + 2 * W_B

HIST = 16
ROW_TILE = 512
ROW_BLOCKS = 2
WEIGHT_CHUNK_COLS = 512
WEIGHT_SLOTS = 5
VMEM_LIMIT_BYTES = 56 * 1024 * 1024

BF16 = jnp.bfloat16
F32 = jnp.float32


def _dot(a, b):
    return jnp.dot(a.astype(BF16), b.astype(BF16), preferred_element_type=F32)


def _rmsnorm(x, g):
    ms = jnp.mean(x * x, axis=-1, keepdims=True)
    return (x * lax.rsqrt(ms + EPS)) * g


def _silu(x):
    hx = 0.5 * x
    return hx * jnp.tanh(hx) + hx


def _head_rmsnorm(v, g):
    parts = []
    for h in range(N_HEADS_B):
        sl = slice(h * HEAD_DIM_B, (h + 1) * HEAD_DIM_B)
        parts.append(_rmsnorm(v[:, sl], g[:, sl]))
    return jnp.concatenate(parts, axis=-1)


def _masked_ws(ws_ref, l):
    row = lax.broadcasted_iota(jnp.int32, (CHUNK, CHUNK), 0)
    col = lax.broadcasted_iota(jnp.int32, (CHUNK, CHUNK), 1)
    keep = col <= row
    return [jnp.where(keep, ws_ref[l, h], 0.0).astype(BF16) for h in range(N_HEADS_B)]


class _Refs:
    def __init__(self, **kw):
        self.__dict__.update(kw)


def _row(ref, l):
    return ref[l:l + 1, :]


class _WeightStream:
    def __init__(self, w_in_hbm, w_out_hbm, w_in_bf, w_out_bf, stage, sems, after_last_start):
        self.chunks = []
        for l in range(DEPTH):
            for col in (COL_V, COL_A_IN, COL_A_GATE, COL_U, COL_B_GATE):
                self.chunks.append((w_in_hbm, w_in_bf, l, col))
            for col in range(0, w_out_hbm.shape[2], WEIGHT_CHUNK_COLS):
                self.chunks.append((w_out_hbm, w_out_bf, l, col))
        self.index = {(id(dst), l, col): i for i, (_, dst, l, col) in enumerate(self.chunks)}
        self.stage, self.sems = stage, sems
        self.after_last_start = after_last_start
        self.done = 0

    def _copy(self, i):
        src, _, l, col = self.chunks[i]
        slot = i % WEIGHT_SLOTS
        return pltpu.make_async_copy(
            src.at[l, :, pl.ds(col, WEIGHT_CHUNK_COLS)], self.stage.at[slot], self.sems.at[slot])

    def _start(self, i):
        self._copy(i).start()
        if i == len(self.chunks) - 1:
            self.after_last_start()

    def start(self):
        for i in range(WEIGHT_SLOTS - 1):
            self._start(i)

    def service(self):
        i = self.done
        if i == len(self.chunks):
            return
        if i + WEIGHT_SLOTS - 1 < len(self.chunks):
            self._start(i + WEIGHT_SLOTS - 1)
        self._copy(i).wait()
        _, dst, l, col = self.chunks[i]
        dst[l, :, col:col + WEIGHT_CHUNK_COLS] = self.stage[i % WEIGHT_SLOTS].astype(BF16)
        self.done += 1

    def need(self, dst, l, col):
        while self.done <= self.index[(id(dst), l, col)]:
            self.service()


class _Chain:
    def __init__(self, l, r0, rows, t, tm, R, need=None):
        self.l, self.r0, self.rows, self.t, self.tm, self.R = l, r0, rows, t, tm, R
        self.need = need if need is not None else (lambda dst, l, col: None)

    def _in(self, col, width):
        R = self.R
        self.need(R.w_in, self.l, col)
        return jnp.dot(R.h_buf[self.r0:self.r0 + self.rows, :], R.w_in[self.l, :, col:col + width],
                       preferred_element_type=F32)

    def p0(self):
        self.v = self._in(COL_V, W_B)
        self.a_in = self._in(COL_A_IN, W_A)

    def p1(self):
        self.a_gate = self._in(COL_A_GATE, W_A)
        self.u = self._in(COL_U, W_B)

    def p2(self):
        l, r0, rows, R = self.l, self.r0, self.rows, self.R
        n_chunks = rows // CHUNK
        a_in = self.a_in
        base = HIST + r0

        vn = _head_rmsnorm(self.v, _row(R.v_norm_g, l))
        self.cv_rows = vn[rows - CHUNK:, :]
        vn_b = vn.astype(BF16)

        a_hist, s2_hist, s4_hist, s8_hist = R.hists
        a_hist[base:base + rows, :] = a_in
        s2 = a_in + a_hist[base - 1:base - 1 + rows, :]
        s2_hist[base:base + rows, :] = s2[:, POOL_GROUP_DIM:]
        s4 = s2[:, POOL_GROUP_DIM:] + s2_hist[base - 2:base - 2 + rows, :]
        s4_hist[base:base + rows, :] = s4[:, POOL_GROUP_DIM:]
        s8 = s4[:, POOL_GROUP_DIM:] + s4_hist[base - 4:base - 4 + rows, :]
        s8_hist[base:base + rows, :] = s8[:, POOL_GROUP_DIM:]
        s16 = s8[:, POOL_GROUP_DIM:] + s8_hist[base - 8:base - 8 + rows, :]
        win_sums = (s2[:, :POOL_GROUP_DIM], s4[:, :POOL_GROUP_DIM], s8[:, :POOL_GROUP_DIM], s16)
        self.pool_rows = a_in[rows - POOL_BUF:, :]

        d_parts = []
        for g, w in enumerate(POOL_WINDOWS):
            sl = slice(g * POOL_GROUP_DIM, (g + 1) * POOL_GROUP_DIM)
            ssum = win_sums[g]
            if r0 == 0:
                pos1 = lax.broadcasted_iota(jnp.int32, (HIST, POOL_GROUP_DIM), 0) + self.t * self.tm + 1
                inv_head = 1.0 / jnp.minimum(pos1, w).astype(F32)
                mean = jnp.concatenate([ssum[:HIST] * inv_head, ssum[HIST:] * (1.0 / w)], axis=0)
            else:
                mean = ssum * (1.0 / w)
            d_parts.append((mean - a_in[:, sl]).astype(BF16))

        ws_b = _masked_ws(R.ws, l)
        s_heads = []
        for hh in range(N_HEADS_B):
            sl = slice(hh * HEAD_DIM_B, (hh + 1) * HEAD_DIM_B)
            rhs = jnp.concatenate([vn_b[c * CHUNK:(c + 1) * CHUNK, sl] for c in range(n_chunks)], axis=1)
            res = jnp.dot(ws_b[hh], rhs, preferred_element_type=F32)
            s_heads.append(jnp.concatenate(
                [res[:, c * HEAD_DIM_B:(c + 1) * HEAD_DIM_B] for c in range(n_chunks)], axis=0))
        self.s_heads = s_heads

        self.y_parts = [_dot(d_parts[g], R.w_pool[l, g]) for g in range(N_POOL_GROUPS)]
        self.b_gate = self._in(COL_B_GATE, W_B)

    def p3(self):
        l, rows, R = self.l, self.rows, self.R
        n_chunks = rows // CHUNK
        self.need(R.w_out, l, R.w_out.shape[2] - WEIGHT_CHUNK_COLS)
        a_out = ((jnp.concatenate(self.y_parts, axis=-1) * _row(R.pool_scale, l)) * _silu(self.a_gate)).astype(BF16)
        out_a = _dot(a_out, R.w_out[l, 0:W_A, :])
        s = jnp.concatenate(self.s_heads, axis=-1) + jnp.concatenate([R.bias[l]] * n_chunks, axis=0)
        b_out = ((self.u * s) * _silu(self.b_gate)).astype(BF16)
        self.out = out_a + _dot(b_out, R.w_out[l, W_A:, :])


def _per_head_lanes(vals):
    return jnp.concatenate([jnp.broadcast_to(v, (1, HEAD_DIM_B)) for v in vals], axis=-1)


def _sample_group(xs_ref, sp_ref, norm_g_ref, b_s_ref, fin_g, R, ys_ref, a_in_s_ref, vn_s_ref, before_pool):
    x = xs_ref[...]
    for l in range(DEPTH):
        h = _rmsnorm(x, _row(norm_g_ref, l)).astype(BF16)
        a_in = _dot(h, R.w_in[l, :, COL_A_IN:COL_A_IN + W_A])
        a_in_s_ref[l] = a_in
        a_gate = _dot(h, R.w_in[l, :, COL_A_GATE:COL_A_GATE + W_A])

        v = _dot(h, R.w_in[l, :, COL_V:COL_V + W_B])
        vn = _head_rmsnorm(v, _row(R.v_norm_g, l))
        vn_s_ref[l] = vn
        ws00 = _per_head_lanes([R.ws[l, hh, 0:1, 0:1] for hh in range(N_HEADS_B)])
        b0 = _per_head_lanes([b_s_ref[l, hh:hh + 1, 0:1] for hh in range(N_HEADS_B)])
        s = ws00 * vn + b0
        u = _dot(h, R.w_in[l, :, COL_U:COL_U + W_B])
        b_gate = _dot(h, R.w_in[l, :, COL_B_GATE:COL_B_GATE + W_B])
        b_out = (u * s) * _silu(b_gate)

        before_pool(l)
        a_parts = []
        for g, w in enumerate(POOL_WINDOWS):
            sl = slice(g * POOL_GROUP_DIM, (g + 1) * POOL_GROUP_DIM)
            ssum = a_in[:, sl]
            for k in range(1, w):
                ssum = ssum + sp_ref[l, POOL_BUF - k, :, sl]
            cnt = float(min(SAMPLE_PAST_LEN + 1, w))
            d = ssum / cnt - a_in[:, sl]
            yg = _dot(d, R.w_pool[l, g]) * _row(R.pool_scale, l)[:, sl]
            a_parts.append(yg * _silu(a_gate[:, sl]))
        a_out = jnp.concatenate(a_parts, axis=-1)

        x = x + (_dot(a_out, R.w_out[l, 0:W_A, :]) + _dot(b_out, R.w_out[l, W_A:, :]))
    ys_ref[...] = _rmsnorm(x, fin_g)


def _trunk_kernel(x_ref, xs_hbm, sp_hbm, norm_g_ref, w_in_hbm, w_pool_ref, pool_scale_ref, v_norm_g_ref,
                  ws_ref, b_s_ref, w_out_hbm, fin_g_ref,
                  y_ref, pool_ref, cv_ref, ys_hbm, pool_s_t_hbm, vn_s_hbm,
                  a_hist, s2_hist, s4_hist, s8_hist, a_carry, s2_carry, s4_carry, s8_carry,
                  h_buf, x_buf, w_in_bf, w_out_bf, bias_buf, sp_buf,
                  xs_buf, ys_buf, a_in_s_buf, vn_s_buf, stage, sems, io_sems, sp_sems, keep_sems,
                  *, single_step):
    b = pl.program_id(0)
    t = pl.program_id(1)
    tm = x_ref.shape[1]
    rows = tm // ROW_BLOCKS
    R = _Refs(w_in=w_in_bf, w_pool=w_pool_ref, pool_scale=pool_scale_ref, v_norm_g=v_norm_g_ref, ws=ws_ref,
              bias=bias_buf, w_out=w_out_bf, hists=(a_hist, s2_hist, s4_hist, s8_hist), h_buf=h_buf)
    hists = R.hists
    carries = (a_carry, s2_carry, s4_carry, s8_carry)
    fin_g = fin_g_ref[...].reshape(1, D_MODEL)

    def tile_body(need=None, between_phases=None):
        for l in range(DEPTH):

            def x_rows(rs, l=l):
                return x_ref[0, rs, :] if l == 0 else x_buf[rs, :]

            def phase_done():
                if between_phases is not None:
                    between_phases()

            blocks = [slice(k * rows, (k + 1) * rows) for k in range(ROW_BLOCKS)]
            for rs in blocks:
                h_buf[rs, :] = _rmsnorm(x_rows(rs), _row(norm_g_ref, l)).astype(BF16)
            chains = [_Chain(l, k * rows, rows, t, tm, R, need) for k in range(ROW_BLOCKS)]
            for c in chains:
                c.p0()
            phase_done()
            for c in chains:
                c.p1()
            phase_done()
            for buf, carry in zip(hists, carries):
                buf[0:HIST, :] = carry[l]
            for c in chains:
                c.p2()
                phase_done()
            for buf, carry in zip(hists, carries):
                carry[l] = buf[tm:tm + HIST, :]
            pool_ref[l, 0] = chains[-1].pool_rows
            cv_ref[l, 0] = chains[-1].cv_rows
            for c, rs in zip(chains, blocks):
                c.p3()
                x_new = x_rows(rs) + c.out
                if l + 1 < DEPTH:
                    x_buf[rs, :] = x_new
                else:
                    y_ref[0, rs, :] = _rmsnorm(x_new, fin_g)
                phase_done()

    def zero_hist():
        for carry in carries:
            carry[...] = jnp.zeros(carry.shape, F32)

    def sample_out_copies():
        return {
            "keep": [pltpu.make_async_copy(sp_buf.at[l, pl.ds(1, POOL_BUF - 1)],
                                           pool_s_t_hbm.at[l, pl.ds(0, POOL_BUF - 1)], keep_sems.at[l])
                     for l in range(DEPTH)],
            "new_row": pltpu.make_async_copy(a_in_s_buf, pool_s_t_hbm.at[:, POOL_BUF - 1], io_sems.at[1]),
            "ys": pltpu.make_async_copy(ys_buf, ys_hbm.at[:, 0, :], io_sems.at[2]),
            "vn": pltpu.make_async_copy(vn_s_buf, vn_s_hbm.at[:, :, 0, :], io_sems.at[3]),
        }

    def wait_sample_outs():
        outs = sample_out_copies()
        for cp in outs["keep"] + [outs["new_row"], outs["ys"], outs["vn"]]:
            cp.wait()

    first = jnp.logical_and(b == 0, t == 0)

    @pl.when(first)
    def _():
        xs_copy = pltpu.make_async_copy(xs_hbm.at[:, 0, :], xs_buf, io_sems.at[0])
        sp_copies = [pltpu.make_async_copy(sp_hbm.at[l], sp_buf.at[l], sp_sems.at[l]) for l in range(DEPTH)]
        outs = sample_out_copies()

        def start_in_copies():
            xs_copy.start()
            sp_copies[0].start()

        def before_pool(l):
            sp_copies[l].wait()
            if l + 1 < DEPTH:
                sp_copies[l + 1].start()

        stream = _WeightStream(w_in_hbm, w_out_hbm, w_in_bf, w_out_bf, stage, sems, start_in_copies)
        stream.start()
        for l in range(DEPTH):
            for hh in range(N_HEADS_B):
                bias_buf[l, :, hh * HEAD_DIM_B:(hh + 1) * HEAD_DIM_B] = jnp.broadcast_to(
                    b_s_ref[l, hh:hh + 1, :], (CHUNK, CHUNK)).T
        zero_hist()

        tile_body(stream.need, stream.service)
        assert stream.done == len(stream.chunks)
        xs_copy.wait()
        _sample_group(xs_buf, sp_buf, norm_g_ref, b_s_ref, fin_g, R, ys_buf, a_in_s_buf, vn_s_buf, before_pool)
        for cp in outs["keep"] + [outs["new_row"], outs["ys"], outs["vn"]]:
            cp.start()
        if single_step:
            wait_sample_outs()

    @pl.when(jnp.logical_not(first))
    def _():
        pl.when(t == 0)(zero_hist)
        tile_body()
        pl.when(b * pl.num_programs(1) + t == 1)(wait_sample_outs)


def _const_spec(shape):
    zeros = (0,) * len(shape)
    return pl.BlockSpec(shape, lambda b, t: zeros, pipeline_mode=pl.Buffered(1))


def _trunk_call(x, xs, sp_t, norm_g, w_in, w_pool, pool_scale, v_norm_g, w_s, b_s, w_out, fin_g):
    batch, seq, d = x.shape
    n_s = xs.shape[0]
    tm = ROW_TILE
    assert seq % tm == 0 and tm % (ROW_BLOCKS * CHUNK) == 0
    assert W_A == W_B == WEIGHT_CHUNK_COLS and d % WEIGHT_CHUNK_COLS == 0 and d == W_A + W_B
    hbm = pl.BlockSpec(memory_space=pl.ANY)
    args = (x, xs, sp_t, norm_g, w_in, w_pool, pool_scale, v_norm_g, w_s, b_s, w_out, fin_g)
    in_specs = []
    for a in args:
        if a is x:
            in_specs.append(pl.BlockSpec((1, tm, d), lambda b, t: (b, t, 0)))
        elif a is w_in or a is w_out or a is sp_t or a is xs:
            in_specs.append(hbm)
        else:
            in_specs.append(_const_spec(a.shape))
    return pl.pallas_call(
        functools.partial(_trunk_kernel, single_step=(batch * (seq // tm) == 1)),
        grid=(batch, seq // tm),
        in_specs=in_specs,
        out_specs=[
            pl.BlockSpec((1, tm, d), lambda b, t: (b, t, 0)),
            pl.BlockSpec((DEPTH, 1, POOL_BUF, W_A), lambda b, t: (0, b, 0, 0)),
            pl.BlockSpec((DEPTH, 1, CHUNK, W_B), lambda b, t: (0, b, 0, 0)),
            hbm, hbm, hbm,
        ],
        out_shape=[
            jax.ShapeDtypeStruct((batch, seq, d), F32),
            jax.ShapeDtypeStruct((DEPTH, batch, POOL_BUF, W_A), F32),
            jax.ShapeDtypeStruct((DEPTH, batch, CHUNK, W_B), F32),
            jax.ShapeDtypeStruct((n_s, 1, d), F32),
            jax.ShapeDtypeStruct((DEPTH, POOL_BUF, n_s, W_A), F32),
            jax.ShapeDtypeStruct((DEPTH, n_s, 1, W_B), F32),
        ],
        scratch_shapes=[
            pltpu.VMEM((HIST + tm, W_A), F32),
            pltpu.VMEM((HIST + tm, W_A - POOL_GROUP_DIM), F32),
            pltpu.VMEM((HIST + tm, W_A - 2 * POOL_GROUP_DIM), F32),
            pltpu.VMEM((HIST + tm, W_A - 3 * POOL_GROUP_DIM), F32),
            pltpu.VMEM((DEPTH, HIST, W_A), F32),
            pltpu.VMEM((DEPTH, HIST, W_A - POOL_GROUP_DIM), F32),
            pltpu.VMEM((DEPTH, HIST, W_A - 2 * POOL_GROUP_DIM), F32),
            pltpu.VMEM((DEPTH, HIST, W_A - 3 * POOL_GROUP_DIM), F32),
            pltpu.VMEM((tm, d), BF16),
            pltpu.VMEM((tm, d), F32),
            pltpu.VMEM((DEPTH, d, D_IN), BF16),
            pltpu.VMEM((DEPTH, W_A + W_B, d), BF16),
            pltpu.VMEM((DEPTH, CHUNK, W_B), F32),
            pltpu.VMEM((DEPTH, POOL_BUF, n_s, W_A), F32),
            pltpu.VMEM((n_s, d), F32),
            pltpu.VMEM((n_s, d), F32),
            pltpu.VMEM((DEPTH, n_s, W_A), F32),
            pltpu.VMEM((DEPTH, n_s, W_B), F32),
            pltpu.VMEM((WEIGHT_SLOTS, d, WEIGHT_CHUNK_COLS), F32),
            pltpu.SemaphoreType.DMA((WEIGHT_SLOTS,)),
            pltpu.SemaphoreType.DMA((4,)),
            pltpu.SemaphoreType.DMA((DEPTH,)),
            pltpu.SemaphoreType.DMA((DEPTH,)),
        ],
        compiler_params=pltpu.CompilerParams(
            dimension_semantics=("arbitrary", "arbitrary"),
            vmem_limit_bytes=VMEM_LIMIT_BYTES,
        ),
        name="trunk",
    )(*args)


def kernel(x_prompt, x_sample, state_pool, norm_g, w_in, w_pool, pool_scale, v_norm_g, w_s, b_s, w_out, final_norm_g):
    assert x_sample.shape[1] == 1 and state_pool.shape[2] == POOL_BUF
    sp_t = jnp.transpose(state_pool, (0, 2, 1, 3))
    y_prompt, pool_prompt, chunk_v_prompt, y_sample, pool_sample_t, chunk_v_sample = _trunk_call(
        x_prompt, x_sample, sp_t, norm_g, w_in, w_pool, pool_scale, v_norm_g, w_s, b_s, w_out, final_norm_g)
    pool_sample = jnp.transpose(pool_sample_t, (0, 2, 1, 3))
    return (y_prompt, y_sample, pool_prompt, pool_sample, chunk_v_prompt, chunk_v_sample)
```

```python
import functools

import jax
import jax.numpy as jnp
from jax import lax
from jax.experimental import pallas as pl
from jax.experimental.pallas import tpu as pltpu

D_MODEL = 1024
DEPTH = 2
W_A = 512
W_B = 512
D_IN = 2 * W_A + 3 * W_B
POOL_WINDOWS = (2, 4, 8, 16)
N_POOL_GROUPS = len(POOL_WINDOWS)
POOL_GROUP_DIM = W_A // N_POOL_GROUPS
POOL_BUF = max(POOL_WINDOWS) - 1
CHUNK = 128
N_HEADS_B = 4
HEAD_DIM_B = W_B // N_HEADS_B
EPS = 1e-6
SAMPLE_PAST_LEN = 16384

COL_A_IN = 0
COL_A_GATE = W_A
COL_U = 2 * W_A
COL_V = 2 * W_A + W_B
COL_B_GATE = 2 * W_A + 2 * W_B

HIST = 16
ROW_TILE = 512
ROW_BLOCKS = 2
WEIGHT_CHUNK_COLS = 512
WEIGHT_SLOTS = 5
VMEM_LIMIT_BYTES = 56 * 1024 * 1024

BF16 = jnp.bfloat16
F32 = jnp.float32


def _dot(a, b):
    return jnp.dot(a.astype(BF16), b.astype(BF16), preferred_element_type=F32)


def _rmsnorm(x, g):
    ms = jnp.mean(x * x, axis=-1, keepdims=True)
    return (x * lax.rsqrt(ms + EPS)) * g


def _silu(x):
    hx = 0.5 * x
    return hx * jnp.tanh(hx) + hx


def _head_rmsnorm(v, g):
    parts = []
    for h in range(N_HEADS_B):
        sl = slice(h * HEAD_DIM_B, (h + 1) * HEAD_DIM_B)
        parts.append(_rmsnorm(v[:, sl], g[:, sl]))
    return jnp.concatenate(parts, axis=-1)


def _masked_ws(ws_ref, l):
    row = lax.broadcasted_iota(jnp.int32, (CHUNK, CHUNK), 0)
    col = lax.broadcasted_iota(jnp.int32, (CHUNK, CHUNK), 1)
    keep = col <= row
    return [jnp.where(keep, ws_ref[l, h], 0.0).astype(BF16) for h in range(N_HEADS_B)]


class _Refs:
    def __init__(self, **kw):
        self.__dict__.update(kw)


def _row(ref, l):
    return ref[l:l + 1, :]


class _WeightStream:
    def __init__(self, w_in_hbm, w_out_hbm, w_in_bf, w_out_bf, stage, sems, after_last_start):
        self.chunks = []
        for l in range(DEPTH):
            for col in (COL_V, COL_A_IN, COL_A_GATE, COL_U, COL_B_GATE):
                self.chunks.append((w_in_hbm, w_in_bf, l, col))
            for col in range(0, w_out_hbm.shape[2], WEIGHT_CHUNK_COLS):
                self.chunks.append((w_out_hbm, w_out_bf, l, col))
        self.index = {(id(dst), l, col): i for i, (_, dst, l, col) in enumerate(self.chunks)}
        self.stage, self.sems = stage, sems
        self.after_last_start = after_last_start
        self.done = 0

    def _copy(self, i):
        src, _, l, col = self.chunks[i]
        slot = i % WEIGHT_SLOTS
        return pltpu.make_async_copy(
            src.at[l, :, pl.ds(col, WEIGHT_CHUNK_COLS)], self.stage.at[slot], self.sems.at[slot])

    def _start(self, i):
        self._copy(i).start()
        if i == len(self.chunks) - 1:
            self.after_last_start()

    def start(self):
        for i in range(WEIGHT_SLOTS - 1):
            self._start(i)

    def service(self):
        i = self.done
        if i == len(self.chunks):
            return
        if i + WEIGHT_SLOTS - 1 < len(self.chunks):
            self._start(i + WEIGHT_SLOTS - 1)
        self._copy(i).wait()
        _, dst, l, col = self.chunks[i]
        dst[l, :, col:col + WEIGHT_CHUNK_COLS] = self.stage[i % WEIGHT_SLOTS].astype(BF16)
        self.done += 1

    def need(self, dst, l, col):
        while self.done <= self.index[(id(dst), l, col)]:
            self.service()


class _Chain:
    def __init__(self, l, r0, rows, t, tm, R, need=None):
        self.l, self.r0, self.rows, self.t, self.tm, self.R = l, r0, rows, t, tm, R
        self.need = need if need is not None else (lambda dst, l, col: None)

    def _in(self, col, width):
        R = self.R
        self.need(R.w_in, self.l, col)
        return jnp.dot(R.h_buf[self.r0:self.r0 + self.rows, :], R.w_in[self.l, :, col:col + width],
                       preferred_element_type=F32)

    def p0(self):
        self.v = self._in(COL_V, W_B)
        self.a_in = self._in(COL_A_IN, W_A)

    def p1(self):
        self.a_gate = self._in(COL_A_GATE, W_A)
        self.u = self._in(COL_U, W_B)

    def p2(self):
        l, r0, rows, R = self.l, self.r0, self.rows, self.R
        n_chunks = rows // CHUNK
        a_in = self.a_in
        base = HIST + r0

        vn = _head_rmsnorm(self.v, _row(R.v_norm_g, l))
        self.cv_rows = vn[rows - CHUNK:, :]
        vn_b = vn.astype(BF16)

        a_hist, s2_hist, s4_hist, s8_hist = R.hists
        a_hist[base:base + rows, :] = a_in
        s2 = a_in + a_hist[base - 1:base - 1 + rows, :]
        s2_hist[base:base + rows, :] = s2[:, POOL_GROUP_DIM:]
        s4 = s2[:, POOL_GROUP_DIM:] + s2_hist[base - 2:base - 2 + rows, :]
        s4_hist[base:base + rows, :] = s4[:, POOL_GROUP_DIM:]
        s8 = s4[:, POOL_GROUP_DIM:] + s4_hist[base - 4:base - 4 + rows, :]
        s8_hist[base:base + rows, :] = s8[:, POOL_GROUP_DIM:]
        s16 = s8[:, POOL_GROUP_DIM:] + s8_hist[base - 8:base - 8 + rows, :]
        win_sums = (s2[:, :POOL_GROUP_DIM], s4[:, :POOL_GROUP_DIM], s8[:, :POOL_GROUP_DIM], s16)
        self.pool_rows = a_in[rows - POOL_BUF:, :]

        d_parts = []
        for g, w in enumerate(POOL_WINDOWS):
            sl = slice(g * POOL_GROUP_DIM, (g + 1) * POOL_GROUP_DIM)
            ssum = win_sums[g]
            if r0 == 0:
                pos1 = lax.broadcasted_iota(jnp.int32, (HIST, POOL_GROUP_DIM), 0) + self.t * self.tm + 1
                inv_head = 1.0 / jnp.minimum(pos1, w).astype(F32)
                mean = jnp.concatenate([ssum[:HIST] * inv_head, ssum[HIST:] * (1.0 / w)], axis=0)
            else:
                mean = ssum * (1.0 / w)
            d_parts.append((mean - a_in[:, sl]).astype(BF16))

        ws_b = _masked_ws(R.ws, l)
        s_heads = []
        for hh in range(N_HEADS_B):
            sl = slice(hh * HEAD_DIM_B, (hh + 1) * HEAD_DIM_B)
            rhs = jnp.concatenate([vn_b[c * CHUNK:(c + 1) * CHUNK, sl] for c in range(n_chunks)], axis=1)
            res = jnp.dot(ws_b[hh], rhs, preferred_element_type=F32)
            s_heads.append(jnp.concatenate(
                [res[:, c * HEAD_DIM_B:(c + 1) * HEAD_DIM_B] for c in range(n_chunks)], axis=0))
        self.s_heads = s_heads

        self.y_parts = [_dot(d_parts[g], R.w_pool[l, g]) for g in range(N_POOL_GROUPS)]
        self.b_gate = self._in(COL_B_GATE, W_B)

    def p3(self):
        l, rows, R = self.l, self.rows, self.R
        n_chunks = rows // CHUNK
        self.need(R.w_out, l, R.w_out.shape[2] - WEIGHT_CHUNK_COLS)
        a_out = ((jnp.concatenate(self.y_parts, axis=-1) * _row(R.pool_scale, l)) * _silu(self.a_gate)).astype(BF16)
        out_a = _dot(a_out, R.w_out[l, 0:W_A, :])
        s = jnp.concatenate(self.s_heads, axis=-1) + jnp.concatenate([R.bias[l]] * n_chunks, axis=0)
        b_out = ((self.u * s) * _silu(self.b_gate)).astype(BF16)
        self.out = out_a + _dot(b_out, R.w_out[l, W_A:, :])


def _per_head_lanes(vals):
    return jnp.concatenate([jnp.broadcast_to(v, (1, HEAD_DIM_B)) for v in vals], axis=-1)


def _sample_group(xs_ref, sp_ref, norm_g_ref, b_s_ref, fin_g, R, ys_ref, a_in_s_ref, vn_s_ref, before_pool):
    x = xs_ref[...]
    for l in range(DEPTH):
        h = _rmsnorm(x, _row(norm_g_ref, l)).astype(BF16)
        a_in = _dot(h, R.w_in[l, :, COL_A_IN:COL_A_IN + W_A])
        a_in_s_ref[l] = a_in
        a_gate = _dot(h, R.w_in[l, :, COL_A_GATE:COL_A_GATE + W_A])

        v = _dot(h, R.w_in[l, :, COL_V:COL_V + W_B])
        vn = _head_rmsnorm(v, _row(R.v_norm_g, l))
        vn_s_ref[l] = vn
        ws00 = _per_head_lanes([R.ws[l, hh, 0:1, 0:1] for hh in range(N_HEADS_B)])
        b0 = _per_head_lanes([b_s_ref[l, hh:hh + 1, 0:1] for hh in range(N_HEADS_B)])
        s = ws00 * vn + b0
        u = _dot(h, R.w_in[l, :, COL_U:COL_U + W_B])
        b_gate = _dot(h, R.w_in[l, :, COL_B_GATE:COL_B_GATE + W_B])
        b_out = (u * s) * _silu(b_gate)

        before_pool(l)
        a_parts = []
        for g, w in enumerate(POOL_WINDOWS):
            sl = slice(g * POOL_GROUP_DIM, (g + 1) * POOL_GROUP_DIM)
            ssum = a_in[:, sl]
            for k in range(1, w):
                ssum = ssum + sp_ref[l, POOL_BUF - k, :, sl]
            cnt = float(min(SAMPLE_PAST_LEN + 1, w))
            d = ssum / cnt - a_in[:, sl]
            yg = _dot(d, R.w_pool[l, g]) * _row(R.pool_scale, l)[:, sl]
            a_parts.append(yg * _silu(a_gate[:, sl]))
        a_out = jnp.concatenate(a_parts, axis=-1)

        x = x + (_dot(a_out, R.w_out[l, 0:W_A, :]) + _dot(b_out, R.w_out[l, W_A:, :]))
    ys_ref[...] = _rmsnorm(x, fin_g)


def _trunk_kernel(x_ref, xs_hbm, sp_hbm, norm_g_ref, w_in_hbm, w_pool_ref, pool_scale_ref, v_norm_g_ref,
                  ws_ref, b_s_ref, w_out_hbm, fin_g_ref,
                  y_ref, pool_ref, cv_ref, ys_hbm, pool_s_t_hbm, vn_s_hbm,
                  a_hist, s2_hist, s4_hist, s8_hist, a_carry, s2_carry, s4_carry, s8_carry,
                  h_buf, x_buf, w_in_bf, w_out_bf, bias_buf, sp_buf,
                  xs_buf, ys_buf, a_in_s_buf, vn_s_buf, stage, sems, io_sems, *, single_step):
    b = pl.program_id(0)
    t = pl.program_id(1)
    tm = x_ref.shape[1]
    rows = tm // ROW_BLOCKS
    R = _Refs(w_in=w_in_bf, w_pool=w_pool_ref, pool_scale=pool_scale_ref, v_norm_g=v_norm_g_ref, ws=ws_ref,
              bias=bias_buf, w_out=w_out_bf, hists=(a_hist, s2_hist, s4_hist, s8_hist), h_buf=h_buf)
    hists = R.hists
    carries = (a_carry, s2_carry, s4_carry, s8_carry)
    fin_g = fin_g_ref[...].reshape(1, D_MODEL)

    def tile_body(need=None, between_phases=None):
        for l in range(DEPTH):

            def x_rows(rs, l=l):
                return x_ref[0, rs, :] if l == 0 else x_buf[rs, :]

            def phase_done():
                if between_phases is not None:
                    between_phases()

            blocks = [slice(k * rows, (k + 1) * rows) for k in range(ROW_BLOCKS)]
            for rs in blocks:
                h_buf[rs, :] = _rmsnorm(x_rows(rs), _row(norm_g_ref, l)).astype(BF16)
            chains = [_Chain(l, k * rows, rows, t, tm, R, need) for k in range(ROW_BLOCKS)]
            for c in chains:
                c.p0()
            phase_done()
            for c in chains:
                c.p1()
            phase_done()
            for buf, carry in zip(hists, carries):
                buf[0:HIST, :] = carry[l]
            for c in chains:
                c.p2()
                phase_done()
            for buf, carry in zip(hists, carries):
                carry[l] = buf[tm:tm + HIST, :]
            pool_ref[l, 0] = chains[-1].pool_rows
            cv_ref[l, 0] = chains[-1].cv_rows
            for c, rs in zip(chains, blocks):
                c.p3()
                x_new = x_rows(rs) + c.out
                if l + 1 < DEPTH:
                    x_buf[rs, :] = x_new
                else:
                    y_ref[0, rs, :] = _rmsnorm(x_new, fin_g)
                phase_done()

    def zero_hist():
        for carry in carries:
            carry[...] = jnp.zeros(carry.shape, F32)

    def sample_out_copies():
        return {
            "keep": pltpu.make_async_copy(sp_buf.at[:, pl.ds(1, POOL_BUF - 1)],
                                          pool_s_t_hbm.at[:, pl.ds(0, POOL_BUF - 1)], io_sems.at[2]),
            "new_row": pltpu.make_async_copy(a_in_s_buf, pool_s_t_hbm.at[:, POOL_BUF - 1], io_sems.at[3]),
            "ys": pltpu.make_async_copy(ys_buf, ys_hbm.at[:, 0, :], io_sems.at[4]),
            "vn": pltpu.make_async_copy(vn_s_buf, vn_s_hbm.at[:, :, 0, :], io_sems.at[5]),
        }

    def wait_sample_outs():
        for cp in sample_out_copies().values():
            cp.wait()

    first = jnp.logical_and(b == 0, t == 0)

    @pl.when(first)
    def _():
        xs_copy = pltpu.make_async_copy(xs_hbm.at[:, 0, :], xs_buf, io_sems.at[0])
        sp_copy = pltpu.make_async_copy(sp_hbm, sp_buf, io_sems.at[1])
        outs = sample_out_copies()

        def start_in_copies():
            xs_copy.start()
            sp_copy.start()

        def before_pool(l):
            if l == 0:
                sp_copy.wait()
                outs["keep"].start()

        stream = _WeightStream(w_in_hbm, w_out_hbm, w_in_bf, w_out_bf, stage, sems, start_in_copies)
        stream.start()
        for l in range(DEPTH):
            for hh in range(N_HEADS_B):
                bias_buf[l, :, hh * HEAD_DIM_B:(hh + 1) * HEAD_DIM_B] = jnp.broadcast_to(
                    b_s_ref[l, hh:hh + 1, :], (CHUNK, CHUNK)).T
        zero_hist()

        tile_body(stream.need, stream.service)
        assert stream.done == len(stream.chunks)
        xs_copy.wait()
        _sample_group(xs_buf, sp_buf, norm_g_ref, b_s_ref, fin_g, R, ys_buf, a_in_s_buf, vn_s_buf, before_pool)
        for name in ("ys", "vn", "new_row"):
            outs[name].start()
        if single_step:
            wait_sample_outs()

    @pl.when(jnp.logical_not(first))
    def _():
        pl.when(t == 0)(zero_hist)
        tile_body()
        pl.when(b * pl.num_programs(1) + t == 1)(wait_sample_outs)


def _const_spec(shape):
    zeros = (0,) * len(shape)
    return pl.BlockSpec(shape, lambda b, t: zeros, pipeline_mode=pl.Buffered(1))


def _trunk_call(x, xs, sp_t, norm_g, w_in, w_pool, pool_scale, v_norm_g, w_s, b_s, w_out, fin_g):
    batch, seq, d = x.shape
    n_s = xs.shape[0]
    tm = ROW_TILE
    assert seq % tm == 0 and tm % (ROW_BLOCKS * CHUNK) == 0
    assert W_A == W_B == WEIGHT_CHUNK_COLS and d % WEIGHT_CHUNK_COLS == 0 and d == W_A + W_B
    hbm = pl.BlockSpec(memory_space=pl.ANY)
    args = (x, xs, sp_t, norm_g, w_in, w_pool, pool_scale, v_norm_g, w_s, b_s, w_out, fin_g)
    in_specs = []
    for a in args:
        if a is x:
            in_specs.append(pl.BlockSpec((1, tm, d), lambda b, t: (b, t, 0)))
        elif a is w_in or a is w_out or a is sp_t or a is xs:
            in_specs.append(hbm)
        else:
            in_specs.append(_const_spec(a.shape))
    return pl.pallas_call(
        functools.partial(_trunk_kernel, single_step=(batch * (seq // tm) == 1)),
        grid=(batch, seq // tm),
        in_specs=in_specs,
        out_specs=[
            pl.BlockSpec((1, tm, d), lambda b, t: (b, t, 0)),
            pl.BlockSpec((DEPTH, 1, POOL_BUF, W_A), lambda b, t: (0, b, 0, 0)),
            pl.BlockSpec((DEPTH, 1, CHUNK, W_B), lambda b, t: (0, b, 0, 0)),
            hbm, hbm, hbm,
        ],
        out_shape=[
            jax.ShapeDtypeStruct((batch, seq, d), F32),
            jax.ShapeDtypeStruct((DEPTH, batch, POOL_BUF, W_A), F32),
            jax.ShapeDtypeStruct((DEPTH, batch, CHUNK, W_B), F32),
            jax.ShapeDtypeStruct((n_s, 1, d), F32),
            jax.ShapeDtypeStruct((DEPTH, POOL_BUF, n_s, W_A), F32),
            jax.ShapeDtypeStruct((DEPTH, n_s, 1, W_B), F32),
        ],
        scratch_shapes=[
            pltpu.VMEM((HIST + tm, W_A), F32),
            pltpu.VMEM((HIST + tm, W_A - POOL_GROUP_DIM), F32),
            pltpu.VMEM((HIST + tm, W_A - 2 * POOL_GROUP_DIM), F32),
            pltpu.VMEM((HIST + tm, W_A - 3 * POOL_GROUP_DIM), F32),
            pltpu.VMEM((DEPTH, HIST, W_A), F32),
            pltpu.VMEM((DEPTH, HIST, W_A - POOL_GROUP_DIM), F32),
            pltpu.VMEM((DEPTH, HIST, W_A - 2 * POOL_GROUP_DIM), F32),
            pltpu.VMEM((DEPTH, HIST, W_A - 3 * POOL_GROUP_DIM), F32),
            pltpu.VMEM((tm, d), BF16),
            pltpu.VMEM((tm, d), F32),
            pltpu.VMEM((DEPTH, d, D_IN), BF16),
            pltpu.VMEM((DEPTH, W_A + W_B, d), BF16),
            pltpu.VMEM((DEPTH, CHUNK, W_B), F32),
            pltpu.VMEM((DEPTH, POOL_BUF, n_s, W_A), F32),
            pltpu.VMEM((n_s, d), F32),
            pltpu.VMEM((n_s, d), F32),
            pltpu.VMEM((DEPTH, n_s, W_A), F32),
            pltpu.VMEM((DEPTH, n_s, W_B), F32),
            pltpu.VMEM((WEIGHT_SLOTS, d, WEIGHT_CHUNK_COLS), F32),
            pltpu.SemaphoreType.DMA((WEIGHT_SLOTS,)),
            pltpu.SemaphoreType.DMA((6,)),
        ],
        compiler_params=pltpu.CompilerParams(
            dimension_semantics=("arbitrary", "arbitrary"),
            vmem_limit_bytes=VMEM_LIMIT_BYTES,
        ),
        name="trunk",
    )(*args)


def kernel(x_prompt, x_sample, state_pool, norm_g, w_in, w_pool, pool_scale, v_norm_g, w_s, b_s, w_out, final_norm_g):
    assert x_sample.shape[1] == 1 and state_pool.shape[2] == POOL_BUF
    sp_t = jnp.transpose(state_pool, (0, 2, 1, 3))
    y_prompt, pool_prompt, chunk_v_prompt, y_sample, pool_sample_t, chunk_v_sample = _trunk_call(
        x_prompt, x_sample, sp_t, norm_g, w_in, w_pool, pool_scale, v_norm_g, w_s, b_s, w_out, final_norm_g)
    pool_sample = jnp.transpose(pool_sample_t, (0, 2, 1, 3))
    return (y_prompt, y_sample, pool_prompt, pool_sample, chunk_v_prompt, chunk_v_sample)
```

```python
import functools

import jax
import jax.numpy as jnp
from jax import lax
from jax.experimental import pallas as pl
from jax.experimental.pallas import tpu as pltpu

D_MODEL = 1024
DEPTH = 2
W_A = 512
W_B = 512
D_IN = 2 * W_A + 3 * W_B
POOL_WINDOWS = (2, 4, 8, 16)
N_POOL_GROUPS = len(POOL_WINDOWS)
POOL_GROUP_DIM = W_A // N_POOL_GROUPS
POOL_BUF = max(POOL_WINDOWS) - 1
CHUNK = 128
N_HEADS_B = 4
HEAD_DIM_B = W_B // N_HEADS_B
EPS = 1e-6
SAMPLE_PAST_LEN = 16384

COL_A_IN = 0
COL_A_GATE = W_A
COL_U = 2 * W_A
COL_V = 2 * W_A + W_B
COL_B_GATE = 2 * W_A + 2 * W_B

HIST = 16
ROW_TILE = 512
ROW_BLOCKS = 2
WEIGHT_CHUNK_COLS = 512
WEIGHT_SLOTS = 5
VMEM_LIMIT_BYTES = 56 * 1024 * 1024

BF16 = jnp.bfloat16
F32 = jnp.float32


def _dot(a, b):
    return jnp.dot(a.astype(BF16), b.astype(BF16), preferred_element_type=F32)


def _rmsnorm(x, g):
    ms = jnp.mean(x * x, axis=-1, keepdims=True)
    return (x * lax.rsqrt(ms + EPS)) * g


def _silu(x):
    hx = 0.5 * x
    return hx * jnp.tanh(hx) + hx


def _head_rmsnorm(v, g):
    parts = []
    for h in range(N_HEADS_B):
        sl = slice(h * HEAD_DIM_B, (h + 1) * HEAD_DIM_B)
        parts.append(_rmsnorm(v[:, sl], g[:, sl]))
    return jnp.concatenate(parts, axis=-1)


def _masked_ws(ws_ref, l):
    row = lax.broadcasted_iota(jnp.int32, (CHUNK, CHUNK), 0)
    col = lax.broadcasted_iota(jnp.int32, (CHUNK, CHUNK), 1)
    keep = col <= row
    return [jnp.where(keep, ws_ref[l, h], 0.0).astype(BF16) for h in range(N_HEADS_B)]


class _Refs:
    def __init__(self, **kw):
        self.__dict__.update(kw)


def _row(ref, l):
    return ref[l:l + 1, :]


class _WeightStream:
    def __init__(self, w_in_hbm, w_out_hbm, w_in_bf, w_out_bf, stage, sems, after_last_start):
        self.chunks = []
        for l in range(DEPTH):
            for col in (COL_V, COL_A_IN, COL_A_GATE, COL_U, COL_B_GATE):
                self.chunks.append((w_in_hbm, w_in_bf, l, col))
            for col in range(0, w_out_hbm.shape[2], WEIGHT_CHUNK_COLS):
                self.chunks.append((w_out_hbm, w_out_bf, l, col))
        self.index = {(id(dst), l, col): i for i, (_, dst, l, col) in enumerate(self.chunks)}
        self.stage, self.sems = stage, sems
        self.after_last_start = after_last_start
        self.done = 0

    def _copy(self, i):
        src, _, l, col = self.chunks[i]
        slot = i % WEIGHT_SLOTS
        return pltpu.make_async_copy(
            src.at[l, :, pl.ds(col, WEIGHT_CHUNK_COLS)], self.stage.at[slot], self.sems.at[slot])

    def _start(self, i):
        self._copy(i).start()
        if i == len(self.chunks) - 1:
            self.after_last_start()

    def start(self):
        for i in range(WEIGHT_SLOTS - 1):
            self._start(i)

    def service(self):
        i = self.done
        if i == len(self.chunks):
            return
        if i + WEIGHT_SLOTS - 1 < len(self.chunks):
            self._start(i + WEIGHT_SLOTS - 1)
        self._copy(i).wait()
        _, dst, l, col = self.chunks[i]
        dst[l, :, col:col + WEIGHT_CHUNK_COLS] = self.stage[i % WEIGHT_SLOTS].astype(BF16)
        self.done += 1

    def need(self, dst, l, col):
        while self.done <= self.index[(id(dst), l, col)]:
            self.service()


class _Chain:
    def __init__(self, l, r0, rows, t, tm, R, need=None):
        self.l, self.r0, self.rows, self.t, self.tm, self.R = l, r0, rows, t, tm, R
        self.need = need if need is not None else (lambda dst, l, col: None)

    def _in(self, col, width):
        R = self.R
        self.need(R.w_in, self.l, col)
        return jnp.dot(R.h_buf[self.r0:self.r0 + self.rows, :], R.w_in[self.l, :, col:col + width],
                       preferred_element_type=F32)

    def p0(self):
        self.v = self._in(COL_V, W_B)
        self.a_in = self._in(COL_A_IN, W_A)

    def p1(self):
        self.a_gate = self._in(COL_A_GATE, W_A)
        self.u = self._in(COL_U, W_B)

    def p2(self):
        l, r0, rows, R = self.l, self.r0, self.rows, self.R
        n_chunks = rows // CHUNK
        a_in = self.a_in
        base = HIST + r0

        vn = _head_rmsnorm(self.v, _row(R.v_norm_g, l))
        self.cv_rows = vn[rows - CHUNK:, :]
        vn_b = vn.astype(BF16)

        a_hist, s2_hist, s4_hist, s8_hist = R.hists
        a_hist[base:base + rows, :] = a_in
        s2 = a_in + a_hist[base - 1:base - 1 + rows, :]
        s2_hist[base:base + rows, :] = s2[:, POOL_GROUP_DIM:]
        s4 = s2[:, POOL_GROUP_DIM:] + s2_hist[base - 2:base - 2 + rows, :]
        s4_hist[base:base + rows, :] = s4[:, POOL_GROUP_DIM:]
        s8 = s4[:, POOL_GROUP_DIM:] + s4_hist[base - 4:base - 4 + rows, :]
        s8_hist[base:base + rows, :] = s8[:, POOL_GROUP_DIM:]
        s16 = s8[:, POOL_GROUP_DIM:] + s8_hist[base - 8:base - 8 + rows, :]
        win_sums = (s2[:, :POOL_GROUP_DIM], s4[:, :POOL_GROUP_DIM], s8[:, :POOL_GROUP_DIM], s16)
        self.pool_rows = a_in[rows - POOL_BUF:, :]

        d_parts = []
        for g, w in enumerate(POOL_WINDOWS):
            sl = slice(g * POOL_GROUP_DIM, (g + 1) * POOL_GROUP_DIM)
            ssum = win_sums[g]
            if r0 == 0:
                pos1 = lax.broadcasted_iota(jnp.int32, (HIST, POOL_GROUP_DIM), 0) + self.t * self.tm + 1
                inv_head = 1.0 / jnp.minimum(pos1, w).astype(F32)
                mean = jnp.concatenate([ssum[:HIST] * inv_head, ssum[HIST:] * (1.0 / w)], axis=0)
            else:
                mean = ssum * (1.0 / w)
            d_parts.append((mean - a_in[:, sl]).astype(BF16))

        ws_b = _masked_ws(R.ws, l)
        s_heads = []
        for hh in range(N_HEADS_B):
            sl = slice(hh * HEAD_DIM_B, (hh + 1) * HEAD_DIM_B)
            rhs = jnp.concatenate([vn_b[c * CHUNK:(c + 1) * CHUNK, sl] for c in range(n_chunks)], axis=1)
            res = jnp.dot(ws_b[hh], rhs, preferred_element_type=F32)
            s_heads.append(jnp.concatenate(
                [res[:, c * HEAD_DIM_B:(c + 1) * HEAD_DIM_B] for c in range(n_chunks)], axis=0))
        self.s_heads = s_heads

        self.y_parts = [_dot(d_parts[g], R.w_pool[l, g]) for g in range(N_POOL_GROUPS)]
        self.b_gate = self._in(COL_B_GATE, W_B)

    def p3(self):
        l, rows, R = self.l, self.rows, self.R
        n_chunks = rows // CHUNK
        self.need(R.w_out, l, R.w_out.shape[2] - WEIGHT_CHUNK_COLS)
        a_out = ((jnp.concatenate(self.y_parts, axis=-1) * _row(R.pool_scale, l)) * _silu(self.a_gate)).astype(BF16)
        out_a = _dot(a_out, R.w_out[l, 0:W_A, :])
        s = jnp.concatenate(self.s_heads, axis=-1) + jnp.concatenate([R.bias[l]] * n_chunks, axis=0)
        b_out = ((self.u * s) * _silu(self.b_gate)).astype(BF16)
        self.out = out_a + _dot(b_out, R.w_out[l, W_A:, :])


def _per_head_lanes(vals):
    return jnp.concatenate([jnp.broadcast_to(v, (1, HEAD_DIM_B)) for v in vals], axis=-1)


def _sample_group(xs_ref, sp_ref, norm_g_ref, b_s_ref, fin_g, R, ys_ref, a_in_s_ref, vn_s_ref, before_pool):
    x = xs_ref[...]
    for l in range(DEPTH):
        h = _rmsnorm(x, _row(norm_g_ref, l)).astype(BF16)
        a_in = _dot(h, R.w_in[l, :, COL_A_IN:COL_A_IN + W_A])
        a_in_s_ref[l] = a_in
        a_gate = _dot(h, R.w_in[l, :, COL_A_GATE:COL_A_GATE + W_A])

        v = _dot(h, R.w_in[l, :, COL_V:COL_V + W_B])
        vn = _head_rmsnorm(v, _row(R.v_norm_g, l))
        vn_s_ref[l] = vn
        ws00 = _per_head_lanes([R.ws[l, hh, 0:1, 0:1] for hh in range(N_HEADS_B)])
        b0 = _per_head_lanes([b_s_ref[l, hh:hh + 1, 0:1] for hh in range(N_HEADS_B)])
        s = ws00 * vn + b0
        u = _dot(h, R.w_in[l, :, COL_U:COL_U + W_B])
        b_gate = _dot(h, R.w_in[l, :, COL_B_GATE:COL_B_GATE + W_B])
        b_out = (u * s) * _silu(b_gate)

        before_pool(l)
        a_parts = []
        for g, w in enumerate(POOL_WINDOWS):
            sl = slice(g * POOL_GROUP_DIM, (g + 1) * POOL_GROUP_DIM)
            ssum = a_in[:, sl]
            for k in range(1, w):
                ssum = ssum + sp_ref[l, POOL_BUF - k, :, sl]
            cnt = float(min(SAMPLE_PAST_LEN + 1, w))
            d = ssum / cnt - a_in[:, sl]
            yg = _dot(d, R.w_pool[l, g]) * _row(R.pool_scale, l)[:, sl]
            a_parts.append(yg * _silu(a_gate[:, sl]))
        a_out = jnp.concatenate(a_parts, axis=-1)

        x = x + (_dot(a_out, R.w_out[l, 0:W_A, :]) + _dot(b_out, R.w_out[l, W_A:, :]))
    ys_ref[...] = _rmsnorm(x, fin_g)


def _trunk_kernel(x_ref, xs_hbm, sp_hbm, norm_g_ref, w_in_hbm, w_pool_ref, pool_scale_ref, v_norm_g_ref,
                  ws_ref, b_s_ref, w_out_hbm, fin_g_ref,
                  y_ref, pool_ref, cv_ref, ys_hbm, pool_s_t_hbm, vn_s_hbm,
                  a_hist, s2_hist, s4_hist, s8_hist, a_carry, s2_carry, s4_carry, s8_carry,
                  h_buf, x_buf, w_in_bf, w_out_bf, bias_buf, sp_buf,
                  xs_buf, ys_buf, a_in_s_buf, vn_s_buf, stage, sems, io_sems, *, single_step):
    b = pl.program_id(0)
    t = pl.program_id(1)
    tm = x_ref.shape[1]
    rows = tm // ROW_BLOCKS
    R = _Refs(w_in=w_in_bf, w_pool=w_pool_ref, pool_scale=pool_scale_ref, v_norm_g=v_norm_g_ref, ws=ws_ref,
              bias=bias_buf, w_out=w_out_bf, hists=(a_hist, s2_hist, s4_hist, s8_hist), h_buf=h_buf)
    hists = R.hists
    carries = (a_carry, s2_carry, s4_carry, s8_carry)
    fin_g = fin_g_ref[...].reshape(1, D_MODEL)

    def tile_body(need=None, between_phases=None):
        for l in range(DEPTH):

            def x_rows(rs, l=l):
                return x_ref[0, rs, :] if l == 0 else x_buf[rs, :]

            def phase_done():
                if between_phases is not None:
                    between_phases()

            blocks = [slice(k * rows, (k + 1) * rows) for k in range(ROW_BLOCKS)]
            for rs in blocks:
                h_buf[rs, :] = _rmsnorm(x_rows(rs), _row(norm_g_ref, l)).astype(BF16)
            chains = [_Chain(l, k * rows, rows, t, tm, R, need) for k in range(ROW_BLOCKS)]
            for c in chains:
                c.p0()
            phase_done()
            for c in chains:
                c.p1()
            phase_done()
            for buf, carry in zip(hists, carries):
                buf[0:HIST, :] = carry[l]
            for c in chains:
                c.p2()
                phase_done()
            for buf, carry in zip(hists, carries):
                carry[l] = buf[tm:tm + HIST, :]
            pool_ref[l, 0] = chains[-1].pool_rows
            cv_ref[l, 0] = chains[-1].cv_rows
            for c, rs in zip(chains, blocks):
                c.p3()
                x_new = x_rows(rs) + c.out
                if l + 1 < DEPTH:
                    x_buf[rs, :] = x_new
                else:
                    y_ref[0, rs, :] = _rmsnorm(x_new, fin_g)
                phase_done()

    def zero_hist():
        for carry in carries:
            carry[...] = jnp.zeros(carry.shape, F32)

    def sample_out_copies():
        return {
            "keep": pltpu.make_async_copy(sp_buf.at[:, pl.ds(1, POOL_BUF - 1)],
                                          pool_s_t_hbm.at[:, pl.ds(0, POOL_BUF - 1)], io_sems.at[2]),
            "new_row": pltpu.make_async_copy(a_in_s_buf, pool_s_t_hbm.at[:, POOL_BUF - 1], io_sems.at[3]),
            "ys": pltpu.make_async_copy(ys_buf, ys_hbm.at[:, 0, :], io_sems.at[4]),
            "vn": pltpu.make_async_copy(vn_s_buf, vn_s_hbm.at[:, :, 0, :], io_sems.at[5]),
        }

    def wait_sample_outs():
        for cp in sample_out_copies().values():
            cp.wait()

    first = jnp.logical_and(b == 0, t == 0)

    @pl.when(first)
    def _():
        xs_copy = pltpu.make_async_copy(xs_hbm.at[:, 0, :], xs_buf, io_sems.at[0])
        sp_copy = pltpu.make_async_copy(sp_hbm, sp_buf, io_sems.at[1])
        outs = sample_out_copies()

        def start_in_copies():
            xs_copy.start()
            sp_copy.start()

        def before_pool(l):
            if l == 0:
                sp_copy.wait()
                outs["keep"].start()

        stream = _WeightStream(w_in_hbm, w_out_hbm, w_in_bf, w_out_bf, stage, sems, start_in_copies)
        stream.start()
        for l in range(DEPTH):
            for hh in range(N_HEADS_B):
                bias_buf[l, :, hh * HEAD_DIM_B:(hh + 1) * HEAD_DIM_B] = jnp.broadcast_to(
                    b_s_ref[l, hh:hh + 1, :], (CHUNK, CHUNK)).T
        zero_hist()

        tile_body(stream.need, None)
        assert stream.done == len(stream.chunks)
        xs_copy.wait()
        _sample_group(xs_buf, sp_buf, norm_g_ref, b_s_ref, fin_g, R, ys_buf, a_in_s_buf, vn_s_buf, before_pool)
        for name in ("ys", "vn", "new_row"):
            outs[name].start()
        if single_step:
            wait_sample_outs()

    @pl.when(jnp.logical_not(first))
    def _():
        pl.when(t == 0)(zero_hist)
        tile_body()
        pl.when(b * pl.num_programs(1) + t == 1)(wait_sample_outs)


def _const_spec(shape):
    zeros = (0,) * len(shape)
    return pl.BlockSpec(shape, lambda b, t: zeros, pipeline_mode=pl.Buffered(1))


def _trunk_call(x, xs, sp_t, norm_g, w_in, w_pool, pool_scale, v_norm_g, w_s, b_s, w_out, fin_g):
    batch, seq, d = x.shape
    n_s = xs.shape[0]
    tm = ROW_TILE
    assert seq % tm == 0 and tm % (ROW_BLOCKS * CHUNK) == 0
    assert W_A == W_B == WEIGHT_CHUNK_COLS and d % WEIGHT_CHUNK_COLS == 0 and d == W_A + W_B
    hbm = pl.BlockSpec(memory_space=pl.ANY)
    args = (x, xs, sp_t, norm_g, w_in, w_pool, pool_scale, v_norm_g, w_s, b_s, w_out, fin_g)
    in_specs = []
    for a in args:
        if a is x:
            in_specs.append(pl.BlockSpec((1, tm, d), lambda b, t: (b, t, 0)))
        elif a is w_in or a is w_out or a is sp_t or a is xs:
            in_specs.append(hbm)
        else:
            in_specs.append(_const_spec(a.shape))
    return pl.pallas_call(
        functools.partial(_trunk_kernel, single_step=(batch * (seq // tm) == 1)),
        grid=(batch, seq // tm),
        in_specs=in_specs,
        out_specs=[
            pl.BlockSpec((1, tm, d), lambda b, t: (b, t, 0)),
            pl.BlockSpec((DEPTH, 1, POOL_BUF, W_A), lambda b, t: (0, b, 0, 0)),
            pl.BlockSpec((DEPTH, 1, CHUNK, W_B), lambda b, t: (0, b, 0, 0)),
            hbm, hbm, hbm,
        ],
        out_shape=[
            jax.ShapeDtypeStruct((batch, seq, d), F32),
            jax.ShapeDtypeStruct((DEPTH, batch, POOL_BUF, W_A), F32),
            jax.ShapeDtypeStruct((DEPTH, batch, CHUNK, W_B), F32),
            jax.ShapeDtypeStruct((n_s, 1, d), F32),
            jax.ShapeDtypeStruct((DEPTH, POOL_BUF, n_s, W_A), F32),
            jax.ShapeDtypeStruct((DEPTH, n_s, 1, W_B), F32),
        ],
        scratch_shapes=[
            pltpu.VMEM((HIST + tm, W_A), F32),
            pltpu.VMEM((HIST + tm, W_A - POOL_GROUP_DIM), F32),
            pltpu.VMEM((HIST + tm, W_A - 2 * POOL_GROUP_DIM), F32),
            pltpu.VMEM((HIST + tm, W_A - 3 * POOL_GROUP_DIM), F32),
            pltpu.VMEM((DEPTH, HIST, W_A), F32),
            pltpu.VMEM((DEPTH, HIST, W_A - POOL_GROUP_DIM), F32),
            pltpu.VMEM((DEPTH, HIST, W_A - 2 * POOL_GROUP_DIM), F32),
            pltpu.VMEM((DEPTH, HIST, W_A - 3 * POOL_GROUP_DIM), F32),
            pltpu.VMEM((tm, d), BF16),
            pltpu.VMEM((tm, d), F32),
            pltpu.VMEM((DEPTH, d, D_IN), BF16),
            pltpu.VMEM((DEPTH, W_A + W_B, d), BF16),
            pltpu.VMEM((DEPTH, CHUNK, W_B), F32),
            pltpu.VMEM((DEPTH, POOL_BUF, n_s, W_A), F32),
            pltpu.VMEM((n_s, d), F32),
            pltpu.VMEM((n_s, d), F32),
            pltpu.VMEM((DEPTH, n_s, W_A), F32),
            pltpu.VMEM((DEPTH, n_s, W_B), F32),
            pltpu.VMEM((WEIGHT_SLOTS, d, WEIGHT_CHUNK_COLS), F32),
            pltpu.SemaphoreType.DMA((WEIGHT_SLOTS,)),
            pltpu.SemaphoreType.DMA((6,)),
        ],
        compiler_params=pltpu.CompilerParams(
            dimension_semantics=("arbitrary", "arbitrary"),
            vmem_limit_bytes=VMEM_LIMIT_BYTES,
        ),
        name="trunk",
    )(*args)


def kernel(x_prompt, x_sample, state_pool, norm_g, w_in, w_pool, pool_scale, v_norm_g, w_s, b_s, w_out, final_norm_g):
    assert x_sample.shape[1] == 1 and state_pool.shape[2] == POOL_BUF
    sp_t = jnp.transpose(state_pool, (0, 2, 1, 3))
    y_prompt, pool_prompt, chunk_v_prompt, y_sample, pool_sample_t, chunk_v_sample = _trunk_call(
        x_prompt, x_sample, sp_t, norm_g, w_in, w_pool, pool_scale, v_norm_g, w_s, b_s, w_out, final_norm_g)
    pool_sample = jnp.transpose(pool_sample_t, (0, 2, 1, 3))
    return (y_prompt, y_sample, pool_prompt, pool_sample, chunk_v_prompt, chunk_v_sample)
```

```python
import functools

import jax
import jax.numpy as jnp
from jax import lax
from jax.experimental import pallas as pl
from jax.experimental.pallas import tpu as pltpu

D_MODEL = 1024
DEPTH = 2
W_A = 512
W_B = 512
D_IN = 2 * W_A + 3 * W_B
POOL_WINDOWS = (2, 4, 8, 16)
N_POOL_GROUPS = len(POOL_WINDOWS)
POOL_GROUP_DIM = W_A // N_POOL_GROUPS
POOL_BUF = max(POOL_WINDOWS) - 1
CHUNK = 128
N_HEADS_B = 4
HEAD_DIM_B = W_B // N_HEADS_B
EPS = 1e-6
SAMPLE_PAST_LEN = 16384

COL_A_IN = 0
COL_A_GATE = W_A
COL_U = 2 * W_A
COL_V = 2 * W_A + W_B
COL_B_GATE = 2 * W_A + 2 * W_B

HIST = 16
ROW_TILE = 512
ROW_BLOCKS = 2
WEIGHT_CHUNK_COLS = 512
WEIGHT_SLOTS = 5
VMEM_LIMIT_BYTES = 56 * 1024 * 1024

BF16 = jnp.bfloat16
F32 = jnp.float32


def _dot(a, b):
    return jnp.dot(a.astype(BF16), b.astype(BF16), preferred_element_type=F32)


def _rmsnorm(x, g):
    ms = jnp.mean(x * x, axis=-1, keepdims=True)
    return (x * lax.rsqrt(ms + EPS)) * g


def _silu(x):
    hx = 0.5 * x
    return hx * jnp.tanh(hx) + hx


def _head_rmsnorm(v, g):
    parts = []
    for h in range(N_HEADS_B):
        sl = slice(h * HEAD_DIM_B, (h + 1) * HEAD_DIM_B)
        parts.append(_rmsnorm(v[:, sl], g[:, sl]))
    return jnp.concatenate(parts, axis=-1)


def _masked_ws(ws_ref, l):
    row = lax.broadcasted_iota(jnp.int32, (CHUNK, CHUNK), 0)
    col = lax.broadcasted_iota(jnp.int32, (CHUNK, CHUNK), 1)
    keep = col <= row
    return [jnp.where(keep, ws_ref[l, h], 0.0).astype(BF16) for h in range(N_HEADS_B)]


class _Refs:
    def __init__(self, **kw):
        self.__dict__.update(kw)


def _row(ref, l):
    return ref[l:l + 1, :]


class _WeightStream:
    def __init__(self, w_in_hbm, w_out_hbm, w_in_bf, w_out_bf, stage, sems, after_last_start):
        self.chunks = []
        for l in range(DEPTH):
            for col in (COL_V, COL_A_IN, COL_A_GATE, COL_U, COL_B_GATE):
                self.chunks.append((w_in_hbm, w_in_bf, l, col))
            for col in range(0, w_out_hbm.shape[2], WEIGHT_CHUNK_COLS):
                self.chunks.append((w_out_hbm, w_out_bf, l, col))
        self.index = {(id(dst), l, col): i for i, (_, dst, l, col) in enumerate(self.chunks)}
        self.stage, self.sems = stage, sems
        self.after_last_start = after_last_start
        self.done = 0

    def _copy(self, i):
        src, _, l, col = self.chunks[i]
        slot = i % WEIGHT_SLOTS
        return pltpu.make_async_copy(
            src.at[l, :, pl.ds(col, WEIGHT_CHUNK_COLS)], self.stage.at[slot], self.sems.at[slot])

    def _start(self, i):
        self._copy(i).start()
        if i == len(self.chunks) - 1:
            self.after_last_start()

    def start(self):
        for i in range(WEIGHT_SLOTS - 1):
            self._start(i)

    def service(self):
        i = self.done
        if i == len(self.chunks):
            return
        if i + WEIGHT_SLOTS - 1 < len(self.chunks):
            self._start(i + WEIGHT_SLOTS - 1)
        self._copy(i).wait()
        _, dst, l, col = self.chunks[i]
        dst[l, :, col:col + WEIGHT_CHUNK_COLS] = self.stage[i % WEIGHT_SLOTS].astype(BF16)
        self.done += 1

    def need(self, dst, l, col):
        while self.done <= self.index[(id(dst), l, col)]:
            self.service()


class _Chain:
    def __init__(self, l, r0, rows, t, tm, R, need=None):
        self.l, self.r0, self.rows, self.t, self.tm, self.R = l, r0, rows, t, tm, R
        self.need = need if need is not None else (lambda dst, l, col: None)

    def _in(self, col, width):
        R = self.R
        self.need(R.w_in, self.l, col)
        return jnp.dot(R.h_buf[self.r0:self.r0 + self.rows, :], R.w_in[self.l, :, col:col + width],
                       preferred_element_type=F32)

    def p0(self):
        self.v = self._in(COL_V, W_B)
        self.a_in = self._in(COL_A_IN, W_A)

    def p1(self):
        self.a_gate = self._in(COL_A_GATE, W_A)
        self.u = self._in(COL_U, W_B)

    def p2(self):
        l, r0, rows, R = self.l, self.r0, self.rows, self.R
        n_chunks = rows // CHUNK
        a_in = self.a_in
        base = HIST + r0

        vn = _head_rmsnorm(self.v, _row(R.v_norm_g, l))
        self.cv_rows = vn[rows - CHUNK:, :]
        vn_b = vn.astype(BF16)

        a_hist, s2_hist, s4_hist, s8_hist = R.hists
        a_hist[base:base + rows, :] = a_in
        s2 = a_in + a_hist[base - 1:base - 1 + rows, :]
        s2_hist[base:base + rows, :] = s2[:, POOL_GROUP_DIM:]
        s4 = s2[:, POOL_GROUP_DIM:] + s2_hist[base - 2:base - 2 + rows, :]
        s4_hist[base:base + rows, :] = s4[:, POOL_GROUP_DIM:]
        s8 = s4[:, POOL_GROUP_DIM:] + s4_hist[base - 4:base - 4 + rows, :]
        s8_hist[base:base + rows, :] = s8[:, POOL_GROUP_DIM:]
        s16 = s8[:, POOL_GROUP_DIM:] + s8_hist[base - 8:base - 8 + rows, :]
        win_sums = (s2[:, :POOL_GROUP_DIM], s4[:, :POOL_GROUP_DIM], s8[:, :POOL_GROUP_DIM], s16)
        self.pool_rows = a_in[rows - POOL_BUF:, :]

        d_parts = []
        for g, w in enumerate(POOL_WINDOWS):
            sl = slice(g * POOL_GROUP_DIM, (g + 1) * POOL_GROUP_DIM)
            ssum = win_sums[g]
            if r0 == 0:
                pos1 = lax.broadcasted_iota(jnp.int32, (HIST, POOL_GROUP_DIM), 0) + self.t * self.tm + 1
                inv_head = 1.0 / jnp.minimum(pos1, w).astype(F32)
                mean = jnp.concatenate([ssum[:HIST] * inv_head, ssum[HIST:] * (1.0 / w)], axis=0)
            else:
                mean = ssum * (1.0 / w)
            d_parts.append((mean - a_in[:, sl]).astype(BF16))

        ws_b = _masked_ws(R.ws, l)
        s_heads = []
        for hh in range(N_HEADS_B):
            sl = slice(hh * HEAD_DIM_B, (hh + 1) * HEAD_DIM_B)
            rhs = jnp.concatenate([vn_b[c * CHUNK:(c + 1) * CHUNK, sl] for c in range(n_chunks)], axis=1)
            res = jnp.dot(ws_b[hh], rhs, preferred_element_type=F32)
            s_heads.append(jnp.concatenate(
                [res[:, c * HEAD_DIM_B:(c + 1) * HEAD_DIM_B] for c in range(n_chunks)], axis=0))
        self.s_heads = s_heads

        self.y_parts = [_dot(d_parts[g], R.w_pool[l, g]) for g in range(N_POOL_GROUPS)]
        self.b_gate = self._in(COL_B_GATE, W_B)

    def p3(self):
        l, rows, R = self.l, self.rows, self.R
        n_chunks = rows // CHUNK
        self.need(R.w_out, l, R.w_out.shape[2] - WEIGHT_CHUNK_COLS)
        a_out = ((jnp.concatenate(self.y_parts, axis=-1) * _row(R.pool_scale, l)) * _silu(self.a_gate)).astype(BF16)
        out_a = _dot(a_out, R.w_out[l, 0:W_A, :])
        s = jnp.concatenate(self.s_heads, axis=-1) + jnp.concatenate([R.bias[l]] * n_chunks, axis=0)
        b_out = ((self.u * s) * _silu(self.b_gate)).astype(BF16)
        self.out = out_a + _dot(b_out, R.w_out[l, W_A:, :])


def _per_head_lanes(vals):
    return jnp.concatenate([jnp.broadcast_to(v, (1, HEAD_DIM_B)) for v in vals], axis=-1)


def _sample_group(xs_ref, sp_ref, norm_g_ref, b_s_ref, fin_g, R, ys_ref, a_in_s_ref, vn_s_ref, before_pool):
    x = xs_ref[...]
    for l in range(DEPTH):
        h = _rmsnorm(x, _row(norm_g_ref, l)).astype(BF16)
        a_in = _dot(h, R.w_in[l, :, COL_A_IN:COL_A_IN + W_A])
        a_in_s_ref[l] = a_in
        a_gate = _dot(h, R.w_in[l, :, COL_A_GATE:COL_A_GATE + W_A])

        v = _dot(h, R.w_in[l, :, COL_V:COL_V + W_B])
        vn = _head_rmsnorm(v, _row(R.v_norm_g, l))
        vn_s_ref[l] = vn
        ws00 = _per_head_lanes([R.ws[l, hh, 0:1, 0:1] for hh in range(N_HEADS_B)])
        b0 = _per_head_lanes([b_s_ref[l, hh:hh + 1, 0:1] for hh in range(N_HEADS_B)])
        s = ws00 * vn + b0
        u = _dot(h, R.w_in[l, :, COL_U:COL_U + W_B])
        b_gate = _dot(h, R.w_in[l, :, COL_B_GATE:COL_B_GATE + W_B])
        b_out = (u * s) * _silu(b_gate)

        before_pool(l)
        a_parts = []
        for g, w in enumerate(POOL_WINDOWS):
            sl = slice(g * POOL_GROUP_DIM, (g + 1) * POOL_GROUP_DIM)
            ssum = a_in[:, sl]
            for k in range(1, w):
                ssum = ssum + sp_ref[l, POOL_BUF - k, :, sl]
            cnt = float(min(SAMPLE_PAST_LEN + 1, w))
            d = ssum / cnt - a_in[:, sl]
            yg = _dot(d, R.w_pool[l, g]) * _row(R.pool_scale, l)[:, sl]
            a_parts.append(yg * _silu(a_gate[:, sl]))
        a_out = jnp.concatenate(a_parts, axis=-1)

        x = x + (_dot(a_out, R.w_out[l, 0:W_A, :]) + _dot(b_out, R.w_out[l, W_A:, :]))
    ys_ref[...] = _rmsnorm(x, fin_g)


def _trunk_kernel(x_ref, xs_hbm, sp_hbm, norm_g_ref, w_in_hbm, w_pool_ref, pool_scale_ref, v_norm_g_ref,
                  ws_ref, b_s_ref, w_out_hbm, fin_g_ref,
                  y_ref, pool_ref, cv_ref, ys_hbm, pool_s_t_hbm, vn_s_hbm,
                  a_hist, s2_hist, s4_hist, s8_hist, a_carry, s2_carry, s4_carry, s8_carry,
                  h_buf, x_buf, w_in_bf, w_out_bf, bias_buf, sp_buf,
                  xs_buf, ys_buf, a_in_s_buf, vn_s_buf, stage, sems, io_sems, *, single_step):
    b = pl.program_id(0)
    t = pl.program_id(1)
    tm = x_ref.shape[1]
    rows = tm // ROW_BLOCKS
    R = _Refs(w_in=w_in_bf, w_pool=w_pool_ref, pool_scale=pool_scale_ref, v_norm_g=v_norm_g_ref, ws=ws_ref,
              bias=bias_buf, w_out=w_out_bf, hists=(a_hist, s2_hist, s4_hist, s8_hist), h_buf=h_buf)
    hists = R.hists
    carries = (a_carry, s2_carry, s4_carry, s8_carry)
    fin_g = fin_g_ref[...].reshape(1, D_MODEL)

    def tile_body(need=None, between_phases=None):
        for l in range(DEPTH):

            def x_rows(rs, l=l):
                return x_ref[0, rs, :] if l == 0 else x_buf[rs, :]

            def phase_done():
                if between_phases is not None:
                    between_phases()

            blocks = [slice(k * rows, (k + 1) * rows) for k in range(ROW_BLOCKS)]
            for rs in blocks:
                h_buf[rs, :] = _rmsnorm(x_rows(rs), _row(norm_g_ref, l)).astype(BF16)
            chains = [_Chain(l, k * rows, rows, t, tm, R, need) for k in range(ROW_BLOCKS)]
            for c in chains:
                c.p0()
            phase_done()
            for c in chains:
                c.p1()
            phase_done()
            for buf, carry in zip(hists, carries):
                buf[0:HIST, :] = jnp.where(t == 0, 0.0, carry[l])
            for c in chains:
                c.p2()
                phase_done()
            for buf, carry in zip(hists, carries):
                carry[l] = buf[tm:tm + HIST, :]
            pool_ref[l, 0] = chains[-1].pool_rows
            cv_ref[l, 0] = chains[-1].cv_rows
            for c, rs in zip(chains, blocks):
                c.p3()
                x_new = x_rows(rs) + c.out
                if l + 1 < DEPTH:
                    x_buf[rs, :] = x_new
                else:
                    y_ref[0, rs, :] = _rmsnorm(x_new, fin_g)
                phase_done()

    def sample_out_copies():
        return {
            "keep": pltpu.make_async_copy(sp_buf.at[:, pl.ds(1, POOL_BUF - 1)],
                                          pool_s_t_hbm.at[:, pl.ds(0, POOL_BUF - 1)], io_sems.at[2]),
            "new_row": pltpu.make_async_copy(a_in_s_buf, pool_s_t_hbm.at[:, POOL_BUF - 1], io_sems.at[3]),
            "ys": pltpu.make_async_copy(ys_buf, ys_hbm.at[:, 0, :], io_sems.at[4]),
            "vn": pltpu.make_async_copy(vn_s_buf, vn_s_hbm.at[:, :, 0, :], io_sems.at[5]),
        }

    def wait_sample_outs():
        for cp in sample_out_copies().values():
            cp.wait()

    first = jnp.logical_and(b == 0, t == 0)

    @pl.when(first)
    def _():
        xs_copy = pltpu.make_async_copy(xs_hbm.at[:, 0, :], xs_buf, io_sems.at[0])
        sp_copy = pltpu.make_async_copy(sp_hbm, sp_buf, io_sems.at[1])
        outs = sample_out_copies()

        def start_in_copies():
            xs_copy.start()
            sp_copy.start()

        def before_pool(l):
            if l == 0:
                sp_copy.wait()
                outs["keep"].start()

        stream = _WeightStream(w_in_hbm, w_out_hbm, w_in_bf, w_out_bf, stage, sems, start_in_copies)
        stream.start()
        for l in range(DEPTH):
            for hh in range(N_HEADS_B):
                bias_buf[l, :, hh * HEAD_DIM_B:(hh + 1) * HEAD_DIM_B] = jnp.broadcast_to(
                    b_s_ref[l, hh:hh + 1, :], (CHUNK, CHUNK)).T

        tile_body(stream.need, stream.service)
        assert stream.done == len(stream.chunks)
        xs_copy.wait()
        _sample_group(xs_buf, sp_buf, norm_g_ref, b_s_ref, fin_g, R, ys_buf, a_in_s_buf, vn_s_buf, before_pool)
        for name in ("ys", "vn", "new_row"):
            outs[name].start()
        if single_step:
            wait_sample_outs()

    @pl.when(jnp.logical_not(first))
    def _():
        tile_body()
        pl.when(b * pl.num_programs(1) + t == 1)(wait_sample_outs)


def _const_spec(shape):
    zeros = (0,) * len(shape)
    return pl.BlockSpec(shape, lambda b, t: zeros, pipeline_mode=pl.Buffered(1))


def _trunk_call(x, xs, sp_t, norm_g, w_in, w_pool, pool_scale, v_norm_g, w_s, b_s, w_out, fin_g):
    batch, seq, d = x.shape
    n_s = xs.shape[0]
    tm = ROW_TILE
    assert seq % tm == 0 and tm % (ROW_BLOCKS * CHUNK) == 0
    assert W_A == W_B == WEIGHT_CHUNK_COLS and d % WEIGHT_CHUNK_COLS == 0 and d == W_A + W_B
    hbm = pl.BlockSpec(memory_space=pl.ANY)
    args = (x, xs, sp_t, norm_g, w_in, w_pool, pool_scale, v_norm_g, w_s, b_s, w_out, fin_g)
    in_specs = []
    for a in args:
        if a is x:
            in_specs.append(pl.BlockSpec((1, tm, d), lambda b, t: (b, t, 0)))
        elif a is w_in or a is w_out or a is sp_t or a is xs:
            in_specs.append(hbm)
        else:
            in_specs.append(_const_spec(a.shape))
    return pl.pallas_call(
        functools.partial(_trunk_kernel, single_step=(batch * (seq // tm) == 1)),
        grid=(batch, seq // tm),
        in_specs=in_specs,
        out_specs=[
            pl.BlockSpec((1, tm, d), lambda b, t: (b, t, 0)),
            pl.BlockSpec((DEPTH, 1, POOL_BUF, W_A), lambda b, t: (0, b, 0, 0)),
            pl.BlockSpec((DEPTH, 1, CHUNK, W_B), lambda b, t: (0, b, 0, 0)),
            hbm, hbm, hbm,
        ],
        out_shape=[
            jax.ShapeDtypeStruct((batch, seq, d), F32),
            jax.ShapeDtypeStruct((DEPTH, batch, POOL_BUF, W_A), F32),
            jax.ShapeDtypeStruct((DEPTH, batch, CHUNK, W_B), F32),
            jax.ShapeDtypeStruct((n_s, 1, d), F32),
            jax.ShapeDtypeStruct((DEPTH, POOL_BUF, n_s, W_A), F32),
            jax.ShapeDtypeStruct((DEPTH, n_s, 1, W_B), F32),
        ],
        scratch_shapes=[
            pltpu.VMEM((HIST + tm, W_A), F32),
            pltpu.VMEM((HIST + tm, W_A - POOL_GROUP_DIM), F32),
            pltpu.VMEM((HIST + tm, W_A - 2 * POOL_GROUP_DIM), F32),
            pltpu.VMEM((HIST + tm, W_A - 3 * POOL_GROUP_DIM), F32),
            pltpu.VMEM((DEPTH, HIST, W_A), F32),
            pltpu.VMEM((DEPTH, HIST, W_A - POOL_GROUP_DIM), F32),
            pltpu.VMEM((DEPTH, HIST, W_A - 2 * POOL_GROUP_DIM), F32),
            pltpu.VMEM((DEPTH, HIST, W_A - 3 * POOL_GROUP_DIM), F32),
            pltpu.VMEM((tm, d), BF16),
            pltpu.VMEM((tm, d), F32),
            pltpu.VMEM((DEPTH, d, D_IN), BF16),
            pltpu.VMEM((DEPTH, W_A + W_B, d), BF16),
            pltpu.VMEM((DEPTH, CHUNK, W_B), F32),
            pltpu.VMEM((DEPTH, POOL_BUF, n_s, W_A), F32),
            pltpu.VMEM((n_s, d), F32),
            pltpu.VMEM((n_s, d), F32),
            pltpu.VMEM((DEPTH, n_s, W_A), F32),
            pltpu.VMEM((DEPTH, n_s, W_B), F32),
            pltpu.VMEM((WEIGHT_SLOTS, d, WEIGHT_CHUNK_COLS), F32),
            pltpu.SemaphoreType.DMA((WEIGHT_SLOTS,)),
            pltpu.SemaphoreType.DMA((6,)),
        ],
        compiler_params=pltpu.CompilerParams(
            dimension_semantics=("arbitrary", "arbitrary"),
            vmem_limit_bytes=VMEM_LIMIT_BYTES,
        ),
        name="trunk",
    )(*args)


def kernel(x_prompt, x_sample, state_pool, norm_g, w_in, w_pool, pool_scale, v_norm_g, w_s, b_s, w_out, final_norm_g):
    assert x_sample.shape[1] == 1 and state_pool.shape[2] == POOL_BUF
    sp_t = jnp.transpose(state_pool, (0, 2, 1, 3))
    y_prompt, pool_prompt, chunk_v_prompt, y_sample, pool_sample_t, chunk_v_sample = _trunk_call(
        x_prompt, x_sample, sp_t, norm_g, w_in, w_pool, pool_scale, v_norm_g, w_s, b_s, w_out, final_norm_g)
    pool_sample = jnp.transpose(pool_sample_t, (0, 2, 1, 3))
    return (y_prompt, y_sample, pool_prompt, pool_sample, chunk_v_prompt, chunk_v_sample)
```

```python
import functools

import jax
import jax.numpy as jnp
from jax import lax
from jax.experimental import pallas as pl
from jax.experimental.pallas import tpu as pltpu

D_MODEL = 1024
DEPTH = 2
W_A = 512
W_B = 512
D_IN = 2 * W_A + 3 * W_B
POOL_WINDOWS = (2, 4, 8, 16)
N_POOL_GROUPS = len(POOL_WINDOWS)
POOL_GROUP_DIM = W_A // N_POOL_GROUPS
POOL_BUF = max(POOL_WINDOWS) - 1
CHUNK = 128
N_HEADS_B = 4
HEAD_DIM_B = W_B // N_HEADS_B
EPS = 1e-6
SAMPLE_PAST_LEN = 16384

COL_A_IN = 0
COL_A_GATE = W_A
COL_U = 2 * W_A
COL_V = 2 * W_A + W_B
COL_B_GATE = 2 * W_A + 2 * W_B

HIST = 16
ROW_TILE = 512
ROW_BLOCKS = 2
WEIGHT_CHUNK_COLS = 512
WEIGHT_SLOTS = 5
VMEM_LIMIT_BYTES = 56 * 1024 * 1024

BF16 = jnp.bfloat16
F32 = jnp.float32


def _dot(a, b):
    return jnp.dot(a.astype(BF16), b.astype(BF16), preferred_element_type=F32)


def _rmsnorm(x, g):
    ms = jnp.mean(x * x, axis=-1, keepdims=True)
    return (x * lax.rsqrt(ms + EPS)) * g


def _silu(x):
    hx = 0.5 * x
    return hx * jnp.tanh(hx) + hx


def _head_rmsnorm(v, g):
    parts = []
    for h in range(N_HEADS_B):
        sl = slice(h * HEAD_DIM_B, (h + 1) * HEAD_DIM_B)
        parts.append(_rmsnorm(v[:, sl], g[:, sl]))
    return jnp.concatenate(parts, axis=-1)


def _masked_ws(ws_ref, l):
    row = lax.broadcasted_iota(jnp.int32, (CHUNK, CHUNK), 0)
    col = lax.broadcasted_iota(jnp.int32, (CHUNK, CHUNK), 1)
    keep = col <= row
    return [jnp.where(keep, ws_ref[l, h], 0.0).astype(BF16) for h in range(N_HEADS_B)]


class _Refs:
    def __init__(self, **kw):
        self.__dict__.update(kw)


def _row(ref, l):
    return ref[l:l + 1, :]


class _WeightStream:
    def __init__(self, w_in_hbm, w_out_hbm, w_in_bf, w_out_bf, stage, sems, after_last_start):
        self.chunks = []
        for l in range(DEPTH):
            for col in (COL_V, COL_A_IN, COL_A_GATE, COL_U, COL_B_GATE):
                self.chunks.append((w_in_hbm, w_in_bf, l, col))
            for col in range(0, w_out_hbm.shape[2], WEIGHT_CHUNK_COLS):
                self.chunks.append((w_out_hbm, w_out_bf, l, col))
        self.index = {(id(dst), l, col): i for i, (_, dst, l, col) in enumerate(self.chunks)}
        self.stage, self.sems = stage, sems
        self.after_last_start = after_last_start
        self.done = 0

    def _copy(self, i):
        src, _, l, col = self.chunks[i]
        slot = i % WEIGHT_SLOTS
        return pltpu.make_async_copy(
            src.at[l, :, pl.ds(col, WEIGHT_CHUNK_COLS)], self.stage.at[slot], self.sems.at[slot])

    def _start(self, i):
        self._copy(i).start(priority=i % 2)
        if i == len(self.chunks) - 1:
            self.after_last_start()

    def start(self):
        for i in range(WEIGHT_SLOTS - 1):
            self._start(i)

    def service(self):
        i = self.done
        if i == len(self.chunks):
            return
        if i + WEIGHT_SLOTS - 1 < len(self.chunks):
            self._start(i + WEIGHT_SLOTS - 1)
        self._copy(i).wait()
        _, dst, l, col = self.chunks[i]
        dst[l, :, col:col + WEIGHT_CHUNK_COLS] = self.stage[i % WEIGHT_SLOTS].astype(BF16)
        self.done += 1

    def need(self, dst, l, col):
        while self.done <= self.index[(id(dst), l, col)]:
            self.service()


class _Chain:
    def __init__(self, l, r0, rows, t, tm, R, need=None):
        self.l, self.r0, self.rows, self.t, self.tm, self.R = l, r0, rows, t, tm, R
        self.need = need if need is not None else (lambda dst, l, col: None)

    def _in(self, col, width):
        R = self.R
        self.need(R.w_in, self.l, col)
        return jnp.dot(R.h_buf[self.r0:self.r0 + self.rows, :], R.w_in[self.l, :, col:col + width],
                       preferred_element_type=F32)

    def p0(self):
        self.v = self._in(COL_V, W_B)
        self.a_in = self._in(COL_A_IN, W_A)

    def p1(self):
        self.a_gate = self._in(COL_A_GATE, W_A)
        self.u = self._in(COL_U, W_B)

    def p2(self):
        l, r0, rows, R = self.l, self.r0, self.rows, self.R
        n_chunks = rows // CHUNK
        a_in = self.a_in
        base = HIST + r0

        vn = _head_rmsnorm(self.v, _row(R.v_norm_g, l))
        self.cv_rows = vn[rows - CHUNK:, :]
        vn_b = vn.astype(BF16)

        a_hist, s2_hist, s4_hist, s8_hist = R.hists
        a_hist[base:base + rows, :] = a_in
        s2 = a_in + a_hist[base - 1:base - 1 + rows, :]
        s2_hist[base:base + rows, :] = s2[:, POOL_GROUP_DIM:]
        s4 = s2[:, POOL_GROUP_DIM:] + s2_hist[base - 2:base - 2 + rows, :]
        s4_hist[base:base + rows, :] = s4[:, POOL_GROUP_DIM:]
        s8 = s4[:, POOL_GROUP_DIM:] + s4_hist[base - 4:base - 4 + rows, :]
        s8_hist[base:base + rows, :] = s8[:, POOL_GROUP_DIM:]
        s16 = s8[:, POOL_GROUP_DIM:] + s8_hist[base - 8:base - 8 + rows, :]
        win_sums = (s2[:, :POOL_GROUP_DIM], s4[:, :POOL_GROUP_DIM], s8[:, :POOL_GROUP_DIM], s16)
        self.pool_rows = a_in[rows - POOL_BUF:, :]

        d_parts = []
        for g, w in enumerate(POOL_WINDOWS):
            sl = slice(g * POOL_GROUP_DIM, (g + 1) * POOL_GROUP_DIM)
            ssum = win_sums[g]
            if r0 == 0:
                pos1 = lax.broadcasted_iota(jnp.int32, (HIST, POOL_GROUP_DIM), 0) + self.t * self.tm + 1
                inv_head = 1.0 / jnp.minimum(pos1, w).astype(F32)
                mean = jnp.concatenate([ssum[:HIST] * inv_head, ssum[HIST:] * (1.0 / w)], axis=0)
            else:
                mean = ssum * (1.0 / w)
            d_parts.append((mean - a_in[:, sl]).astype(BF16))

        ws_b = _masked_ws(R.ws, l)
        s_heads = []
        for hh in range(N_HEADS_B):
            sl = slice(hh * HEAD_DIM_B, (hh + 1) * HEAD_DIM_B)
            rhs = jnp.concatenate([vn_b[c * CHUNK:(c + 1) * CHUNK, sl] for c in range(n_chunks)], axis=1)
            res = jnp.dot(ws_b[hh], rhs, preferred_element_type=F32)
            s_heads.append(jnp.concatenate(
                [res[:, c * HEAD_DIM_B:(c + 1) * HEAD_DIM_B] for c in range(n_chunks)], axis=0))
        self.s_heads = s_heads

        self.y_parts = [_dot(d_parts[g], R.w_pool[l, g]) for g in range(N_POOL_GROUPS)]
        self.b_gate = self._in(COL_B_GATE, W_B)

    def p3(self):
        l, rows, R = self.l, self.rows, self.R
        n_chunks = rows // CHUNK
        self.need(R.w_out, l, R.w_out.shape[2] - WEIGHT_CHUNK_COLS)
        a_out = ((jnp.concatenate(self.y_parts, axis=-1) * _row(R.pool_scale, l)) * _silu(self.a_gate)).astype(BF16)
        out_a = _dot(a_out, R.w_out[l, 0:W_A, :])
        s = jnp.concatenate(self.s_heads, axis=-1) + jnp.concatenate([R.bias[l]] * n_chunks, axis=0)
        b_out = ((self.u * s) * _silu(self.b_gate)).astype(BF16)
        self.out = out_a + _dot(b_out, R.w_out[l, W_A:, :])


def _per_head_lanes(vals):
    return jnp.concatenate([jnp.broadcast_to(v, (1, HEAD_DIM_B)) for v in vals], axis=-1)


def _sample_group(xs_ref, sp_ref, norm_g_ref, b_s_ref, fin_g, R, ys_ref, a_in_s_ref, vn_s_ref, before_pool):
    x = xs_ref[...]
    for l in range(DEPTH):
        h = _rmsnorm(x, _row(norm_g_ref, l)).astype(BF16)
        a_in = _dot(h, R.w_in[l, :, COL_A_IN:COL_A_IN + W_A])
        a_in_s_ref[l] = a_in
        a_gate = _dot(h, R.w_in[l, :, COL_A_GATE:COL_A_GATE + W_A])

        v = _dot(h, R.w_in[l, :, COL_V:COL_V + W_B])
        vn = _head_rmsnorm(v, _row(R.v_norm_g, l))
        vn_s_ref[l] = vn
        ws00 = _per_head_lanes([R.ws[l, hh, 0:1, 0:1] for hh in range(N_HEADS_B)])
        b0 = _per_head_lanes([b_s_ref[l, hh:hh + 1, 0:1] for hh in range(N_HEADS_B)])
        s = ws00 * vn + b0
        u = _dot(h, R.w_in[l, :, COL_U:COL_U + W_B])
        b_gate = _dot(h, R.w_in[l, :, COL_B_GATE:COL_B_GATE + W_B])
        b_out = (u * s) * _silu(b_gate)

        before_pool(l)
        a_parts = []
        for g, w in enumerate(POOL_WINDOWS):
            sl = slice(g * POOL_GROUP_DIM, (g + 1) * POOL_GROUP_DIM)
            ssum = a_in[:, sl]
            for k in range(1, w):
                ssum = ssum + sp_ref[l, POOL_BUF - k, :, sl]
            cnt = float(min(SAMPLE_PAST_LEN + 1, w))
            d = ssum / cnt - a_in[:, sl]
            yg = _dot(d, R.w_pool[l, g]) * _row(R.pool_scale, l)[:, sl]
            a_parts.append(yg * _silu(a_gate[:, sl]))
        a_out = jnp.concatenate(a_parts, axis=-1)

        x = x + (_dot(a_out, R.w_out[l, 0:W_A, :]) + _dot(b_out, R.w_out[l, W_A:, :]))
    ys_ref[...] = _rmsnorm(x, fin_g)


def _trunk_kernel(x_ref, xs_hbm, sp_hbm, norm_g_ref, w_in_hbm, w_pool_ref, pool_scale_ref, v_norm_g_ref,
                  ws_ref, b_s_ref, w_out_hbm, fin_g_ref,
                  y_ref, pool_ref, cv_ref, ys_hbm, pool_s_t_hbm, vn_s_hbm,
                  a_hist, s2_hist, s4_hist, s8_hist, a_carry, s2_carry, s4_carry, s8_carry,
                  h_buf, x_buf, w_in_bf, w_out_bf, bias_buf, sp_buf,
                  xs_buf, ys_buf, a_in_s_buf, vn_s_buf, stage, sems, io_sems, *, single_step):
    b = pl.program_id(0)
    t = pl.program_id(1)
    tm = x_ref.shape[1]
    rows = tm // ROW_BLOCKS
    R = _Refs(w_in=w_in_bf, w_pool=w_pool_ref, pool_scale=pool_scale_ref, v_norm_g=v_norm_g_ref, ws=ws_ref,
              bias=bias_buf, w_out=w_out_bf, hists=(a_hist, s2_hist, s4_hist, s8_hist), h_buf=h_buf)
    hists = R.hists
    carries = (a_carry, s2_carry, s4_carry, s8_carry)
    fin_g = fin_g_ref[...].reshape(1, D_MODEL)

    def tile_body(need=None, between_phases=None):
        for l in range(DEPTH):

            def x_rows(rs, l=l):
                return x_ref[0, rs, :] if l == 0 else x_buf[rs, :]

            def phase_done():
                if between_phases is not None:
                    between_phases()

            blocks = [slice(k * rows, (k + 1) * rows) for k in range(ROW_BLOCKS)]
            for rs in blocks:
                h_buf[rs, :] = _rmsnorm(x_rows(rs), _row(norm_g_ref, l)).astype(BF16)
            chains = [_Chain(l, k * rows, rows, t, tm, R, need) for k in range(ROW_BLOCKS)]
            for c in chains:
                c.p0()
            phase_done()
            for c in chains:
                c.p1()
            phase_done()
            for buf, carry in zip(hists, carries):
                buf[0:HIST, :] = carry[l]
            for c in chains:
                c.p2()
                phase_done()
            for buf, carry in zip(hists, carries):
                carry[l] = buf[tm:tm + HIST, :]
            pool_ref[l, 0] = chains[-1].pool_rows
            cv_ref[l, 0] = chains[-1].cv_rows
            for c, rs in zip(chains, blocks):
                c.p3()
                x_new = x_rows(rs) + c.out
                if l + 1 < DEPTH:
                    x_buf[rs, :] = x_new
                else:
                    y_ref[0, rs, :] = _rmsnorm(x_new, fin_g)
                phase_done()

    def zero_hist():
        for carry in carries:
            carry[...] = jnp.zeros(carry.shape, F32)

    def sample_out_copies():
        return {
            "keep": pltpu.make_async_copy(sp_buf.at[:, pl.ds(1, POOL_BUF - 1)],
                                          pool_s_t_hbm.at[:, pl.ds(0, POOL_BUF - 1)], io_sems.at[2]),
            "new_row": pltpu.make_async_copy(a_in_s_buf, pool_s_t_hbm.at[:, POOL_BUF - 1], io_sems.at[3]),
            "ys": pltpu.make_async_copy(ys_buf, ys_hbm.at[:, 0, :], io_sems.at[4]),
            "vn": pltpu.make_async_copy(vn_s_buf, vn_s_hbm.at[:, :, 0, :], io_sems.at[5]),
        }

    def wait_sample_outs():
        for cp in sample_out_copies().values():
            cp.wait()

    first = jnp.logical_and(b == 0, t == 0)

    @pl.when(first)
    def _():
        xs_copy = pltpu.make_async_copy(xs_hbm.at[:, 0, :], xs_buf, io_sems.at[0])
        sp_copy = pltpu.make_async_copy(sp_hbm, sp_buf, io_sems.at[1])
        outs = sample_out_copies()

        def start_in_copies():
            xs_copy.start()
            sp_copy.start()

        def before_pool(l):
            if l == 0:
                sp_copy.wait()
                outs["keep"].start()

        stream = _WeightStream(w_in_hbm, w_out_hbm, w_in_bf, w_out_bf, stage, sems, start_in_copies)
        stream.start()
        for l in range(DEPTH):
            for hh in range(N_HEADS_B):
                bias_buf[l, :, hh * HEAD_DIM_B:(hh + 1) * HEAD_DIM_B] = jnp.broadcast_to(
                    b_s_ref[l, hh:hh + 1, :], (CHUNK, CHUNK)).T
        zero_hist()

        tile_body(stream.need, stream.service)
        assert stream.done == len(stream.chunks)
        xs_copy.wait()
        _sample_group(xs_buf, sp_buf, norm_g_ref, b_s_ref, fin_g, R, ys_buf, a_in_s_buf, vn_s_buf, before_pool)
        for name in ("ys", "vn", "new_row"):
            outs[name].start()
        if single_step:
            wait_sample_outs()

    @pl.when(jnp.logical_not(first))
    def _():
        pl.when(t == 0)(zero_hist)
        tile_body()
        pl.when(b * pl.num_programs(1) + t == 1)(wait_sample_outs)


def _const_spec(shape):
    zeros = (0,) * len(shape)
    return pl.BlockSpec(shape, lambda b, t: zeros, pipeline_mode=pl.Buffered(1))


def _trunk_call(x, xs, sp_t, norm_g, w_in, w_pool, pool_scale, v_norm_g, w_s, b_s, w_out, fin_g):
    batch, seq, d = x.shape
    n_s = xs.shape[0]
    tm = ROW_TILE
    assert seq % tm == 0 and tm % (ROW_BLOCKS * CHUNK) == 0
    assert W_A == W_B == WEIGHT_CHUNK_COLS and d % WEIGHT_CHUNK_COLS == 0 and d == W_A + W_B
    hbm = pl.BlockSpec(memory_space=pl.ANY)
    args = (x, xs, sp_t, norm_g, w_in, w_pool, pool_scale, v_norm_g, w_s, b_s, w_out, fin_g)
    in_specs = []
    for a in args:
        if a is x:
            in_specs.append(pl.BlockSpec((1, tm, d), lambda b, t: (b, t, 0)))
        elif a is w_in or a is w_out or a is sp_t or a is xs:
            in_specs.append(hbm)
        else:
            in_specs.append(_const_spec(a.shape))
    return pl.pallas_call(
        functools.partial(_trunk_kernel, single_step=(batch * (seq // tm) == 1)),
        grid=(batch, seq // tm),
        in_specs=in_specs,
        out_specs=[
            pl.BlockSpec((1, tm, d), lambda b, t: (b, t, 0)),
            pl.BlockSpec((DEPTH, 1, POOL_BUF, W_A), lambda b, t: (0, b, 0, 0)),
            pl.BlockSpec((DEPTH, 1, CHUNK, W_B), lambda b, t: (0, b, 0, 0)),
            hbm, hbm, hbm,
        ],
        out_shape=[
            jax.ShapeDtypeStruct((batch, seq, d), F32),
            jax.ShapeDtypeStruct((DEPTH, batch, POOL_BUF, W_A), F32),
            jax.ShapeDtypeStruct((DEPTH, batch, CHUNK, W_B), F32),
            jax.ShapeDtypeStruct((n_s, 1, d), F32),
            jax.ShapeDtypeStruct((DEPTH, POOL_BUF, n_s, W_A), F32),
            jax.ShapeDtypeStruct((DEPTH, n_s, 1, W_B), F32),
        ],
        scratch_shapes=[
            pltpu.VMEM((HIST + tm, W_A), F32),
            pltpu.VMEM((HIST + tm, W_A - POOL_GROUP_DIM), F32),
            pltpu.VMEM((HIST + tm, W_A - 2 * POOL_GROUP_DIM), F32),
            pltpu.VMEM((HIST + tm, W_A - 3 * POOL_GROUP_DIM), F32),
            pltpu.VMEM((DEPTH, HIST, W_A), F32),
            pltpu.VMEM((DEPTH, HIST, W_A - POOL_GROUP_DIM), F32),
            pltpu.VMEM((DEPTH, HIST, W_A - 2 * POOL_GROUP_DIM), F32),
            pltpu.VMEM((DEPTH, HIST, W_A - 3 * POOL_GROUP_DIM), F32),
            pltpu.VMEM((tm, d), BF16),
            pltpu.VMEM((tm, d), F32),
            pltpu.VMEM((DEPTH, d, D_IN), BF16),
            pltpu.VMEM((DEPTH, W_A + W_B, d), BF16),
            pltpu.VMEM((DEPTH, CHUNK, W_B), F32),
            pltpu.VMEM((DEPTH, POOL_BUF, n_s, W_A), F32),
            pltpu.VMEM((n_s, d), F32),
            pltpu.VMEM((n_s, d), F32),
            pltpu.VMEM((DEPTH, n_s, W_A), F32),
            pltpu.VMEM((DEPTH, n_s, W_B), F32),
            pltpu.VMEM((WEIGHT_SLOTS, d, WEIGHT_CHUNK_COLS), F32),
            pltpu.SemaphoreType.DMA((WEIGHT_SLOTS,)),
            pltpu.SemaphoreType.DMA((6,)),
        ],
        compiler_params=pltpu.CompilerParams(
            dimension_semantics=("arbitrary", "arbitrary"),
            vmem_limit_bytes=VMEM_LIMIT_BYTES,
        ),
        name="trunk",
    )(*args)


def kernel(x_prompt, x_sample, state_pool, norm_g, w_in, w_pool, pool_scale, v_norm_g, w_s, b_s, w_out, final_norm_g):
    assert x_sample.shape[1] == 1 and state_pool.shape[2] == POOL_BUF
    sp_t = jnp.transpose(state_pool, (0, 2, 1, 3))
    y_prompt, pool_prompt, chunk_v_prompt, y_sample, pool_sample_t, chunk_v_sample = _trunk_call(
        x_prompt, x_sample, sp_t, norm_g, w_in, w_pool, pool_scale, v_norm_g, w_s, b_s, w_out, final_norm_g)
    pool_sample = jnp.transpose(pool_sample_t, (0, 2, 1, 3))
    return (y_prompt, y_sample, pool_prompt, pool_sample, chunk_v_prompt, chunk_v_sample)
```

```python
import functools

import jax
import jax.numpy as jnp
from jax import lax
from jax.experimental import pallas as pl
from jax.experimental.pallas import tpu as pltpu

D_MODEL = 1024
DEPTH = 2
W_A = 512
W_B = 512
D_IN = 2 * W_A + 3 * W_B
POOL_WINDOWS = (2, 4, 8, 16)
N_POOL_GROUPS = len(POOL_WINDOWS)
POOL_GROUP_DIM = W_A // N_POOL_GROUPS
POOL_BUF = max(POOL_WINDOWS) - 1
CHUNK = 128
N_HEADS_B = 4
HEAD_DIM_B = W_B // N_HEADS_B
EPS = 1e-6
SAMPLE_PAST_LEN = 16384

COL_A_IN = 0
COL_A_GATE = W_A
COL_U = 2 * W_A
COL_V = 2 * W_A + W_B
COL_B_GATE = 2 * W_A + 2 * W_B

HIST = 16
HIST_PAD = 8
ROW_TILE = 512
ROW_BLOCKS = 2
WEIGHT_CHUNK_COLS = 512
WEIGHT_SLOTS = 5
VMEM_LIMIT_BYTES = 56 * 1024 * 1024

BF16 = jnp.bfloat16
F32 = jnp.float32


def _dot(a, b):
    return jnp.dot(a.astype(BF16), b.astype(BF16), preferred_element_type=F32)


def _rmsnorm(x, g):
    ms = jnp.mean(x * x, axis=-1, keepdims=True)
    return (x * lax.rsqrt(ms + EPS)) * g


def _silu(x):
    hx = 0.5 * x
    return hx * jnp.tanh(hx) + hx


def _head_rmsnorm(v, g):
    parts = []
    for h in range(N_HEADS_B):
        sl = slice(h * HEAD_DIM_B, (h + 1) * HEAD_DIM_B)
        parts.append(_rmsnorm(v[:, sl], g[:, sl]))
    return jnp.concatenate(parts, axis=-1)


def _masked_ws(ws_ref, l):
    row = lax.broadcasted_iota(jnp.int32, (CHUNK, CHUNK), 0)
    col = lax.broadcasted_iota(jnp.int32, (CHUNK, CHUNK), 1)
    keep = col <= row
    return [jnp.where(keep, ws_ref[l, h], 0.0).astype(BF16) for h in range(N_HEADS_B)]


class _Refs:
    def __init__(self, **kw):
        self.__dict__.update(kw)


def _row(ref, l):
    return ref[l:l + 1, :]


class _WeightStream:
    def __init__(self, w_in_hbm, w_out_hbm, w_in_bf, w_out_bf, stage, sems, after_last_start):
        self.chunks = []
        for l in range(DEPTH):
            for col in (COL_V, COL_A_IN, COL_A_GATE, COL_U, COL_B_GATE):
                self.chunks.append((w_in_hbm, w_in_bf, l, col))
            for col in range(0, w_out_hbm.shape[2], WEIGHT_CHUNK_COLS):
                self.chunks.append((w_out_hbm, w_out_bf, l, col))
        self.index = {(id(dst), l, col): i for i, (_, dst, l, col) in enumerate(self.chunks)}
        self.stage, self.sems = stage, sems
        self.after_last_start = after_last_start
        self.done = 0

    def _copy(self, i):
        src, _, l, col = self.chunks[i]
        slot = i % WEIGHT_SLOTS
        return pltpu.make_async_copy(
            src.at[l, :, pl.ds(col, WEIGHT_CHUNK_COLS)], self.stage.at[slot], self.sems.at[slot])

    def _start(self, i):
        self._copy(i).start()
        if i == len(self.chunks) - 1:
            self.after_last_start()

    def start(self):
        for i in range(WEIGHT_SLOTS - 1):
            self._start(i)

    def service(self):
        i = self.done
        if i == len(self.chunks):
            return
        if i + WEIGHT_SLOTS - 1 < len(self.chunks):
            self._start(i + WEIGHT_SLOTS - 1)
        self._copy(i).wait()
        _, dst, l, col = self.chunks[i]
        dst[l, :, col:col + WEIGHT_CHUNK_COLS] = self.stage[i % WEIGHT_SLOTS].astype(BF16)
        self.done += 1

    def need(self, dst, l, col):
        while self.done <= self.index[(id(dst), l, col)]:
            self.service()


class _Chain:
    def __init__(self, l, r0, rows, t, tm, R, need=None):
        self.l, self.r0, self.rows, self.t, self.tm, self.R = l, r0, rows, t, tm, R
        self.need = need if need is not None else (lambda dst, l, col: None)

    def _in(self, col, width):
        R = self.R
        self.need(R.w_in, self.l, col)
        return jnp.dot(R.h_buf[self.r0:self.r0 + self.rows, :], R.w_in[self.l, :, col:col + width],
                       preferred_element_type=F32)

    def p0(self):
        self.v = self._in(COL_V, W_B)
        self.a_in = self._in(COL_A_IN, W_A)

    def p1(self):
        self.a_gate = self._in(COL_A_GATE, W_A)
        self.u = self._in(COL_U, W_B)

    def p2(self):
        l, r0, rows, R = self.l, self.r0, self.rows, self.R
        n_chunks = rows // CHUNK
        a_in = self.a_in
        base = HIST + r0

        vn = _head_rmsnorm(self.v, _row(R.v_norm_g, l))
        self.cv_rows = vn[rows - CHUNK:, :]
        vn_b = vn.astype(BF16)

        a_hist, s2_hist, s4_hist, s8_hist = R.hists
        a_hist[base:base + rows, :] = a_in
        s2 = a_in + a_hist[base - 1:base - 1 + rows, :]
        s2_hist[base:base + rows, :] = s2[:, POOL_GROUP_DIM:]
        s4 = s2[:, POOL_GROUP_DIM:] + s2_hist[base - 2:base - 2 + rows, :]
        s4_hist[base:base + rows, :] = s4[:, POOL_GROUP_DIM:]
        s8 = s4[:, POOL_GROUP_DIM:] + s4_hist[base - 4:base - 4 + rows, :]
        s8_hist[base:base + rows, :] = s8[:, POOL_GROUP_DIM:]
        s16 = s8[:, POOL_GROUP_DIM:] + s8_hist[base - 8:base - 8 + rows, :]
        win_sums = (s2[:, :POOL_GROUP_DIM], s4[:, :POOL_GROUP_DIM], s8[:, :POOL_GROUP_DIM], s16)
        self.pool_rows = a_in[rows - POOL_BUF:, :]

        d_parts = []
        for g, w in enumerate(POOL_WINDOWS):
            sl = slice(g * POOL_GROUP_DIM, (g + 1) * POOL_GROUP_DIM)
            ssum = win_sums[g]
            if r0 == 0:
                pos1 = lax.broadcasted_iota(jnp.int32, (HIST, POOL_GROUP_DIM), 0) + self.t * self.tm + 1
                inv_head = 1.0 / jnp.minimum(pos1, w).astype(F32)
                mean = jnp.concatenate([ssum[:HIST] * inv_head, ssum[HIST:] * (1.0 / w)], axis=0)
            else:
                mean = ssum * (1.0 / w)
            d_parts.append((mean - a_in[:, sl]).astype(BF16))

        ws_b = _masked_ws(R.ws, l)
        s_heads = []
        for hh in range(N_HEADS_B):
            sl = slice(hh * HEAD_DIM_B, (hh + 1) * HEAD_DIM_B)
            rhs = jnp.concatenate([vn_b[c * CHUNK:(c + 1) * CHUNK, sl] for c in range(n_chunks)], axis=1)
            res = jnp.dot(ws_b[hh], rhs, preferred_element_type=F32)
            s_heads.append(jnp.concatenate(
                [res[:, c * HEAD_DIM_B:(c + 1) * HEAD_DIM_B] for c in range(n_chunks)], axis=0))
        self.s_heads = s_heads

        self.y_parts = [_dot(d_parts[g], R.w_pool[l, g]) for g in range(N_POOL_GROUPS)]
        self.b_gate = self._in(COL_B_GATE, W_B)

    def p3(self):
        l, rows, R = self.l, self.rows, self.R
        n_chunks = rows // CHUNK
        self.need(R.w_out, l, R.w_out.shape[2] - WEIGHT_CHUNK_COLS)
        a_out = ((jnp.concatenate(self.y_parts, axis=-1) * _row(R.pool_scale, l)) * _silu(self.a_gate)).astype(BF16)
        out_a = _dot(a_out, R.w_out[l, 0:W_A, :])
        s = jnp.concatenate(self.s_heads, axis=-1) + jnp.concatenate([R.bias[l]] * n_chunks, axis=0)
        b_out = ((self.u * s) * _silu(self.b_gate)).astype(BF16)
        self.out = out_a + _dot(b_out, R.w_out[l, W_A:, :])


def _per_head_lanes(vals):
    return jnp.concatenate([jnp.broadcast_to(v, (1, HEAD_DIM_B)) for v in vals], axis=-1)


def _sample_group(xs_ref, sp_ref, norm_g_ref, b_s_ref, fin_g, R, ys_ref, a_in_s_ref, vn_s_ref, before_pool):
    x = xs_ref[...]
    for l in range(DEPTH):
        h = _rmsnorm(x, _row(norm_g_ref, l)).astype(BF16)
        a_in = _dot(h, R.w_in[l, :, COL_A_IN:COL_A_IN + W_A])
        a_in_s_ref[l] = a_in
        a_gate = _dot(h, R.w_in[l, :, COL_A_GATE:COL_A_GATE + W_A])

        v = _dot(h, R.w_in[l, :, COL_V:COL_V + W_B])
        vn = _head_rmsnorm(v, _row(R.v_norm_g, l))
        vn_s_ref[l] = vn
        ws00 = _per_head_lanes([R.ws[l, hh, 0:1, 0:1] for hh in range(N_HEADS_B)])
        b0 = _per_head_lanes([b_s_ref[l, hh:hh + 1, 0:1] for hh in range(N_HEADS_B)])
        s = ws00 * vn + b0
        u = _dot(h, R.w_in[l, :, COL_U:COL_U + W_B])
        b_gate = _dot(h, R.w_in[l, :, COL_B_GATE:COL_B_GATE + W_B])
        b_out = (u * s) * _silu(b_gate)

        before_pool(l)
        a_parts = []
        for g, w in enumerate(POOL_WINDOWS):
            sl = slice(g * POOL_GROUP_DIM, (g + 1) * POOL_GROUP_DIM)
            ssum = a_in[:, sl]
            for k in range(1, w):
                ssum = ssum + sp_ref[l, POOL_BUF - k, :, sl]
            cnt = float(min(SAMPLE_PAST_LEN + 1, w))
            d = ssum / cnt - a_in[:, sl]
            yg = _dot(d, R.w_pool[l, g]) * _row(R.pool_scale, l)[:, sl]
            a_parts.append(yg * _silu(a_gate[:, sl]))
        a_out = jnp.concatenate(a_parts, axis=-1)

        x = x + (_dot(a_out, R.w_out[l, 0:W_A, :]) + _dot(b_out, R.w_out[l, W_A:, :]))
    ys_ref[...] = _rmsnorm(x, fin_g)


def _trunk_kernel(x_ref, xs_hbm, sp_hbm, norm_g_ref, w_in_hbm, w_pool_ref, pool_scale_ref, v_norm_g_ref,
                  ws_ref, b_s_ref, w_out_hbm, fin_g_ref,
                  y_ref, pool_ref, cv_ref, ys_hbm, pool_s_t_hbm, vn_s_hbm,
                  a_hist, s2_hist, s4_hist, s8_hist, a_carry, s2_carry, s4_carry, s8_carry,
                  h_buf, x_buf, w_in_bf, w_out_bf, bias_buf, sp_buf,
                  xs_buf, ys_buf, a_in_s_buf, vn_s_buf, stage, sems, io_sems, *, single_step):
    b = pl.program_id(0)
    t = pl.program_id(1)
    tm = x_ref.shape[1]
    rows = tm // ROW_BLOCKS
    R = _Refs(w_in=w_in_bf, w_pool=w_pool_ref, pool_scale=pool_scale_ref, v_norm_g=v_norm_g_ref, ws=ws_ref,
              bias=bias_buf, w_out=w_out_bf, hists=(a_hist, s2_hist, s4_hist, s8_hist), h_buf=h_buf)
    hists = R.hists
    carries = (a_carry, s2_carry, s4_carry, s8_carry)
    fin_g = fin_g_ref[...].reshape(1, D_MODEL)

    def tile_body(need=None, between_phases=None):
        for l in range(DEPTH):

            def x_rows(rs, l=l):
                return x_ref[0, rs, :] if l == 0 else x_buf[rs, :]

            def phase_done():
                if between_phases is not None:
                    between_phases()

            blocks = [slice(k * rows, (k + 1) * rows) for k in range(ROW_BLOCKS)]
            for rs in blocks:
                h_buf[rs, :] = _rmsnorm(x_rows(rs), _row(norm_g_ref, l)).astype(BF16)
            chains = [_Chain(l, k * rows, rows, t, tm, R, need) for k in range(ROW_BLOCKS)]
            for c in chains:
                c.p0()
            phase_done()
            for c in chains:
                c.p1()
            phase_done()
            for buf, carry in zip(hists, carries):
                buf[0:HIST, :] = carry[l]
            for c in chains:
                c.p2()
                phase_done()
            for buf, carry in zip(hists, carries):
                carry[l] = buf[tm:tm + HIST, :]
            pool_ref[l, 0] = chains[-1].pool_rows
            cv_ref[l, 0] = chains[-1].cv_rows
            for c, rs in zip(chains, blocks):
                c.p3()
                x_new = x_rows(rs) + c.out
                if l + 1 < DEPTH:
                    x_buf[rs, :] = x_new
                else:
                    y_ref[0, rs, :] = _rmsnorm(x_new, fin_g)
                phase_done()

    def zero_hist():
        for carry in carries:
            carry[...] = jnp.zeros(carry.shape, F32)

    def sample_out_copies():
        return {
            "keep": pltpu.make_async_copy(sp_buf.at[:, pl.ds(1, POOL_BUF - 1)],
                                          pool_s_t_hbm.at[:, pl.ds(0, POOL_BUF - 1)], io_sems.at[2]),
            "new_row": pltpu.make_async_copy(a_in_s_buf, pool_s_t_hbm.at[:, POOL_BUF - 1], io_sems.at[3]),
            "ys": pltpu.make_async_copy(ys_buf, ys_hbm.at[:, 0, :], io_sems.at[4]),
            "vn": pltpu.make_async_copy(vn_s_buf, vn_s_hbm.at[:, :, 0, :], io_sems.at[5]),
        }

    def wait_sample_outs():
        for cp in sample_out_copies().values():
            cp.wait()

    first = jnp.logical_and(b == 0, t == 0)

    @pl.when(first)
    def _():
        xs_copy = pltpu.make_async_copy(xs_hbm.at[:, 0, :], xs_buf, io_sems.at[0])
        sp_copy = pltpu.make_async_copy(sp_hbm, sp_buf, io_sems.at[1])
        outs = sample_out_copies()

        def start_in_copies():
            xs_copy.start()
            sp_copy.start()

        def before_pool(l):
            if l == 0:
                sp_copy.wait()
                outs["keep"].start()

        stream = _WeightStream(w_in_hbm, w_out_hbm, w_in_bf, w_out_bf, stage, sems, start_in_copies)
        stream.start()
        for l in range(DEPTH):
            for hh in range(N_HEADS_B):
                bias_buf[l, :, hh * HEAD_DIM_B:(hh + 1) * HEAD_DIM_B] = jnp.broadcast_to(
                    b_s_ref[l, hh:hh + 1, :], (CHUNK, CHUNK)).T
        zero_hist()

        tile_body(stream.need, stream.service)
        assert stream.done == len(stream.chunks)
        xs_copy.wait()
        _sample_group(xs_buf, sp_buf, norm_g_ref, b_s_ref, fin_g, R, ys_buf, a_in_s_buf, vn_s_buf, before_pool)
        for name in ("ys", "vn", "new_row"):
            outs[name].start()
        if single_step:
            wait_sample_outs()

    @pl.when(jnp.logical_not(first))
    def _():
        pl.when(t == 0)(zero_hist)
        tile_body()
        pl.when(b * pl.num_programs(1) + t == 1)(wait_sample_outs)


def _const_spec(shape):
    zeros = (0,) * len(shape)
    return pl.BlockSpec(shape, lambda b, t: zeros, pipeline_mode=pl.Buffered(1))


def _trunk_call(x, xs, sp_t, norm_g, w_in, w_pool, pool_scale, v_norm_g, w_s, b_s, w_out, fin_g):
    batch, seq, d = x.shape
    n_s = xs.shape[0]
    tm = ROW_TILE
    assert seq % tm == 0 and tm % (ROW_BLOCKS * CHUNK) == 0
    assert W_A == W_B == WEIGHT_CHUNK_COLS and d % WEIGHT_CHUNK_COLS == 0 and d == W_A + W_B
    hbm = pl.BlockSpec(memory_space=pl.ANY)
    args = (x, xs, sp_t, norm_g, w_in, w_pool, pool_scale, v_norm_g, w_s, b_s, w_out, fin_g)
    in_specs = []
    for a in args:
        if a is x:
            in_specs.append(pl.BlockSpec((1, tm, d), lambda b, t: (b, t, 0)))
        elif a is w_in or a is w_out or a is sp_t or a is xs:
            in_specs.append(hbm)
        else:
            in_specs.append(_const_spec(a.shape))
    return pl.pallas_call(
        functools.partial(_trunk_kernel, single_step=(batch * (seq // tm) == 1)),
        grid=(batch, seq // tm),
        in_specs=in_specs,
        out_specs=[
            pl.BlockSpec((1, tm, d), lambda b, t: (b, t, 0)),
            pl.BlockSpec((DEPTH, 1, POOL_BUF, W_A), lambda b, t: (0, b, 0, 0)),
            pl.BlockSpec((DEPTH, 1, CHUNK, W_B), lambda b, t: (0, b, 0, 0)),
            hbm, hbm, hbm,
        ],
        out_shape=[
            jax.ShapeDtypeStruct((batch, seq, d), F32),
            jax.ShapeDtypeStruct((DEPTH, batch, POOL_BUF, W_A), F32),
            jax.ShapeDtypeStruct((DEPTH, batch, CHUNK, W_B), F32),
            jax.ShapeDtypeStruct((n_s, 1, d), F32),
            jax.ShapeDtypeStruct((DEPTH, POOL_BUF, n_s, W_A), F32),
            jax.ShapeDtypeStruct((DEPTH, n_s, 1, W_B), F32),
        ],
        scratch_shapes=[
            pltpu.VMEM((HIST + tm + HIST_PAD, W_A), F32),
            pltpu.VMEM((HIST + tm + HIST_PAD, W_A - POOL_GROUP_DIM), F32),
            pltpu.VMEM((HIST + tm + HIST_PAD, W_A - 2 * POOL_GROUP_DIM), F32),
            pltpu.VMEM((HIST + tm + HIST_PAD, W_A - 3 * POOL_GROUP_DIM), F32),
            pltpu.VMEM((DEPTH, HIST, W_A), F32),
            pltpu.VMEM((DEPTH, HIST, W_A - POOL_GROUP_DIM), F32),
            pltpu.VMEM((DEPTH, HIST, W_A - 2 * POOL_GROUP_DIM), F32),
            pltpu.VMEM((DEPTH, HIST, W_A - 3 * POOL_GROUP_DIM), F32),
            pltpu.VMEM((tm, d), BF16),
            pltpu.VMEM((tm, d), F32),
            pltpu.VMEM((DEPTH, d, D_IN), BF16),
            pltpu.VMEM((DEPTH, W_A + W_B, d), BF16),
            pltpu.VMEM((DEPTH, CHUNK, W_B), F32),
            pltpu.VMEM((DEPTH, POOL_BUF, n_s, W_A), F32),
            pltpu.VMEM((n_s, d), F32),
            pltpu.VMEM((n_s, d), F32),
            pltpu.VMEM((DEPTH, n_s, W_A), F32),
            pltpu.VMEM((DEPTH, n_s, W_B), F32),
            pltpu.VMEM((WEIGHT_SLOTS, d, WEIGHT_CHUNK_COLS), F32),
            pltpu.SemaphoreType.DMA((WEIGHT_SLOTS,)),
            pltpu.SemaphoreType.DMA((6,)),
        ],
        compiler_params=pltpu.CompilerParams(
            dimension_semantics=("arbitrary", "arbitrary"),
            vmem_limit_bytes=VMEM_LIMIT_BYTES,
        ),
        name="trunk",
    )(*args)


def kernel(x_prompt, x_sample, state_pool, norm_g, w_in, w_pool, pool_scale, v_norm_g, w_s, b_s, w_out, final_norm_g):
    assert x_sample.shape[1] == 1 and state_pool.shape[2] == POOL_BUF
    sp_t = jnp.transpose(state_pool, (0, 2, 1, 3))
    y_prompt, pool_prompt, chunk_v_prompt, y_sample, pool_sample_t, chunk_v_sample = _trunk_call(
        x_prompt, x_sample, sp_t, norm_g, w_in, w_pool, pool_scale, v_norm_g, w_s, b_s, w_out, final_norm_g)
    pool_sample = jnp.transpose(pool_sample_t, (0, 2, 1, 3))
    return (y_prompt, y_sample, pool_prompt, pool_sample, chunk_v_prompt, chunk_v_sample)
```
